```python
import math
import jax, jax.numpy as jnp
from jax import lax
import numpy as np

D_MODEL = 1024
BATCH = 16
SEQ = 2048
DEPTH = 1

NSA_HEADS = 8
NSA_KV_GROUPS = 2
NSA_HEADS_PER_GROUP = NSA_HEADS // NSA_KV_GROUPS
HEAD_DIM = 64
NSA_Q_WIDTH = NSA_HEADS * HEAD_DIM
NSA_KV_WIDTH = NSA_KV_GROUPS * HEAD_DIM
CMP_BLOCK = 32
CMP_STRIDE = 16
CMP_HIDDEN = 2 * HEAD_DIM
SEL_BLOCK = 64
SEL_TOPN = 8
SEL_QBLOCK = 64
WINDOW = 256
WIN_QBLOCK = 128
ATTN_SCALE = HEAD_DIM ** -0.5

MLSTM_HEADS = 4
MLSTM_HEAD_DIM = 128
MLSTM_WIDTH = MLSTM_HEADS * MLSTM_HEAD_DIM
MLSTM_CHUNK = 128
CONV_WIDTH = 4

N_EXPERTS = 32
TOP_K = 4
D_EXPERT = D_MODEL
SWIGLU_LIMIT = 7.0
SWIGLU_ALPHA = 1.702
MOE_BLOCK = 256

RMS_EPS = 1e-5
LN_EPS = 1e-5

IN_SIZES = (NSA_Q_WIDTH, 6 * NSA_KV_WIDTH, 3 * NSA_HEADS, MLSTM_WIDTH, MLSTM_WIDTH, 2 * MLSTM_HEADS, 2 * D_MODEL)
IN_WIDTH = sum(IN_SIZES)
IN_SPLITS = [int(v) for v in np.cumsum(IN_SIZES)[:-1]]

kernel_name = 'hybrid_nsa_mlstm_moe_block'


def rms_norm(x, g):
    xf = x.astype(jnp.float32)
    y = xf * lax.rsqrt(jnp.mean(xf * xf, axis=-1, keepdims=True) + RMS_EPS)
    return (y * g.astype(jnp.float32)).astype(x.dtype)


def alibi_slopes(n_heads):
    return jnp.asarray(2.0 ** (-8.0 * np.arange(1, n_heads + 1) / n_heads), jnp.float32)


def masked_softmax(s, mask):
    s = jnp.where(mask, s, -jnp.inf)
    m = jnp.max(s, axis=-1, keepdims=True)
    m = jnp.where(jnp.isfinite(m), m, 0.0)
    p = jnp.where(mask, jnp.exp(s - m), 0.0)
    return p / jnp.maximum(jnp.sum(p, axis=-1, keepdims=True), 1e-30)


def compress_blocks(kv, pe, w1, w2):
    S = kv.shape[1]
    n_cmp = (S - CMP_BLOCK) // CMP_STRIDE + 1
    idx = np.arange(n_cmp)[:, None] * CMP_STRIDE + np.arange(CMP_BLOCK)[None, :]
    blocks = kv[:, idx] + pe[None, None, :, None, :]
    hid = jax.nn.gelu(jnp.einsum('bnlgd,lde->bnge', blocks, w1))
    return jnp.einsum('bnge,ed->bngd', hid, w2)


def compressed_branch(qg, kc, vc, slopes):
    S = qg.shape[1]
    n_cmp = kc.shape[1]
    t = np.arange(S)[:, None]
    start = np.arange(n_cmp)[None, :] * CMP_STRIDE
    mask = (start + CMP_BLOCK - 1) <= t
    dist = (t - start - (CMP_BLOCK - 1) / 2.0).astype(np.float32)
    s = jnp.einsum('bsghd,bngd->bghsn', qg, kc).astype(jnp.float32) * ATTN_SCALE
    s = s - slopes[None, :, :, None, None] * dist
    p = masked_softmax(s, mask)
    o = jnp.einsum('bghsn,bngd->bsghd', p.astype(vc.dtype), vc)
    return o, p


def selection_overlap(n_cmp, n_slc):
    cs = np.arange(n_cmp)[:, None] * CMP_STRIDE
    ss = np.arange(n_slc)[None, :] * SEL_BLOCK
    ov = np.clip(np.minimum(cs + CMP_BLOCK, ss + SEL_BLOCK) - np.maximum(cs, ss), 0, None)
    return jnp.asarray(ov / CMP_BLOCK, jnp.float32)


def selected_branch(qg, k_slc, v_slc, p_cmp, slopes):
    B, S, G, HPG, dh = qg.shape
    n_slc = S // SEL_BLOCK
    n_cmp = p_cmp.shape[-1]
    top_n = min(SEL_TOPN, n_slc)
    score = jnp.einsum('bghsn,nj->bgsj', p_cmp, selection_overlap(n_cmp, n_slc))
    t = np.arange(S)
    cur = t // SEL_BLOCK
    j = np.arange(n_slc)[None, :]
    forced = (j == cur[:, None]) | (j == 0)
    future = j > cur[:, None]
    score = jnp.where(forced, jnp.inf, jnp.where(future, -jnp.inf, score))
    _, idx = lax.top_k(score, top_n)
    valid = idx <= jnp.asarray(cur)[:, None]
    kb = k_slc.reshape(B, n_slc, SEL_BLOCK, G, dh).transpose(0, 3, 1, 2, 4)
    vb = v_slc.reshape(B, n_slc, SEL_BLOCK, G, dh).transpose(0, 3, 1, 2, 4)
    nq = S // SEL_QBLOCK

    def to_chunks(a, axis):
        shp = a.shape
        a = a.reshape(shp[:axis] + (nq, SEL_QBLOCK) + shp[axis + 1:])
        return jnp.moveaxis(a, axis, 0)

    q_c = to_chunks(qg, 1)
    idx_c = to_chunks(idx, 2)
    val_c = to_chunks(valid, 2)
    t_c = jnp.arange(S, dtype=jnp.int32).reshape(nq, SEL_QBLOCK)
    gather = jax.vmap(jax.vmap(lambda blocks, ix: blocks[ix]))
    offs = jnp.arange(SEL_BLOCK, dtype=jnp.int32)
    flat = (B, G, HPG, SEL_QBLOCK, top_n * SEL_BLOCK)

    def one_chunk(args):
        qc, ic, vc_, tc = args
        kg = gather(kb, ic)
        vg = gather(vb, ic)
        dist = tc[None, None, :, None, None] - (ic[..., None] * SEL_BLOCK + offs)
        mask = vc_[..., None] & (dist >= 0)
        s = jnp.einsum('bqghd,bgqnld->bghqnl', qc, kg).astype(jnp.float32) * ATTN_SCALE
        s = s - slopes[None, :, :, None, None, None] * dist[:, :, None].astype(jnp.float32)
        m6 = jnp.broadcast_to(mask[:, :, None], s.shape)
        p = masked_softmax(s.reshape(flat), m6.reshape(flat)).reshape(s.shape)
        return jnp.einsum('bghqnl,bgqnld->bqghd', p.astype(vg.dtype), vg)

    o = lax.map(one_chunk, (q_c, idx_c, val_c, t_c))
    return jnp.moveaxis(o, 0, 1).reshape(B, S, G, HPG, dh)


def window_branch(qg, k_win, v_win, slopes):
    B, S, G, HPG, dh = qg.shape
    nb = S // WIN_QBLOCK
    span = WINDOW + WIN_QBLOCK
    pad = ((0, 0), (WINDOW, 0), (0, 0), (0, 0))
    kp = jnp.pad(k_win, pad)
    vp = jnp.pad(v_win, pad)
    idx = np.arange(nb)[:, None] * WIN_QBLOCK + np.arange(span)[None, :]
    kb = kp[:, idx]
    vb = vp[:, idx]
    qb = qg.reshape(B, nb, WIN_QBLOCK, G, HPG, dh)
    qpos = np.arange(WIN_QBLOCK)[:, None]
    kpos = np.arange(span)[None, :]
    dist = qpos - kpos + WINDOW
    kabs = np.arange(nb)[:, None, None] * WIN_QBLOCK - WINDOW + kpos[None]
    mask = (dist >= 0) & (dist < WINDOW) & (kabs >= 0)
    s = jnp.einsum('biqghd,bikgd->bghiqk', qb, kb).astype(jnp.float32) * ATTN_SCALE
    s = s - slopes[None, :, :, None, None, None] * dist.astype(np.float32)
    p = masked_softmax(s, mask)
    o = jnp.einsum('bghiqk,bikgd->biqghd', p.astype(vb.dtype), vb)
    return o.reshape(B, S, G, HPG, dh)


def mlstm_branch(x_m, i_pre, f_pre, o_pre, conv_w, conv_b, wq, wk, wv, f_bias, norm_g):
    B, S, C = x_m.shape
    H, dh = MLSTM_HEADS, MLSTM_HEAD_DIM
    xc = lax.conv_general_dilated(x_m, conv_w[:, None, :], window_strides=(1,), padding=[(CONV_WIDTH - 1, 0)],
                                  dimension_numbers=('NWC', 'WIO', 'NWC'), feature_group_count=C)
    xc = jax.nn.silu(xc + conv_b)
    xch = xc.reshape(B, S, H, dh)
    xmh = x_m.reshape(B, S, H, dh)
    q = jnp.einsum('bshd,hde->bhse', xch, wq).astype(jnp.float32)
    k = jnp.einsum('bshd,hde->bhse', xch, wk).astype(jnp.float32) / math.sqrt(dh)
    v = jnp.einsum('bshd,hde->bhse', xmh, wv).astype(jnp.float32)
    ig = i_pre.astype(jnp.float32).transpose(0, 2, 1)
    lf = jax.nn.log_sigmoid((f_pre + f_bias).astype(jnp.float32)).transpose(0, 2, 1)
    L = MLSTM_CHUNK
    nc = S // L

    def chunks(a):
        a = a.reshape(a.shape[:2] + (nc, L) + a.shape[3:])
        return jnp.moveaxis(a, 2, 0)

    causal = np.tril(np.ones((L, L), dtype=bool))

    def step(carry, inp):
        Cm, nm, mm = carry
        qc, kc, vc, ic, fc = inp
        F = jnp.cumsum(fc, axis=-1)
        logD = jnp.where(causal, F[..., :, None] - F[..., None, :] + ic[..., None, :], -jnp.inf)
        inter = F + mm[..., None]
        m_t = jnp.maximum(inter, jnp.max(logD, axis=-1))
        Dm = jnp.exp(logD - m_t[..., None])
        wi = jnp.exp(inter - m_t)
        qk = jnp.einsum('bhtd,bhsd->bhts', qc, kc) * Dm
        num = wi[..., None] * jnp.einsum('bhtd,bhde->bhte', qc, Cm) + jnp.einsum('bhts,bhse->bhte', qk, vc)
        den = wi * jnp.einsum('bhtd,bhd->bht', qc, nm) + jnp.sum(qk, axis=-1)
        h = num / jnp.maximum(jnp.abs(den), jnp.exp(-m_t))[..., None]
        FL = F[..., -1]
        logw = FL[..., None] - F + ic
        m_new = jnp.maximum(FL + mm, jnp.max(logw, axis=-1))
        decay = jnp.exp(FL + mm - m_new)
        w = jnp.exp(logw - m_new[..., None])
        C_new = decay[..., None, None] * Cm + jnp.einsum('bhs,bhsd,bhse->bhde', w, kc, vc)
        n_new = decay[..., None] * nm + jnp.einsum('bhs,bhsd->bhd', w, kc)
        return (C_new, n_new, m_new), h

    init = (jnp.zeros((B, H, dh, dh), jnp.float32), jnp.zeros((B, H, dh), jnp.float32), jnp.zeros((B, H), jnp.float32))
    _, hs = lax.scan(step, init, (chunks(q), chunks(k), chunks(v), chunks(ig), chunks(lf)))
    h = jnp.moveaxis(hs, 0, 2).reshape(B, H, S, dh).transpose(0, 2, 1, 3)
    mu = jnp.mean(h, axis=-1, keepdims=True)
    var = jnp.mean(jnp.square(h - mu), axis=-1, keepdims=True)
    hn = ((h - mu) * lax.rsqrt(var + LN_EPS)).reshape(B, S, C) * norm_g.astype(jnp.float32)
    return (jax.nn.sigmoid(o_pre.astype(jnp.float32)) * hn).astype(x_m.dtype)


def moe_ffn(h, router_w, router_b, w_up, b_up, w_down, b_down):
    B, S, D = h.shape
    T = B * S
    R = T * TOP_K
    xt = h.reshape(T, D)
    logits = (xt @ router_w + router_b).astype(jnp.float32)
    top_v, top_e = lax.top_k(logits, TOP_K)
    wts = jax.nn.softmax(top_v, axis=-1)
    e_flat = top_e.reshape(R)
    onehot = jax.nn.one_hot(e_flat, N_EXPERTS, dtype=jnp.int32)
    rank = jnp.sum((jnp.cumsum(onehot, axis=0) - 1) * onehot, axis=-1)
    counts = jnp.sum(onehot, axis=0)
    padded = (counts + MOE_BLOCK - 1) // MOE_BLOCK * MOE_BLOCK
    pends = jnp.cumsum(padded)
    dest = (pends - padded)[e_flat] + rank
    n_blocks = (R + N_EXPERTS * (MOE_BLOCK - 1) + MOE_BLOCK - 1) // MOE_BLOCK
    P = n_blocks * MOE_BLOCK
    xpad = jnp.zeros((P, D), h.dtype).at[dest].set(jnp.repeat(xt, TOP_K, axis=0))
    block_start = jnp.arange(n_blocks, dtype=jnp.int32) * MOE_BLOCK
    block_e = jnp.minimum(jnp.searchsorted(pends, block_start, side='right'), N_EXPERTS - 1)

    def expert_block(args):
        xb, e = args
        gu = xb @ w_up[e] + b_up[e]
        g, lin = jnp.split(gu, 2, axis=-1)
        g = jnp.minimum(g, SWIGLU_LIMIT)
        lin = jnp.clip(lin, -SWIGLU_LIMIT, SWIGLU_LIMIT)
        a = g * jax.nn.sigmoid(SWIGLU_ALPHA * g) * (lin + 1.0)
        return a @ w_down[e] + b_down[e]

    ypad = lax.map(expert_block, (xpad.reshape(n_blocks, MOE_BLOCK, D), block_e)).reshape(P, D)
    y_rows = ypad[dest].reshape(T, TOP_K, D)
    out = jnp.einsum('tk,tkd->td', wts.astype(y_rows.dtype), y_rows)
    return out.reshape(B, S, D)


def hybrid_layer(x, c, ada_w, ada_b, norm1_g, w_in, b_in, cmp_pe_k, cmp_w1_k, cmp_w2_k, cmp_pe_v, cmp_w1_v, cmp_w2_v,
                 ml_conv_w, ml_conv_b, ml_wq, ml_wk, ml_wv, ml_f_bias, ml_norm_g, proj_a, proj_b, w_out, norm2_g,
                 router_w, router_b, exp_w_up, exp_b_up, exp_w_down, exp_b_down):
    B, S, D = x.shape
    G, HPG = NSA_KV_GROUPS, NSA_HEADS_PER_GROUP
    mod = jax.nn.silu(c) @ ada_w + ada_b
    shift1, scale1, gate1, shift2, scale2, gate2 = jnp.split(mod[:, None, :], 6, axis=-1)
    h = rms_norm(x, norm1_g) * (1.0 + scale1) + shift1
    proj = h @ w_in + b_in
    q, kv, g_nsa, x_m, o_pre, if_pre, g_merge = jnp.split(proj, IN_SPLITS, axis=-1)
    qg = q.reshape(B, S, G, HPG, HEAD_DIM)
    k_cmp, v_cmp, k_slc, v_slc, k_win, v_win = [a.reshape(B, S, G, HEAD_DIM) for a in jnp.split(kv, 6, axis=-1)]
    slopes = alibi_slopes(NSA_HEADS).reshape(G, HPG)
    kc = compress_blocks(k_cmp, cmp_pe_k, cmp_w1_k, cmp_w2_k)
    vc = compress_blocks(v_cmp, cmp_pe_v, cmp_w1_v, cmp_w2_v)
    o_cmp, p_cmp = compressed_branch(qg, kc, vc, slopes)
    o_slc = selected_branch(qg, k_slc, v_slc, p_cmp, slopes)
    o_win = window_branch(qg, k_win, v_win, slopes)
    bg = jax.nn.sigmoid(g_nsa).reshape(B, S, 3, G, HPG, 1)
    o_nsa = (bg[:, :, 0] * o_cmp + bg[:, :, 1] * o_slc + bg[:, :, 2] * o_win).reshape(B, S, NSA_Q_WIDTH)
    i_pre, f_pre = jnp.split(if_pre, 2, axis=-1)
    y_ml = mlstm_branch(x_m, i_pre, f_pre, o_pre, ml_conv_w, ml_conv_b, ml_wq, ml_wk, ml_wv, ml_f_bias, ml_norm_g)
    g_a, g_b = jnp.split(jax.nn.sigmoid(g_merge), 2, axis=-1)
    mixed = (g_a * (o_nsa @ proj_a) + g_b * (y_ml @ proj_b)) @ w_out
    x = x + gate1 * mixed
    h2 = rms_norm(x, norm2_g) * (1.0 + scale2) + shift2
    x = x + gate2 * moe_ffn(h2, router_w, router_b, exp_w_up, exp_b_up, exp_w_down, exp_b_down)
    return x


def setup_inputs(seed: int = 0) -> dict:
    key = jax.random.key(seed)
    keys = iter(jax.random.split(key, 40))

    def nrm(shape, scale):
        return jax.random.normal(next(keys), shape, jnp.float32) * scale

    L, D = DEPTH, D_MODEL
    return {
        'x': nrm((BATCH, SEQ, D), 1.0),
        'c': nrm((BATCH, D), 1.0),
        'ada_w': nrm((L, D, 6 * D), 0.5 * D ** -0.5),
        'ada_b': nrm((L, 6 * D), 0.02),
        'norm1_g': 1.0 + nrm((L, D), 0.05),
        'w_in': nrm((L, D, IN_WIDTH), D ** -0.5),
        'b_in': nrm((L, IN_WIDTH), 0.02),
        'cmp_pe_k': nrm((L, CMP_BLOCK, HEAD_DIM), 0.02),
        'cmp_w1_k': nrm((L, CMP_BLOCK, HEAD_DIM, CMP_HIDDEN), (CMP_BLOCK * HEAD_DIM) ** -0.5),
        'cmp_w2_k': nrm((L, CMP_HIDDEN, HEAD_DIM), CMP_HIDDEN ** -0.5),
        'cmp_pe_v': nrm((L, CMP_BLOCK, HEAD_DIM), 0.02),
        'cmp_w1_v': nrm((L, CMP_BLOCK, HEAD_DIM, CMP_HIDDEN), (CMP_BLOCK * HEAD_DIM) ** -0.5),
        'cmp_w2_v': nrm((L, CMP_HIDDEN, HEAD_DIM), CMP_HIDDEN ** -0.5),
        'ml_conv_w': nrm((L, CONV_WIDTH, MLSTM_WIDTH), CONV_WIDTH ** -0.5),
        'ml_conv_b': nrm((L, MLSTM_WIDTH), 0.02),
        'ml_wq': nrm((L, MLSTM_HEADS, MLSTM_HEAD_DIM, MLSTM_HEAD_DIM), MLSTM_HEAD_DIM ** -0.5),
        'ml_wk': nrm((L, MLSTM_HEADS, MLSTM_HEAD_DIM, MLSTM_HEAD_DIM), MLSTM_HEAD_DIM ** -0.5),
        'ml_wv': nrm((L, MLSTM_HEADS, MLSTM_HEAD_DIM, MLSTM_HEAD_DIM), MLSTM_HEAD_DIM ** -0.5),
        'ml_f_bias': jnp.linspace(3.0, 6.0, MLSTM_HEADS, dtype=jnp.float32)[None, :] + nrm((L, MLSTM_HEADS), 0.1),
        'ml_norm_g': 1.0 + nrm((L, MLSTM_WIDTH), 0.05),
        'proj_a': nrm((L, NSA_Q_WIDTH, D), NSA_Q_WIDTH ** -0.5),
        'proj_b': nrm((L, MLSTM_WIDTH, D), MLSTM_WIDTH ** -0.5),
        'w_out': nrm((L, D, D), D ** -0.5),
        'norm2_g': 1.0 + nrm((L, D), 0.05),
        'router_w': nrm((L, D, N_EXPERTS), D ** -0.5),
        'router_b': nrm((L, N_EXPERTS), 0.01),
        'exp_w_up': nrm((L, N_EXPERTS, D, 2 * D_EXPERT), D ** -0.5),
        'exp_b_up': nrm((L, N_EXPERTS, 2 * D_EXPERT), 0.02),
        'exp_w_down': nrm((L, N_EXPERTS, D_EXPERT, D), D_EXPERT ** -0.5),
        'exp_b_down': nrm((L, N_EXPERTS, D), 0.02),
        'final_g': 1.0 + nrm((D,), 0.05),
    }


def reference(x, c, ada_w, ada_b, norm1_g, w_in, b_in, cmp_pe_k, cmp_w1_k, cmp_w2_k, cmp_pe_v, cmp_w1_v, cmp_w2_v,
              ml_conv_w, ml_conv_b, ml_wq, ml_wk, ml_wv, ml_f_bias, ml_norm_g, proj_a, proj_b, w_out, norm2_g,
              router_w, router_b, exp_w_up, exp_b_up, exp_w_down, exp_b_down, final_g):
    for l in range(DEPTH):
        x = hybrid_layer(x, c, ada_w[l], ada_b[l], norm1_g[l], w_in[l], b_in[l],
                         cmp_pe_k[l], cmp_w1_k[l], cmp_w2_k[l], cmp_pe_v[l], cmp_w1_v[l], cmp_w2_v[l],
                         ml_conv_w[l], ml_conv_b[l], ml_wq[l], ml_wk[l], ml_wv[l], ml_f_bias[l], ml_norm_g[l],
                         proj_a[l], proj_b[l], w_out[l], norm2_g[l],
                         router_w[l], router_b[l], exp_w_up[l], exp_b_up[l], exp_w_down[l], exp_b_down[l])
    return rms_norm(x, final_g)
```

```python
import functools
import math

import numpy as np
import jax
import jax.numpy as jnp
from jax import lax
from jax.experimental import pallas as pl
from jax.experimental.pallas import tpu as pltpu

F32 = jnp.float32
BF16 = jnp.bfloat16
I32 = jnp.int32

D_MODEL = 1024
N_HEADS = 8
N_GROUPS = 2
HPG = N_HEADS // N_GROUPS
HEAD_DIM = 64
CMP_BLOCK = 32
CMP_STRIDE = 16
CMP_HIDDEN = 128
SEL_BLOCK = 64
SEL_TOPN = 8
WINDOW = 256
ATTN_SCALE = HEAD_DIM ** -0.5
ML_HEADS = 4
ML_DH = 128
ML_WIDTH = ML_HEADS * ML_DH
ML_CHUNK = 128
CONV_W = 4
N_EXPERTS = 32
TOP_K = 4
SWIGLU_LIMIT = 7.0
SWIGLU_ALPHA = 1.702
RMS_EPS = 1e-5
LN_EPS = 1e-5

LANES = 128
NEG = -1e30
VMEM_LIMIT = 56 * 1024 * 1024

QP_W = N_HEADS * LANES
C_Q = 0
C_KC = C_Q + QP_W
C_VC = C_KC + LANES
C_KV4 = C_VC + LANES
C_XM = C_KV4 + 4 * LANES
C_OP = C_XM + ML_WIDTH
C_GM = C_OP + ML_WIDTH
C_SM = C_GM + 2 * D_MODEL
IN_W = C_SM + LANES


def _cparams(sem):
    return pltpu.CompilerParams(dimension_semantics=sem, vmem_limit_bytes=VMEM_LIMIT)


def _adaln_kernel(c_ref, w_ref, b_ref, o_ref):
    c = c_ref[...]
    sc = (c * jax.nn.sigmoid(c)).astype(BF16)
    o_ref[...] = jnp.dot(sc, w_ref[...].astype(BF16), preferred_element_type=F32) + b_ref[...]


def _adaln(c, ada_w, ada_b):
    B = c.shape[0]
    tn = 1024
    n = ada_w.shape[1]
    return pl.pallas_call(
        _adaln_kernel,
        grid=(n // tn,),
        in_specs=[pl.BlockSpec((B, D_MODEL), lambda j: (0, 0)),
                  pl.BlockSpec((D_MODEL, tn), lambda j: (0, j)),
                  pl.BlockSpec((1, tn), lambda j: (0, j))],
        out_specs=pl.BlockSpec((B, tn), lambda j: (0, j)),
        out_shape=jax.ShapeDtypeStruct((B, n), F32),
        compiler_params=_cparams(("arbitrary",)),
        name="adaln",
    )(c, ada_w, ada_b.reshape(1, n))


def _prep_w_in(w_in, b_in):
    def pad_q(a):
        lead = a.shape[:-1]
        a = a.reshape(lead + (N_HEADS, HEAD_DIM))
        z = jnp.zeros_like(a[..., :HPG, :])
        lo = jnp.concatenate([a[..., :HPG, :], z], axis=-1)
        hi = jnp.concatenate([z, a[..., HPG:, :]], axis=-1)
        return jnp.concatenate([lo, hi], axis=-2).reshape(lead + (QP_W,))

    o = 0
    q = lax.slice_in_dim(w_in, 0, 512, axis=-1)
    kv = lax.slice_in_dim(w_in, 512, 1280, axis=-1)
    gn = lax.slice_in_dim(w_in, 1280, 1304, axis=-1)
    xm = lax.slice_in_dim(w_in, 1304, 1816, axis=-1)
    op = lax.slice_in_dim(w_in, 1816, 2328, axis=-1)
    fi = lax.slice_in_dim(w_in, 2328, 2336, axis=-1)
    gm = lax.slice_in_dim(w_in, 2336, 4384, axis=-1)
    bq, bkv, bgn, bxm, bop, bfi, bgm = (b_in[0:512], b_in[512:1280], b_in[1280:1304], b_in[1304:1816],
                                        b_in[1816:2328], b_in[2328:2336], b_in[2336:4384])
    zs = jnp.zeros(w_in.shape[:-1] + (LANES - 32,), w_in.dtype)
    w = jnp.concatenate([pad_q(q), kv, xm, op, gm, gn, fi, zs], axis=-1)
    b = jnp.concatenate([pad_q(bq), bkv, bxm, bop, bgm, bgn, bfi, jnp.zeros((LANES - 32,), b_in.dtype)], axis=-1)
    return w.astype(BF16), b.reshape(1, IN_W)


def _inproj_kernel(x_ref, mod_ref, g_ref, w_ref, b_ref,
                   q_ref, kc_ref, vc_ref, kv_ref, xm_ref, op_ref, gm_ref, sm_ref):
    x = x_ref[0]
    ms = jnp.mean(x * x, axis=-1, keepdims=True)
    y = x * lax.rsqrt(ms + RMS_EPS) * g_ref[...]
    h = (y * (1.0 + mod_ref[0, 1:2, :]) + mod_ref[0, 0:1, :]).astype(BF16)

    def seg(lo, hi):
        return jnp.dot(h, w_ref[:, lo:hi], preferred_element_type=F32) + b_ref[:, lo:hi]

    q_ref[0] = seg(C_Q, C_KC).astype(BF16)
    kc_ref[0] = seg(C_KC, C_VC).astype(BF16)
    vc_ref[0] = seg(C_VC, C_KV4).astype(BF16)
    kv_ref[0] = seg(C_KV4, C_XM).astype(BF16)
    xm_ref[0] = seg(C_XM, C_OP)
    op_ref[0] = seg(C_OP, C_GM).astype(BF16)
    gm_ref[0] = seg(C_GM, C_SM).astype(BF16)
    sm_ref[0] = seg(C_SM, IN_W)


def _inproj(x, mod, norm_g, w, b, ts=512):
    B, S, _ = x.shape
    widths = [(QP_W, BF16), (LANES, BF16), (LANES, BF16), (4 * LANES, BF16), (ML_WIDTH, F32),
              (ML_WIDTH, BF16), (2 * D_MODEL, BF16), (LANES, F32)]
    return pl.pallas_call(
        _inproj_kernel,
        grid=(B, S // ts),
        in_specs=[pl.BlockSpec((1, ts, D_MODEL), lambda b_, s: (b_, s, 0)),
                  pl.BlockSpec((1, 6, D_MODEL), lambda b_, s: (b_, 0, 0)),
                  pl.BlockSpec((1, D_MODEL), lambda b_, s: (0, 0)),
                  pl.BlockSpec((D_MODEL, IN_W), lambda b_, s: (0, 0)),
                  pl.BlockSpec((1, IN_W), lambda b_, s: (0, 0))],
        out_specs=[pl.BlockSpec((1, ts, wd), lambda b_, s: (b_, s, 0)) for wd, _ in widths],
        out_shape=[jax.ShapeDtypeStruct((B, S, wd), dt) for wd, dt in widths],
        compiler_params=_cparams(("arbitrary", "arbitrary")),
        name="inproj",
    )(x, mod, norm_g.reshape(1, D_MODEL), w, b)


def _prep_compress(pe, w1, w2):
    w1r = w1.reshape(2, CMP_STRIDE, HEAD_DIM, CMP_HIDDEN)
    z = jnp.zeros_like(w1r[0])
    cols = []
    for g in range(N_GROUPS):
        for a in range(2):
            parts = [w1r[a] if gg == g else z for gg in range(N_GROUPS)]
            cols.append(jnp.concatenate(parts, axis=1).reshape(CMP_STRIDE * LANES, CMP_HIDDEN))
    wk = jnp.concatenate(cols, axis=1)
    per = pe.reshape(2, CMP_STRIDE, HEAD_DIM)
    pe_rows = jnp.concatenate([per, per], axis=-1).reshape(2, CMP_STRIDE * LANES)
    pe_rows = jnp.concatenate([pe_rows, jnp.zeros((6, CMP_STRIDE * LANES), pe.dtype)], axis=0)
    z2 = jnp.zeros_like(w2)
    w2p = jnp.stack([jnp.concatenate([w2, z2], axis=1), jnp.concatenate([z2, w2], axis=1)])
    return wk.astype(BF16), pe_rows.astype(BF16), w2p.astype(BF16)


def _compress_kernel(ak_ref, av_ref, wk_ref, pe_ref, w2_ref, ok_ref, ov_ref):
    for i, (a_ref, o_ref) in enumerate(((ak_ref, ok_ref), (av_ref, ov_ref))):
        y = jnp.dot(a_ref[0], wk_ref[i], preferred_element_type=F32)
        yp = jnp.dot(pe_ref[i], wk_ref[i], preferred_element_type=F32)
        out = jnp.zeros((LANES, LANES), F32)
        for g in range(N_GROUPS):
            ca, cb = 2 * g * LANES, (2 * g + 1) * LANES
            bias = yp[0:1, ca:ca + LANES] + yp[1:2, cb:cb + LANES]
            hid = y[:, ca:ca + LANES] + pltpu.roll(y[:, cb:cb + LANES], LANES - 1, 0) + bias
            hid = jax.nn.gelu(hid, approximate=True).astype(BF16)
            out = out + jnp.dot(hid, w2_ref[i, g], preferred_element_type=F32)
        o_ref[0] = out.astype(BF16)


def _compress(kc_in, vc_in, wk, pe, w2):
    B, S, _ = kc_in.shape
    n = S // CMP_STRIDE
    kw = CMP_STRIDE * LANES
    a_spec = pl.BlockSpec((1, n, kw), lambda b_: (b_, 0, 0))
    o_spec = pl.BlockSpec((1, n, LANES), lambda b_: (b_, 0, 0))
    full = lambda shp: pl.BlockSpec(shp, lambda b_: (0,) * len(shp))
    return pl.pallas_call(
        _compress_kernel,
        grid=(B,),
        in_specs=[a_spec, a_spec, full(wk.shape), full(pe.shape), full(w2.shape)],
        out_specs=[o_spec, o_spec],
        out_shape=[jax.ShapeDtypeStruct((B, n, LANES), BF16)] * 2,
        compiler_params=_cparams(("arbitrary",)),
        name="compress",
    )(kc_in.reshape(B, n, kw), vc_in.reshape(B, n, kw), wk, pe, w2)


def _nsa_consts(S):
    n_cmp = (S - CMP_BLOCK) // CMP_STRIDE + 1
    n_slc = S // SEL_BLOCK
    cs = np.arange(n_cmp)[:, None] * CMP_STRIDE
    ss = np.arange(n_slc)[None, :] * SEL_BLOCK
    ov = np.clip(np.minimum(cs + CMP_BLOCK, ss + SEL_BLOCK) - np.maximum(cs, ss), 0, None) / CMP_BLOCK
    ovp = np.zeros((LANES, LANES), np.float32)
    ovp[:n_cmp, :n_slc] = ov
    nkt = S // LANES
    ex = np.zeros((nkt, LANES, LANES), np.float32)
    for kt in range(nkt):
        for l in range(LANES):
            ex[kt, (kt * LANES + l) // SEL_BLOCK, l] = 1.0
    return jnp.asarray(ovp, BF16), jnp.asarray(ex, BF16)


def _nsa_kernel(q_ref, kc_ref, vc_ref, ks_ref, vs_ref, kw_ref, vw_ref, gate_ref, ov_ref, ex_ref,
                o_ref, m_sc, l_sc, acc_sc, out_sc, *, tq):
    g = pl.program_id(1)
    t0 = pl.program_id(2) * tq
    tpos = t0 + lax.broadcasted_iota(I32, (tq, LANES), 0)
    lane = lax.broadcasted_iota(I32, (tq, LANES), 1)
    tposf = tpos.astype(F32)
    lanef = lane.astype(F32)
    slopes = [jnp.where(g == 0, 2.0 ** -(hh + 1), 2.0 ** -(HPG + hh + 1)).astype(F32) for hh in range(HPG)]
    nt = (((1,), (1,)), ((), ()))

    def gate(br, hh):
        c0, c1 = br * N_HEADS + hh, br * N_HEADS + HPG + hh
        col = jnp.where(g == 0, gate_ref[0, :, c0:c0 + 1], gate_ref[0, :, c1:c1 + 1])
        return jax.nn.sigmoid(col)

    kc, vc = kc_ref[0], vc_ref[0]
    cstart = lane * CMP_STRIDE
    cmask = (cstart + (CMP_BLOCK - 1)) <= tpos
    cdist = tposf - cstart.astype(F32) - (CMP_BLOCK - 1) / 2.0
    psum = jnp.zeros((tq, LANES), F32)
    for hh in range(HPG):
        qh = q_ref[0, :, hh * LANES:(hh + 1) * LANES]
        s = lax.dot_general(qh, kc, nt, preferred_element_type=F32) * ATTN_SCALE - slopes[hh] * cdist
        s = jnp.where(cmask, s, NEG)
        m = jnp.max(s, axis=-1, keepdims=True)
        p = jnp.where(cmask, jnp.exp(s - m), 0.0)
        p = p / jnp.maximum(jnp.sum(p, axis=-1, keepdims=True), 1e-30)
        psum = psum + p
        out_sc[:, hh * LANES:(hh + 1) * LANES] = gate(0, hh) * jnp.dot(p.astype(BF16), vc, preferred_element_type=F32)

    ph = psum.astype(BF16)
    plo = (psum - ph.astype(F32)).astype(BF16)
    imp = (jnp.dot(ph, ov_ref[...], preferred_element_type=F32)
           + jnp.dot(plo, ov_ref[...], preferred_element_type=F32))
    cur = tpos // SEL_BLOCK
    sc = jnp.where((lane == cur) | (lane == 0), -NEG, jnp.where(lane > cur, NEG, imp))
    sel = jnp.zeros((tq, LANES), F32)
    for _ in range(SEL_TOPN):
        mx = jnp.max(sc, axis=-1, keepdims=True)
        idx = jnp.min(jnp.where(sc == mx, lanef, 1e9), axis=-1, keepdims=True)
        hit = lanef == idx
        sel = jnp.where(hit, 1.0, sel)
        sc = jnp.where(hit, -3e38, sc)
    sel = jnp.where(lane <= cur, sel, 0.0).astype(BF16)

    m_sc[...] = jnp.full(m_sc.shape, NEG, F32)
    l_sc[...] = jnp.zeros(l_sc.shape, F32)
    acc_sc[...] = jnp.zeros(acc_sc.shape, F32)

    def update(slot, hh, k_t, v_t, valid, distf):
        i = slot * HPG + hh
        qh = q_ref[0, :, hh * LANES:(hh + 1) * LANES]
        s = lax.dot_general(qh, k_t, nt, preferred_element_type=F32) * ATTN_SCALE - slopes[hh] * distf
        s = jnp.where(valid, s, NEG)
        m_old = m_sc[i]
        m_new = jnp.maximum(m_old, jnp.max(s, axis=-1, keepdims=True))
        alpha = jnp.exp(m_old - m_new)
        p = jnp.where(valid, jnp.exp(s - m_new), 0.0)
        l_sc[i] = alpha * l_sc[i] + jnp.sum(p, axis=-1, keepdims=True)
        acc_sc[i] = alpha * acc_sc[i] + jnp.dot(p.astype(BF16), v_t, preferred_element_type=F32)
        m_sc[i] = m_new

    def slc_body(kt, carry):
        r0 = pl.multiple_of(kt * LANES, LANES)
        k_t = ks_ref[0, pl.ds(r0, LANES), :]
        v_t = vs_ref[0, pl.ds(r0, LANES), :]
        picked = jnp.dot(sel, ex_ref[kt], preferred_element_type=F32)
        dist = tpos - (kt * LANES + lane)
        valid = (picked > 0.5) & (dist >= 0)
        distf = dist.astype(F32)
        for hh in range(HPG):
            update(0, hh, k_t, v_t, valid, distf)
        return carry

    lax.fori_loop(0, (t0 + tq) // LANES, slc_body, 0)

    for r in range(tq // LANES + WINDOW // LANES):
        kt = t0 // LANES - WINDOW // LANES + r
        r0 = pl.multiple_of(jnp.maximum(kt, 0) * LANES, LANES)
        k_t = kw_ref[0, pl.ds(r0, LANES), :]
        v_t = vw_ref[0, pl.ds(r0, LANES), :]
        spos = kt * LANES + lane
        dist = tpos - spos
        valid = (dist >= 0) & (dist < WINDOW) & (spos >= 0)
        distf = dist.astype(F32)
        for hh in range(HPG):
            update(1, hh, k_t, v_t, valid, distf)

    for hh in range(HPG):
        o = out_sc[:, hh * LANES:(hh + 1) * LANES]
        o = o + gate(1, hh) * (acc_sc[hh] / l_sc[hh])
        o = o + gate(2, hh) * (acc_sc[HPG + hh] / l_sc[HPG + hh])
        o_ref[0, :, hh * LANES:(hh + 1) * LANES] = o.astype(BF16)


def _nsa(qp, kc, vc, kv4, small, ov, ex, tq=256):
    B, S, _ = qp.shape
    gw = HPG * LANES
    seq = lambda j: pl.BlockSpec((1, S, LANES), lambda b_, g, i: (b_, 0, j))
    cmp_spec = pl.BlockSpec((1, LANES, LANES), lambda b_, g, i: (b_, 0, 0))
    return pl.pallas_call(
        functools.partial(_nsa_kernel, tq=tq),
        grid=(B, N_GROUPS, S // tq),
        in_specs=[pl.BlockSpec((1, tq, gw), lambda b_, g, i: (b_, i, g)),
                  cmp_spec, cmp_spec, seq(0), seq(1), seq(2), seq(3),
                  pl.BlockSpec((1, tq, LANES), lambda b_, g, i: (b_, i, 0)),
                  pl.BlockSpec(ov.shape, lambda b_, g, i: (0, 0)),
                  pl.BlockSpec(ex.shape, lambda b_, g, i: (0, 0, 0))],
        out_specs=pl.BlockSpec((1, tq, gw), lambda b_, g, i: (b_, i, g)),
        out_shape=jax.ShapeDtypeStruct((B, S, QP_W), BF16),
        scratch_shapes=[pltpu.VMEM((2 * HPG, tq, 1), F32), pltpu.VMEM((2 * HPG, tq, 1), F32),
                        pltpu.VMEM((2 * HPG, tq, LANES), F32), pltpu.VMEM((tq, gw), F32)],
        compiler_params=_cparams(("arbitrary", "arbitrary", "arbitrary")),
        name="nsa",
    )(qp, kc, vc, kv4, kv4, kv4, kv4, small, ov, ex)


def _split_bf16(a):
    hi = a.astype(BF16)
    return hi, (a - hi.astype(F32)).astype(BF16)


def _mlstm_kernel(xm_ref, op_ref, gate_ref, cw_ref, cb_ref, wq_ref, wk_ref, wkt_ref, wv_ref, fb_ref, ng_ref,
                  y_ref, xpad_sc, xc_sc, c_sc, m_sc, *, S):
    L = ML_CHUNK
    nt = (((1,), (1,)), ((), ()))
    xpad_sc[0:8, :] = jnp.zeros((8, ML_WIDTH), F32)
    xpad_sc[8:, :] = xm_ref[0]
    rb = 256
    for r in range(S // rb):
        acc = jnp.zeros((rb, ML_WIDTH), F32) + cb_ref[...]
        for j in range(CONV_W):
            o = 8 + r * rb - (CONV_W - 1) + j
            acc = acc + cw_ref[j:j + 1, :] * xpad_sc[o:o + rb, :]
        xc_sc[r * rb:(r + 1) * rb, :] = (acc * jax.nn.sigmoid(acc)).astype(BF16)

    c_sc[...] = jnp.zeros(c_sc.shape, F32)
    m_sc[...] = jnp.zeros(m_sc.shape, F32)
    row = lax.broadcasted_iota(I32, (L, L), 0)
    col = lax.broadcasted_iota(I32, (L, L), 1)
    causal = col <= row
    triu = jnp.where(row <= col, 1.0, 0.0).astype(BF16)
    ones = jnp.ones((L, L), BF16)
    rs = 1.0 / math.sqrt(ML_DH)

    def chunk(c, carry):
        r0 = pl.multiple_of(c * L, L)
        for hh in range(ML_HEADS):
            hs = slice(hh * ML_DH, (hh + 1) * ML_DH)
            xc = xc_sc[pl.ds(r0, L), hs]
            xm = xm_ref[0, pl.ds(r0, L), hs].astype(BF16)
            q = jnp.dot(xc, wq_ref[hh], preferred_element_type=F32).astype(BF16)
            k = (jnp.dot(xc, wk_ref[hh], preferred_element_type=F32) * rs).astype(BF16)
            kt = lax.dot_general(wkt_ref[hh], xc, nt, preferred_element_type=F32) * rs
            v = jnp.dot(xm, wv_ref[hh], preferred_element_type=F32).astype(BF16)
            vaug = jnp.concatenate([v, ones], axis=1)

            i_row = gate_ref[0, hh, pl.ds(c, 1), :]
            z = gate_ref[0, ML_HEADS + hh, pl.ds(c, 1), :] + fb_ref[hh:hh + 1, :]
            lf_row = jnp.minimum(z, 0.0) - jnp.log(1.0 + jnp.exp(-jnp.abs(z)))
            rh, rl = _split_bf16(jnp.broadcast_to(lf_row, (8, L)))
            f_r = (jnp.dot(rh, triu, preferred_element_type=F32)
                   + jnp.dot(rl, triu, preferred_element_type=F32))[0:1]
            lh, ll = _split_bf16(jnp.where(causal, jnp.broadcast_to(lf_row, (L, L)), 0.0))
            f_c = (jnp.dot(lh, ones, preferred_element_type=F32)
                   + jnp.dot(ll, ones, preferred_element_type=F32))
            m_prev = m_sc[hh, 0:1, :]
            logd = jnp.where(causal, f_c - f_r + i_row, NEG)
            inter = f_c + m_prev
            m_t = jnp.maximum(inter, jnp.max(logd, axis=-1, keepdims=True))
            dm = jnp.where(causal, jnp.exp(logd - m_t), 0.0)
            wi = jnp.exp(inter - m_t)
            qk = (lax.dot_general(q, k, nt, preferred_element_type=F32) * dm).astype(BF16)
            caug = c_sc[hh]
            qc = jnp.dot(q, caug.astype(BF16), preferred_element_type=F32)
            qv = jnp.dot(qk, vaug, preferred_element_type=F32)
            num = wi * qc[:, :L] + qv[:, :L]
            den = wi * qc[:, L:] + qv[:, L:]
            h = num / jnp.maximum(jnp.abs(den), jnp.exp(-m_t))
            mu = jnp.mean(h, axis=-1, keepdims=True)
            hc = h - mu
            var = jnp.mean(hc * hc, axis=-1, keepdims=True)
            hn = hc * lax.rsqrt(var + LN_EPS) * ng_ref[:, hs]
            o = op_ref[0, pl.ds(r0, L), hs].astype(F32)
            y_ref[0, pl.ds(r0, L), hs] = (jax.nn.sigmoid(o) * hn).astype(BF16)

            f_l = f_c[L - 1:L, :]
            logw = f_l - f_r + i_row
            m_new = jnp.maximum(f_l + m_prev, jnp.max(logw, axis=-1, keepdims=True))
            decay = jnp.exp(f_l + m_prev - m_new)
            w_row = jnp.exp(logw - m_new)
            upd = jnp.dot((kt * w_row).astype(BF16), vaug, preferred_element_type=F32)
            c_sc[hh] = jnp.concatenate([decay, decay], axis=1) * caug + upd
            m_sc[hh] = jnp.broadcast_to(m_new, (8, L))
        return carry

    lax.fori_loop(0, S // L, chunk, 0)


def _mlstm(xm, op, gates, conv_w, conv_b, wq, wk, wkt, wv, fb, ng):
    B, S, _ = xm.shape
    full = lambda a: pl.BlockSpec(a.shape, lambda b_: (0,) * a.ndim)
    seq = pl.BlockSpec((1, S, ML_WIDTH), lambda b_: (b_, 0, 0))
    return pl.pallas_call(
        functools.partial(_mlstm_kernel, S=S),
        grid=(B,),
        in_specs=[seq, seq, pl.BlockSpec((1,) + gates.shape[1:], lambda b_: (b_, 0, 0, 0)),
                  full(conv_w), full(conv_b), full(wq), full(wk), full(wkt), full(wv), full(fb), full(ng)],
        out_specs=seq,
        out_shape=jax.ShapeDtypeStruct((B, S, ML_WIDTH), BF16),
        scratch_shapes=[pltpu.VMEM((S + 8, ML_WIDTH), F32), pltpu.VMEM((S, ML_WIDTH), BF16),
                        pltpu.VMEM((ML_HEADS, ML_DH, 2 * ML_DH), F32), pltpu.VMEM((ML_HEADS, 8, ML_CHUNK), F32)],
        compiler_params=_cparams(("arbitrary",)),
        name="mlstm",
    )(xm, op, gates, conv_w, conv_b, wq, wk, wkt, wv, fb, ng)


def _prep_proj_a(proj_a):
    pa = proj_a.reshape(N_HEADS, HEAD_DIM, D_MODEL)
    z = jnp.zeros_like(pa[:HPG])
    lo = jnp.concatenate([pa[:HPG], z], axis=1)
    hi = jnp.concatenate([z, pa[HPG:]], axis=1)
    return jnp.concatenate([lo, hi], axis=0).reshape(QP_W, D_MODEL).astype(BF16)


def _mix_kernel(o_ref, y_ref, gm_ref, x_ref, mod_ref, pa_ref, pb_ref, wo_ref, g2_ref, rwh_ref, rwl_ref, rb_ref,
                x1_ref, h2_ref, e_ref, w_ref, *, ts):
    nt = (((1,), (1,)), ((), ()))
    a = jnp.dot(o_ref[0], pa_ref[...], preferred_element_type=F32)
    bm = jnp.dot(y_ref[0], pb_ref[...], preferred_element_type=F32)
    ga = jax.nn.sigmoid(gm_ref[0, :, :D_MODEL].astype(F32))
    gb = jax.nn.sigmoid(gm_ref[0, :, D_MODEL:].astype(F32))
    mixed = jnp.dot((ga * a + gb * bm).astype(BF16), wo_ref[...], preferred_element_type=F32)
    x1 = x_ref[0] + mod_ref[0, 2:3, :] * mixed
    x1_ref[0] = x1
    ms = jnp.mean(x1 * x1, axis=-1, keepdims=True)
    h2 = x1 * lax.rsqrt(ms + RMS_EPS) * g2_ref[...] * (1.0 + mod_ref[0, 4:5, :]) + mod_ref[0, 3:4, :]
    h2_ref[0] = h2
    hh, hl = _split_bf16(h2)
    logit = (lax.dot_general(rwh_ref[...], hh, nt, preferred_element_type=F32)
             + lax.dot_general(rwh_ref[...], hl, nt, preferred_element_type=F32)
             + lax.dot_general(rwl_ref[...], hh, nt, preferred_element_type=F32)) + rb_ref[...]
    eio = lax.broadcasted_iota(I32, (N_EXPERTS, ts), 0).astype(F32)
    vals, idxs = [], []
    for _ in range(TOP_K):
        mx = jnp.max(logit, axis=0, keepdims=True)
        ix = jnp.min(jnp.where(logit == mx, eio, 1e9), axis=0, keepdims=True)
        vals.append(mx)
        idxs.append(ix)
        logit = jnp.where(eio == ix, -3e38, logit)
    ex = [jnp.exp(v - vals[0]) for v in vals]
    tot = ex[0] + ex[1] + ex[2] + ex[3]
    e_ref[...] = jnp.concatenate(idxs, axis=0).astype(I32)
    wrows = jnp.concatenate([e_ / tot for e_ in ex] + [jnp.zeros((LANES - TOP_K, ts), F32)], axis=0)
    w_ref[...] = jnp.transpose(wrows)


def _mix(o_nsa, y_ml, gm, x, mod, pa, pb, wo, g2, rwh, rwl, rb, ts=512):
    B, S, _ = x.shape
    T = B * S
    nst = S // ts
    blk = lambda wd: pl.BlockSpec((1, ts, wd), lambda b_, s: (b_, s, 0))
    full = lambda a: pl.BlockSpec(a.shape, lambda b_, s: (0,) * a.ndim)
    return pl.pallas_call(
        functools.partial(_mix_kernel, ts=ts),
        grid=(B, nst),
        in_specs=[blk(QP_W), blk(ML_WIDTH), blk(2 * D_MODEL), blk(D_MODEL),
                  pl.BlockSpec((1, 6, D_MODEL), lambda b_, s: (b_, 0, 0)),
                  full(pa), full(pb), full(wo), full(g2), full(rwh), full(rwl), full(rb)],
        out_specs=[blk(D_MODEL), blk(D_MODEL),
                   pl.BlockSpec((TOP_K, ts), lambda b_, s: (0, b_ * nst + s)),
                   pl.BlockSpec((ts, LANES), lambda b_, s: (b_ * nst + s, 0))],
        out_shape=[jax.ShapeDtypeStruct((B, S, D_MODEL), F32), jax.ShapeDtypeStruct((B, S, D_MODEL), F32),
                   jax.ShapeDtypeStruct((TOP_K, T), I32), jax.ShapeDtypeStruct((T, LANES), F32)],
        compiler_params=_cparams(("arbitrary", "arbitrary")),
        name="mix_router",
    )(o_nsa, y_ml, gm, x, mod, pa, pb, wo, g2, rwh, rwl, rb)


MOE_TM = 256


def _moe_tiles(T):
    r = T * TOP_K
    return (r + N_EXPERTS * (MOE_TM - 1) + MOE_TM - 1) // MOE_TM


def _route_kernel(e_ref, dest_ref, te_ref, na_ref, *, ntp):
    nrow = e_ref.shape[0]
    eio = lax.broadcasted_iota(I32, (N_EXPERTS, LANES), 0)
    row = lax.broadcasted_iota(I32, (LANES, LANES), 0)
    col = lax.broadcasted_iota(I32, (LANES, LANES), 1)
    triu = jnp.where(row <= col, 1.0, 0.0).astype(BF16)
    ones = jnp.ones((LANES, LANES), BF16)

    def rank_body(i, counts):
        oh = eio == e_ref[pl.ds(i, 1), :]
        ohb = jnp.where(oh, 1.0, 0.0).astype(BF16)
        cum = jnp.dot(ohb, triu, preferred_element_type=F32)
        rank = jnp.sum(jnp.where(oh, cum - 1.0 + counts, 0.0), axis=0, keepdims=True)
        dest_ref[pl.ds(i, 1), :] = rank.astype(I32)
        return counts + jnp.dot(ohb, ones, preferred_element_type=F32)

    counts = lax.fori_loop(0, nrow, rank_body, jnp.zeros((N_EXPERTS, LANES), F32)).astype(I32)
    tiles = (counts + (MOE_TM - 1)) // MOE_TM
    incl = tiles
    for s in (1, 2, 4, 8, 16):
        incl = incl + jnp.where(eio >= s, pltpu.roll(incl, s, 0), 0)
    off = (incl - tiles) * MOE_TM

    def dest_body(i, carry):
        oh = eio == e_ref[pl.ds(i, 1), :]
        base = jnp.sum(jnp.where(oh, off, 0), axis=0, keepdims=True)
        dest_ref[pl.ds(i, 1), :] = dest_ref[pl.ds(i, 1), :] + base
        return carry

    lax.fori_loop(0, nrow, dest_body, 0)
    ends = jnp.concatenate([incl] * (ntp // LANES), axis=1)
    tix = lax.broadcasted_iota(I32, (N_EXPERTS, ntp), 1)
    te = jnp.sum(jnp.where(ends <= tix, 1, 0), axis=0, keepdims=True)
    te_ref[...] = jnp.minimum(te, N_EXPERTS - 1)
    na_ref[...] = incl[N_EXPERTS - 1:N_EXPERTS, :]


def _route(e_t):
    T = e_t.shape[1]
    nrow = TOP_K * T // LANES
    ntp = -(-_moe_tiles(T) // LANES) * LANES
    dest, te, na = pl.pallas_call(
        functools.partial(_route_kernel, ntp=ntp),
        out_shape=[jax.ShapeDtypeStruct((nrow, LANES), I32), jax.ShapeDtypeStruct((1, ntp), I32),
                   jax.ShapeDtypeStruct((1, LANES), I32)],
        compiler_params=pltpu.CompilerParams(vmem_limit_bytes=VMEM_LIMIT),
        name="route",
    )(e_t.reshape(nrow, LANES))
    return dest.reshape(TOP_K, T), te.reshape(ntp), na[0, :1]


MOE_TT = 256


def _row_copy(src_ref, s, dst_ref, d, sem):
    return pltpu.make_async_copy(src_ref.at[pl.ds(s, 1), :], dst_ref.at[pl.ds(d, 1), :], sem)


def _dispatch_kernel(dest_ref, h_ref, xs_in_ref, xs_ref, sem):
    del xs_in_ref

    def issue(i, carry):
        for k in range(TOP_K):
            _row_copy(h_ref, i, xs_ref, dest_ref[k, i], sem).start()
        return carry

    lax.fori_loop(0, MOE_TT, issue, 0)

    def drain(i, carry):
        _row_copy(h_ref, 0, xs_ref, 0, sem).wait()
        return carry

    lax.fori_loop(0, MOE_TT * TOP_K, drain, 0)


def _dispatch(h2, dest, n_rows):
    T = h2.shape[0]
    xs0 = jnp.zeros((n_rows, D_MODEL), F32)
    return pl.pallas_call(
        _dispatch_kernel,
        grid=(T // MOE_TT,),
        in_specs=[pl.BlockSpec((TOP_K, MOE_TT), lambda i: (0, i), memory_space=pltpu.SMEM),
                  pl.BlockSpec((MOE_TT, D_MODEL), lambda i: (i, 0)),
                  pl.BlockSpec(memory_space=pl.ANY)],
        out_specs=pl.BlockSpec(memory_space=pl.ANY),
        out_shape=jax.ShapeDtypeStruct((n_rows, D_MODEL), F32),
        scratch_shapes=[pltpu.SemaphoreType.DMA],
        input_output_aliases={2: 0},
        compiler_params=_cparams(("arbitrary",)),
        name="dispatch",
    )(dest, h2, xs0)


def _ffn_kernel(te_ref, na_ref, x_ref, wu_ref, bu_ref, wd_ref, bd_ref, y_ref, wu_sc, wd_sc):
    i = pl.program_id(0)
    fresh = jnp.logical_or(i == 0, te_ref[i] != te_ref[jnp.maximum(i - 1, 0)])

    @pl.when(fresh)
    def _():
        wu_sc[...] = wu_ref[0].astype(BF16)
        wd_sc[...] = wd_ref[0].astype(BF16)

    @pl.when(i < na_ref[0])
    def _():
        x = x_ref[...].astype(BF16)
        gu = jnp.dot(x, wu_sc[...], preferred_element_type=F32) + bu_ref[0]
        gl = jnp.minimum(gu[:, :D_MODEL], SWIGLU_LIMIT)
        lin = jnp.clip(gu[:, D_MODEL:], -SWIGLU_LIMIT, SWIGLU_LIMIT)
        a = gl * jax.nn.sigmoid(SWIGLU_ALPHA * gl) * (lin + 1.0)
        y_ref[...] = jnp.dot(a.astype(BF16), wd_sc[...], preferred_element_type=F32) + bd_ref[0]

    @pl.when(i >= na_ref[0])
    def _():
        y_ref[...] = jnp.zeros(y_ref.shape, F32)


def _ffn(xs, te, na, w_up, b_up, w_down, b_down):
    n_rows = xs.shape[0]
    nt = n_rows // MOE_TM
    de = w_up.shape[2]
    grid_spec = pltpu.PrefetchScalarGridSpec(
        num_scalar_prefetch=2,
        grid=(nt,),
        in_specs=[pl.BlockSpec((MOE_TM, D_MODEL), lambda i, te_, na_: (i, 0)),
                  pl.BlockSpec((1, D_MODEL, de), lambda i, te_, na_: (te_[i], 0, 0)),
                  pl.BlockSpec((1, 1, de), lambda i, te_, na_: (te_[i], 0, 0)),
                  pl.BlockSpec((1, de // 2, D_MODEL), lambda i, te_, na_: (te_[i], 0, 0)),
                  pl.BlockSpec((1, 1, D_MODEL), lambda i, te_, na_: (te_[i], 0, 0))],
        out_specs=pl.BlockSpec((MOE_TM, D_MODEL), lambda i, te_, na_: (i, 0)),
        scratch_shapes=[pltpu.VMEM((D_MODEL, de), BF16), pltpu.VMEM((de // 2, D_MODEL), BF16)],
    )
    return pl.pallas_call(
        _ffn_kernel,
        grid_spec=grid_spec,
        out_shape=jax.ShapeDtypeStruct((n_rows, D_MODEL), F32),
        compiler_params=_cparams(("arbitrary",)),
        name="expert_ffn",
    )(te, na, xs, w_up, b_up.reshape(N_EXPERTS, 1, de), w_down, b_down.reshape(N_EXPERTS, 1, D_MODEL))


def _combine_kernel(dest_ref, y_ref, w_ref, x1_ref, mod_ref, fg_ref, o_ref, buf, sem):
    def issue(i, carry):
        for k in range(TOP_K):
            _row_copy(y_ref, dest_ref[k, i], buf.at[k], i, sem).start()
        return carry

    lax.fori_loop(0, MOE_TT, issue, 0)

    def drain(i, carry):
        _row_copy(y_ref, 0, buf.at[0], 0, sem).wait()
        return carry

    lax.fori_loop(0, MOE_TT * TOP_K, drain, 0)
    moe = w_ref[:, 0:1] * buf[0]
    for k in range(1, TOP_K):
        moe = moe + w_ref[:, k:k + 1] * buf[k]
    x2 = x1_ref[...] + mod_ref[0, 5:6, :] * moe
    ms = jnp.mean(x2 * x2, axis=-1, keepdims=True)
    o_ref[...] = x2 * lax.rsqrt(ms + RMS_EPS) * fg_ref[...]


def _combine(y, dest, wcol, x1, mod, final_g, S):
    T = x1.shape[0]
    per_b = S // MOE_TT
    return pl.pallas_call(
        _combine_kernel,
        grid=(T // MOE_TT,),
        in_specs=[pl.BlockSpec((TOP_K, MOE_TT), lambda i: (0, i), memory_space=pltpu.SMEM),
                  pl.BlockSpec(memory_space=pl.ANY),
                  pl.BlockSpec((MOE_TT, LANES), lambda i: (i, 0)),
                  pl.BlockSpec((MOE_TT, D_MODEL), lambda i: (i, 0)),
                  pl.BlockSpec((1, 6, D_MODEL), lambda i: (i // per_b, 0, 0)),
                  pl.BlockSpec((1, D_MODEL), lambda i: (0, 0))],
        out_specs=pl.BlockSpec((MOE_TT, D_MODEL), lambda i: (i, 0)),
        out_shape=jax.ShapeDtypeStruct((T, D_MODEL), F32),
        scratch_shapes=[pltpu.VMEM((TOP_K, MOE_TT, D_MODEL), F32), pltpu.SemaphoreType.DMA],
        compiler_params=_cparams(("arbitrary",)),
        name="combine",
    )(dest, y, wcol, x1, mod, final_g.reshape(1, D_MODEL))


def _layer(x, c, ada_w, ada_b, norm1_g, w_in, b_in, cmp_pe_k, cmp_w1_k, cmp_w2_k, cmp_pe_v, cmp_w1_v, cmp_w2_v,
           ml_conv_w, ml_conv_b, ml_wq, ml_wk, ml_wv, ml_f_bias, ml_norm_g, proj_a, proj_b, w_out, norm2_g,
           router_w, router_b, exp_w_up, exp_b_up, exp_w_down, exp_b_down, final_g):
    B, S, D = x.shape
    T = B * S
    mod = _adaln(c, ada_w, ada_b).reshape(B, 6, D)
    w, b = _prep_w_in(w_in, b_in)
    qp, kc_in, vc_in, kv4, xm, op, gm, small = _inproj(x, mod, norm1_g, w, b)
    prep = [_prep_compress(cmp_pe_k, cmp_w1_k, cmp_w2_k), _prep_compress(cmp_pe_v, cmp_w1_v, cmp_w2_v)]
    wk, pe, w2 = [jnp.stack([prep[0][i], prep[1][i]]) for i in range(3)]
    kc, vc = _compress(kc_in, vc_in, wk, pe, w2)
    ov, ex = _nsa_consts(S)
    o_nsa = _nsa(qp, kc, vc, kv4, small, ov, ex)
    gates = jnp.transpose(small[:, :, 3 * N_HEADS:3 * N_HEADS + 2 * ML_HEADS], (0, 2, 1))
    gates = gates.reshape(B, 2 * ML_HEADS, S // ML_CHUNK, ML_CHUNK)
    y_ml = _mlstm(xm, op, gates, ml_conv_w, ml_conv_b.reshape(1, ML_WIDTH), ml_wq.astype(BF16), ml_wk.astype(BF16),
                  jnp.transpose(ml_wk, (0, 2, 1)).astype(BF16), ml_wv.astype(BF16),
                  jnp.broadcast_to(ml_f_bias[:, None], (ML_HEADS, LANES)), ml_norm_g.reshape(1, ML_WIDTH))
    rwt = jnp.transpose(router_w)
    rwh = rwt.astype(BF16)
    rwl = (rwt - rwh.astype(F32)).astype(BF16)
    x1, h2, e_t, wcol = _mix(o_nsa, y_ml, gm, x, mod, _prep_proj_a(proj_a), proj_b.astype(BF16), w_out.astype(BF16),
                             norm2_g.reshape(1, D), rwh, rwl, router_b.reshape(N_EXPERTS, 1))
    dest, te, na = _route(e_t)
    n_rows = _moe_tiles(T) * MOE_TM
    xs = _dispatch(h2.reshape(T, D), dest, n_rows)
    y = _ffn(xs, te, na, exp_w_up, exp_b_up, exp_w_down, exp_b_down)
    out = _combine(y, dest, wcol, x1.reshape(T, D), mod, final_g, S)
    return out.reshape(B, S, D)


def kernel(x, c, ada_w, ada_b, norm1_g, w_in, b_in, cmp_pe_k, cmp_w1_k, cmp_w2_k, cmp_pe_v, cmp_w1_v, cmp_w2_v, ml_conv_w, ml_conv_b, ml_wq, ml_wk, ml_wv, ml_f_bias, ml_norm_g, proj_a, proj_b, w_out, norm2_g, router_w, router_b, exp_w_up, exp_b_up, exp_w_down, exp_b_down, final_g):
    assert ada_w.shape[0] == 1, "one layer: the final RMSNorm is fused into the layer's last kernel"
    layer = (ada_w, ada_b, norm1_g, w_in, b_in, cmp_pe_k, cmp_w1_k, cmp_w2_k, cmp_pe_v, cmp_w1_v, cmp_w2_v,
             ml_conv_w, ml_conv_b, ml_wq, ml_wk, ml_wv, ml_f_bias, ml_norm_g, proj_a, proj_b, w_out, norm2_g,
             router_w, router_b, exp_w_up, exp_b_up, exp_w_down, exp_b_down)
    return _layer(x, c, *[p[0] for p in layer], final_g)
```

```python
import functools
import math

import numpy as np
import jax
import jax.numpy as jnp
from jax import lax
from jax.experimental import pallas as pl
from jax.experimental.pallas import tpu as pltpu

F32 = jnp.float32
BF16 = jnp.bfloat16
I32 = jnp.int32

D_MODEL = 1024
N_HEADS = 8
N_GROUPS = 2
HPG = N_HEADS // N_GROUPS
HEAD_DIM = 64
CMP_BLOCK = 32
CMP_STRIDE = 16
CMP_HIDDEN = 128
SEL_BLOCK = 64
SEL_TOPN = 8
WINDOW = 256
ATTN_SCALE = HEAD_DIM ** -0.5
ML_HEADS = 4
ML_DH = 128
ML_WIDTH = ML_HEADS * ML_DH
ML_CHUNK = 128
CONV_W = 4
N_EXPERTS = 32
TOP_K = 4
SWIGLU_LIMIT = 7.0
SWIGLU_ALPHA = 1.702
RMS_EPS = 1e-5
LN_EPS = 1e-5

LANES = 128
NEG = -1e30
VMEM_LIMIT = 56 * 1024 * 1024

QP_W = N_HEADS * LANES
C_Q = 0
C_KC = C_Q + QP_W
C_VC = C_KC + LANES
C_KV6 = C_VC + LANES
C_XM = C_KV6 + 6 * LANES
C_OP = C_XM + ML_WIDTH
C_GM = C_OP + ML_WIDTH
C_SM = C_GM + 2 * D_MODEL
IN_W = C_SM + LANES


def _cparams(sem):
    return pltpu.CompilerParams(dimension_semantics=sem, vmem_limit_bytes=VMEM_LIMIT)


def _adaln_kernel(c_ref, w_ref, b_ref, o_ref):
    c = c_ref[...]
    sc = (c * jax.nn.sigmoid(c)).astype(BF16)
    o_ref[...] = jnp.dot(sc, w_ref[...].astype(BF16), preferred_element_type=F32) + b_ref[...]


def _adaln(c, ada_w, ada_b):
    B = c.shape[0]
    tn = 1024
    n = ada_w.shape[1]
    return pl.pallas_call(
        _adaln_kernel,
        grid=(n // tn,),
        in_specs=[pl.BlockSpec((B, D_MODEL), lambda j: (0, 0)),
                  pl.BlockSpec((D_MODEL, tn), lambda j: (0, j)),
                  pl.BlockSpec((1, tn), lambda j: (0, j))],
        out_specs=pl.BlockSpec((B, tn), lambda j: (0, j)),
        out_shape=jax.ShapeDtypeStruct((B, n), F32),
        compiler_params=_cparams(("arbitrary",)),
        name="adaln",
    )(c, ada_w, ada_b.reshape(1, n))


def _prep_w_in(w_in, b_in):
    def pad_q(a):
        lead = a.shape[:-1]
        a = a.reshape(lead + (N_HEADS, HEAD_DIM))
        z = jnp.zeros_like(a[..., :HPG, :])
        lo = jnp.concatenate([a[..., :HPG, :], z], axis=-1)
        hi = jnp.concatenate([z, a[..., HPG:, :]], axis=-1)
        return jnp.concatenate([lo, hi], axis=-2).reshape(lead + (QP_W,))

    def v_aug(a, fill):
        f = jnp.full(a.shape[:-1] + (HEAD_DIM,), fill, a.dtype)
        return jnp.concatenate([a[..., :HEAD_DIM], f, a[..., HEAD_DIM:], f], axis=-1)

    def cols(a):
        sl = lambda lo, hi: lax.slice_in_dim(a, lo, hi, axis=-1)
        return dict(q=sl(0, 512), kc=sl(512, 640), vc=sl(640, 768), ks=sl(768, 896), vs=sl(896, 1024),
                    kw=sl(1024, 1152), vw=sl(1152, 1280), gn=sl(1280, 1304), xm=sl(1304, 1816),
                    op=sl(1816, 2328), fi=sl(2328, 2336), gm=sl(2336, 4384))

    def layout(c, fill):
        zs = jnp.zeros(c["q"].shape[:-1] + (LANES - 32,), c["q"].dtype)
        return jnp.concatenate([pad_q(c["q"]), c["kc"], c["vc"], c["ks"], c["kw"], v_aug(c["vs"], fill),
                                v_aug(c["vw"], fill), c["xm"], c["op"], c["gm"], c["gn"], c["fi"], zs], axis=-1)

    w = layout(cols(w_in), 0.0)
    b = layout(cols(b_in), 1.0)
    return w.astype(BF16), b.reshape(1, IN_W)


def _inproj_kernel(x_ref, mod_ref, g_ref, w_ref, b_ref,
                   q_ref, kc_ref, vc_ref, kv_ref, xm_ref, op_ref, gm_ref, sm_ref):
    x = x_ref[0]
    ms = jnp.mean(x * x, axis=-1, keepdims=True)
    y = x * lax.rsqrt(ms + RMS_EPS) * g_ref[...]
    h = (y * (1.0 + mod_ref[0, 1:2, :]) + mod_ref[0, 0:1, :]).astype(BF16)

    def seg(lo, hi):
        return jnp.dot(h, w_ref[:, lo:hi], preferred_element_type=F32) + b_ref[:, lo:hi]

    for hd in range(N_HEADS):
        q_ref[0, hd] = seg(C_Q + hd * LANES, C_Q + (hd + 1) * LANES).astype(BF16)
    kc_ref[0] = seg(C_KC, C_VC).astype(BF16)
    vc_ref[0] = seg(C_VC, C_KV6).astype(BF16)
    kv_ref[0] = seg(C_KV6, C_XM).astype(BF16)
    xm_ref[0] = seg(C_XM, C_OP)
    op_ref[0] = seg(C_OP, C_GM).astype(BF16)
    gm_ref[0] = seg(C_GM, C_SM).astype(BF16)
    sm_ref[0] = seg(C_SM, IN_W)


def _inproj(x, mod, norm_g, w, b, ts=512):
    B, S, _ = x.shape
    widths = [(LANES, BF16), (LANES, BF16), (6 * LANES, BF16), (ML_WIDTH, F32),
              (ML_WIDTH, BF16), (2 * D_MODEL, BF16), (LANES, F32)]
    q_spec = pl.BlockSpec((1, N_HEADS, ts, LANES), lambda b_, s: (b_, 0, s, 0))
    return pl.pallas_call(
        _inproj_kernel,
        grid=(B, S // ts),
        in_specs=[pl.BlockSpec((1, ts, D_MODEL), lambda b_, s: (b_, s, 0)),
                  pl.BlockSpec((1, 6, D_MODEL), lambda b_, s: (b_, 0, 0)),
                  pl.BlockSpec((1, D_MODEL), lambda b_, s: (0, 0)),
                  pl.BlockSpec((D_MODEL, IN_W), lambda b_, s: (0, 0)),
                  pl.BlockSpec((1, IN_W), lambda b_, s: (0, 0))],
        out_specs=[q_spec] + [pl.BlockSpec((1, ts, wd), lambda b_, s: (b_, s, 0)) for wd, _ in widths],
        out_shape=([jax.ShapeDtypeStruct((B, N_HEADS, S, LANES), BF16)]
                   + [jax.ShapeDtypeStruct((B, S, wd), dt) for wd, dt in widths]),
        compiler_params=_cparams(("arbitrary", "arbitrary")),
        name="inproj",
    )(x, mod, norm_g.reshape(1, D_MODEL), w, b)


def _prep_compress(pe, w1, w2):
    w1r = w1.reshape(2, CMP_STRIDE, HEAD_DIM, CMP_HIDDEN)
    z = jnp.zeros_like(w1r[0])
    cols = []
    for g in range(N_GROUPS):
        for a in range(2):
            parts = [w1r[a] if gg == g else z for gg in range(N_GROUPS)]
            cols.append(jnp.concatenate(parts, axis=1).reshape(CMP_STRIDE * LANES, CMP_HIDDEN))
    wk = jnp.concatenate(cols, axis=1)
    per = pe.reshape(2, CMP_STRIDE, HEAD_DIM)
    pe_rows = jnp.concatenate([per, per], axis=-1).reshape(2, CMP_STRIDE * LANES)
    pe_rows = jnp.concatenate([pe_rows, jnp.zeros((6, CMP_STRIDE * LANES), pe.dtype)], axis=0)
    z2 = jnp.zeros_like(w2)
    w2p = jnp.stack([jnp.concatenate([w2, z2], axis=1), jnp.concatenate([z2, w2], axis=1)])
    return wk.astype(BF16), pe_rows.astype(BF16), w2p.astype(BF16)


def _compress_kernel(ak_ref, av_ref, wk_ref, pe_ref, w2_ref, ok_ref, ov_ref):
    def hidden(i, a_ref, g):
        y = jnp.dot(a_ref[0], wk_ref[i], preferred_element_type=F32)
        yp = jnp.dot(pe_ref[i], wk_ref[i], preferred_element_type=F32)
        ca, cb = 2 * g * LANES, (2 * g + 1) * LANES
        bias = yp[0:1, ca:ca + LANES] + yp[1:2, cb:cb + LANES]
        hid = y[:, ca:ca + LANES] + pltpu.roll(y[:, cb:cb + LANES], LANES - 1, 0) + bias
        return jax.nn.gelu(hid, approximate=True).astype(BF16)

    ok_ref[0] = (jnp.dot(hidden(0, ak_ref, 0), w2_ref[0, 0], preferred_element_type=F32)
                 + jnp.dot(hidden(0, ak_ref, 1), w2_ref[0, 1], preferred_element_type=F32)).astype(BF16)
    ones_hi = jnp.where(lax.broadcasted_iota(I32, (LANES, LANES), 1) >= HEAD_DIM, 1.0, 0.0)
    for g in range(N_GROUPS):
        v = jnp.dot(hidden(1, av_ref, g), w2_ref[1, 0], preferred_element_type=F32)
        ov_ref[0, g] = (v + ones_hi).astype(BF16)


def _compress(kc_in, vc_in, wk, pe, w2):
    B, S, _ = kc_in.shape
    n = S // CMP_STRIDE
    kw = CMP_STRIDE * LANES
    a_spec = pl.BlockSpec((1, n, kw), lambda b_: (b_, 0, 0))
    full = lambda shp: pl.BlockSpec(shp, lambda b_: (0,) * len(shp))
    return pl.pallas_call(
        _compress_kernel,
        grid=(B,),
        in_specs=[a_spec, a_spec, full(wk.shape), full(pe.shape), full(w2.shape)],
        out_specs=[pl.BlockSpec((1, n, LANES), lambda b_: (b_, 0, 0)),
                   pl.BlockSpec((1, N_GROUPS, n, LANES), lambda b_: (b_, 0, 0, 0))],
        out_shape=[jax.ShapeDtypeStruct((B, n, LANES), BF16), jax.ShapeDtypeStruct((B, N_GROUPS, n, LANES), BF16)],
        compiler_params=_cparams(("arbitrary",)),
        name="compress",
    )(kc_in.reshape(B, n, kw), vc_in.reshape(B, n, kw), wk, pe, w2)


def _nsa_consts(S):
    n_cmp = (S - CMP_BLOCK) // CMP_STRIDE + 1
    n_slc = S // SEL_BLOCK
    cs = np.arange(n_cmp)[:, None] * CMP_STRIDE
    ss = np.arange(n_slc)[None, :] * SEL_BLOCK
    ov = np.clip(np.minimum(cs + CMP_BLOCK, ss + SEL_BLOCK) - np.maximum(cs, ss), 0, None) / CMP_BLOCK
    ovp = np.zeros((LANES, LANES), np.float32)
    ovp[:n_cmp, :n_slc] = ov
    nkt = S // LANES
    ex = np.zeros((nkt, LANES, LANES), np.float32)
    for kt in range(nkt):
        for l in range(LANES):
            ex[kt, (kt * LANES + l) // SEL_BLOCK, l] = 1.0
    return jnp.asarray(ovp.T, BF16), jnp.asarray(ex, BF16)


M_FLOOR = 0.5 * NEG


def _nsa_kernel(q_ref, kc_ref, vc_ref, ks_ref, kw_ref, vs_ref, vw_ref, gate_ref, ovt_ref, ex_ref,
                o_ref, s_sc, mx_sc, mb_sc, acc_sc, out_sc, *, tq):
    g = pl.program_id(1)
    t0 = pl.program_id(2) * tq
    n_slc = ks_ref.shape[1] // SEL_BLOCK
    tpos = t0 + lax.broadcasted_iota(I32, (tq, LANES), 0)
    lane = lax.broadcasted_iota(I32, (tq, LANES), 1)
    lane_row = lax.broadcasted_iota(I32, (1, LANES), 1)
    slopes = [jnp.where(g == 0, 2.0 ** -(hh + 1), 2.0 ** -(HPG + hh + 1)).astype(F32) for hh in range(HPG)]
    rows = [slice(hh * tq, (hh + 1) * tq) for hh in range(HPG)]
    nt = (((1,), (1,)), ((), ()))
    q4 = q_ref[0].reshape(HPG * tq, LANES) * ATTN_SCALE

    def gate(br, hh):
        c0, c1 = br * N_HEADS + hh, br * N_HEADS + HPG + hh
        col = jnp.where(g == 0, gate_ref[0, :, c0:c0 + 1], gate_ref[0, :, c1:c1 + 1])
        return jax.nn.sigmoid(col)

    def denom(pv):
        return jnp.where(lane < HEAD_DIM, pltpu.roll(pv, HEAD_DIM, 1), pv)

    s_all = lax.dot_general(q4, kc_ref[0], nt, preferred_element_type=F32)
    cmask = jnp.where(lane * CMP_STRIDE + (CMP_BLOCK - 1) <= tpos, 0.0, NEG)
    crel = (lane_row * CMP_STRIDE - t0).astype(F32)
    vca = vc_ref[0, 0]
    psum = jnp.zeros((tq, LANES), F32)
    for hh in range(HPG):
        s = s_all[rows[hh]] + (cmask + slopes[hh] * crel)
        m = jnp.maximum(jnp.max(s, axis=-1, keepdims=True), M_FLOOR)
        p = jnp.exp(s - m)
        pv = jnp.dot(p.astype(BF16), vca, preferred_element_type=F32)
        inv = 1.0 / jnp.maximum(denom(pv), 1e-30)
        psum = psum + p * inv
        out_sc[rows[hh], :] = gate(0, hh) * (pv * inv)

    ph, plo = _split_bf16(psum)
    imp = (lax.dot_general(ovt_ref[...], ph, nt, preferred_element_type=F32)
           + lax.dot_general(ovt_ref[...], plo, nt, preferred_element_type=F32))[0:n_slc]
    blk = lax.broadcasted_iota(I32, (n_slc, tq), 0)
    cur = (t0 + lax.broadcasted_iota(I32, (n_slc, tq), 1)) // SEL_BLOCK
    blkf = blk.astype(F32)
    sc = jnp.where((blk == cur) | (blk == 0), -NEG, jnp.where(blk > cur, NEG, imp))
    selt = jnp.zeros((n_slc, tq), F32)
    for _ in range(SEL_TOPN):
        mx = jnp.max(sc, axis=0, keepdims=True)
        idx = jnp.min(jnp.where(sc == mx, blkf, 1e9), axis=0, keepdims=True)
        hit = blkf == idx
        selt = jnp.where(hit, 1.0, selt)
        sc = jnp.where(hit, -3e38, sc)
    selt = jnp.where(blk <= cur, selt, 0.0)
    sel = jnp.transpose(jnp.concatenate([selt, jnp.zeros((LANES - n_slc, tq), F32)], axis=0)).astype(BF16)

    KT = NSA_KCHUNK

    def scores(slot, k_t, kpos0, mask):
        s_t = lax.dot_general(q4, k_t, nt, preferred_element_type=F32)
        rel = (kpos0 + lane_row - t0).astype(F32)
        for hh in range(HPG):
            s = s_t[rows[hh]] + (mask + slopes[hh] * rel)
            s_sc[slot, rows[hh], :] = s
            mx_sc[rows[hh], :] = jnp.maximum(mx_sc[rows[hh], :], s)

    def finish_max():
        for hh in range(HPG):
            m = jnp.maximum(jnp.max(mx_sc[rows[hh], :], axis=-1, keepdims=True), M_FLOOR)
            mb_sc[rows[hh], :] = jnp.broadcast_to(m, (tq, LANES))

    def weighted(c, v_c):
        for hh in range(HPG):
            s4 = jnp.concatenate([s_sc[KT * c + j, rows[hh], :] for j in range(KT)], axis=1)
            mb = mb_sc[rows[hh], :]
            p = jnp.exp(s4 - jnp.concatenate([mb] * KT, axis=1)).astype(BF16)
            acc_sc[rows[hh], :] += jnp.dot(p, v_c, preferred_element_type=F32)

    mx_sc[...] = jnp.full(mx_sc.shape, NEG, F32)
    acc_sc[...] = jnp.zeros(acc_sc.shape, F32)
    n_chunks = (t0 + tq + KT * LANES - 1) // (KT * LANES)

    def slc_scores(kt, carry):
        r0 = pl.multiple_of(kt * LANES, LANES)
        picked = jnp.dot(sel, ex_ref[kt], preferred_element_type=F32)
        mask = jnp.where((picked > 0.5) & (r0 + lane <= tpos), 0.0, NEG)
        scores(kt, ks_ref[0, pl.ds(r0, LANES), :], r0, mask)
        return carry

    lax.fori_loop(0, KT * n_chunks, slc_scores, 0)
    finish_max()

    def slc_weighted(c, carry):
        r0 = pl.multiple_of(c * KT * LANES, KT * LANES)
        weighted(c, vs_ref[0, pl.ds(r0, KT * LANES), :])
        return carry

    lax.fori_loop(0, n_chunks, slc_weighted, 0)
    for hh in range(HPG):
        pv = acc_sc[rows[hh], :]
        out_sc[rows[hh], :] += gate(1, hh) * (pv / denom(pv))

    mx_sc[...] = jnp.full(mx_sc.shape, NEG, F32)
    acc_sc[...] = jnp.zeros(acc_sc.shape, F32)
    v_tiles = []
    for r in range(KT):
        kt = t0 // LANES - WINDOW // LANES + r
        r0 = pl.multiple_of(jnp.maximum(kt, 0) * LANES, LANES)
        spos = kt * LANES + lane
        dist = tpos - spos
        mask = jnp.where((dist >= 0) & (dist < WINDOW) & (spos >= 0), 0.0, NEG)
        scores(r, kw_ref[0, pl.ds(r0, LANES), :], kt * LANES, mask)
        v_tiles.append(vw_ref[0, pl.ds(r0, LANES), :])
    finish_max()
    weighted(0, jnp.concatenate(v_tiles, axis=0))
    for hh in range(HPG):
        pv = acc_sc[rows[hh], :]
        o = out_sc[rows[hh], :] + gate(2, hh) * (pv / denom(pv))
        o_ref[0, :, hh * LANES:(hh + 1) * LANES] = o.astype(BF16)


NSA_KCHUNK = 4
NSA_TQ = (NSA_KCHUNK - WINDOW // LANES) * LANES


def _nsa(q, kc, vc, kv6, small, ovt, ex):
    B, _, S, _ = q.shape
    tq = NSA_TQ
    gw = HPG * LANES
    seq = lambda f: pl.BlockSpec((1, S, LANES), f)
    return pl.pallas_call(
        functools.partial(_nsa_kernel, tq=tq),
        grid=(B, N_GROUPS, S // tq),
        in_specs=[pl.BlockSpec((1, HPG, tq, LANES), lambda b_, g, i: (b_, g, i, 0)),
                  pl.BlockSpec((1, LANES, LANES), lambda b_, g, i: (b_, 0, 0)),
                  pl.BlockSpec((1, 1, LANES, LANES), lambda b_, g, i: (b_, g, 0, 0)),
                  seq(lambda b_, g, i: (b_, 0, 0)), seq(lambda b_, g, i: (b_, 0, 1)),
                  seq(lambda b_, g, i: (b_, 0, 2 + g)), seq(lambda b_, g, i: (b_, 0, 4 + g)),
                  pl.BlockSpec((1, tq, LANES), lambda b_, g, i: (b_, i, 0)),
                  pl.BlockSpec(ovt.shape, lambda b_, g, i: (0, 0)),
                  pl.BlockSpec(ex.shape, lambda b_, g, i: (0, 0, 0))],
        out_specs=pl.BlockSpec((1, tq, gw), lambda b_, g, i: (b_, i, g)),
        out_shape=jax.ShapeDtypeStruct((B, S, QP_W), BF16),
        scratch_shapes=[pltpu.VMEM((S // LANES, HPG * tq, LANES), F32)] + [pltpu.VMEM((HPG * tq, LANES), F32)] * 4,
        compiler_params=_cparams(("arbitrary", "arbitrary", "arbitrary")),
        name="nsa",
    )(q, kc, vc, kv6, kv6, kv6, kv6, small, ovt, ex)


def _split_bf16(a):
    hi = a.astype(BF16)
    return hi, (a - hi.astype(F32)).astype(BF16)


def _mlstm_kernel(xm_ref, op_ref, gate_ref, cw_ref, cb_ref, wq_ref, wk_ref, wkt_ref, wv_ref, fb_ref, ng_ref,
                  y_ref, xpad_sc, xc_sc, c_sc, m_sc, *, S):
    L = ML_CHUNK
    nt = (((1,), (1,)), ((), ()))
    xpad_sc[0:8, :] = jnp.zeros((8, ML_WIDTH), F32)
    xpad_sc[8:, :] = xm_ref[0]
    rb = 256
    for r in range(S // rb):
        acc = jnp.zeros((rb, ML_WIDTH), F32) + cb_ref[...]
        for j in range(CONV_W):
            o = 8 + r * rb - (CONV_W - 1) + j
            acc = acc + cw_ref[j:j + 1, :] * xpad_sc[o:o + rb, :]
        xc_sc[r * rb:(r + 1) * rb, :] = (acc * jax.nn.sigmoid(acc)).astype(BF16)

    c_sc[...] = jnp.zeros(c_sc.shape, F32)
    m_sc[...] = jnp.zeros(m_sc.shape, F32)
    row = lax.broadcasted_iota(I32, (L, L), 0)
    col = lax.broadcasted_iota(I32, (L, L), 1)
    causal = col <= row
    triu = jnp.where(row <= col, 1.0, 0.0).astype(BF16)
    ones = jnp.ones((L, L), BF16)
    rs = 1.0 / math.sqrt(ML_DH)

    def chunk(c, carry):
        r0 = pl.multiple_of(c * L, L)
        for hh in range(ML_HEADS):
            hs = slice(hh * ML_DH, (hh + 1) * ML_DH)
            xc = xc_sc[pl.ds(r0, L), hs]
            xm = xm_ref[0, pl.ds(r0, L), hs].astype(BF16)
            q = jnp.dot(xc, wq_ref[hh], preferred_element_type=F32).astype(BF16)
            k = (jnp.dot(xc, wk_ref[hh], preferred_element_type=F32) * rs).astype(BF16)
            kt = lax.dot_general(wkt_ref[hh], xc, nt, preferred_element_type=F32) * rs
            v = jnp.dot(xm, wv_ref[hh], preferred_element_type=F32).astype(BF16)
            vaug = jnp.concatenate([v, ones], axis=1)

            i_row = gate_ref[0, hh, pl.ds(c, 1), :]
            z = gate_ref[0, ML_HEADS + hh, pl.ds(c, 1), :] + fb_ref[hh:hh + 1, :]
            lf_row = jnp.minimum(z, 0.0) - jnp.log(1.0 + jnp.exp(-jnp.abs(z)))
            rh, rl = _split_bf16(jnp.broadcast_to(lf_row, (8, L)))
            f_r = (jnp.dot(rh, triu, preferred_element_type=F32)
                   + jnp.dot(rl, triu, preferred_element_type=F32))[0:1]
            lh, ll = _split_bf16(jnp.where(causal, jnp.broadcast_to(lf_row, (L, L)), 0.0))
            f_c = (jnp.dot(lh, ones, preferred_element_type=F32)
                   + jnp.dot(ll, ones, preferred_element_type=F32))
            m_prev = m_sc[hh, 0:1, :]
            logd = jnp.where(causal, f_c - f_r + i_row, NEG)
            inter = f_c + m_prev
            m_t = jnp.maximum(inter, jnp.max(logd, axis=-1, keepdims=True))
            dm = jnp.where(causal, jnp.exp(logd - m_t), 0.0)
            wi = jnp.exp(inter - m_t)
            qk = (lax.dot_general(q, k, nt, preferred_element_type=F32) * dm).astype(BF16)
            caug = c_sc[hh]
            qc = jnp.dot(q, caug.astype(BF16), preferred_element_type=F32)
            qv = jnp.dot(qk, vaug, preferred_element_type=F32)
            num = wi * qc[:, :L] + qv[:, :L]
            den = wi * qc[:, L:] + qv[:, L:]
            h = num / jnp.maximum(jnp.abs(den), jnp.exp(-m_t))
            mu = jnp.mean(h, axis=-1, keepdims=True)
            hc = h - mu
            var = jnp.mean(hc * hc, axis=-1, keepdims=True)
            hn = hc * lax.rsqrt(var + LN_EPS) * ng_ref[:, hs]
            o = op_ref[0, pl.ds(r0, L), hs].astype(F32)
            y_ref[0, pl.ds(r0, L), hs] = (jax.nn.sigmoid(o) * hn).astype(BF16)

            f_l = f_c[L - 1:L, :]
            logw = f_l - f_r + i_row
            m_new = jnp.maximum(f_l + m_prev, jnp.max(logw, axis=-1, keepdims=True))
            decay = jnp.exp(f_l + m_prev - m_new)
            w_row = jnp.exp(logw - m_new)
            upd = jnp.dot((kt * w_row).astype(BF16), vaug, preferred_element_type=F32)
            c_sc[hh] = jnp.concatenate([decay, decay], axis=1) * caug + upd
            m_sc[hh] = jnp.broadcast_to(m_new, (8, L))
        return carry

    lax.fori_loop(0, S // L, chunk, 0)


def _mlstm(xm, op, gates, conv_w, conv_b, wq, wk, wkt, wv, fb, ng):
    B, S, _ = xm.shape
    full = lambda a: pl.BlockSpec(a.shape, lambda b_: (0,) * a.ndim)
    seq = pl.BlockSpec((1, S, ML_WIDTH), lambda b_: (b_, 0, 0))
    return pl.pallas_call(
        functools.partial(_mlstm_kernel, S=S),
        grid=(B,),
        in_specs=[seq, seq, pl.BlockSpec((1,) + gates.shape[1:], lambda b_: (b_, 0, 0, 0)),
                  full(conv_w), full(conv_b), full(wq), full(wk), full(wkt), full(wv), full(fb), full(ng)],
        out_specs=seq,
        out_shape=jax.ShapeDtypeStruct((B, S, ML_WIDTH), BF16),
        scratch_shapes=[pltpu.VMEM((S + 8, ML_WIDTH), F32), pltpu.VMEM((S, ML_WIDTH), BF16),
                        pltpu.VMEM((ML_HEADS, ML_DH, 2 * ML_DH), F32), pltpu.VMEM((ML_HEADS, 8, ML_CHUNK), F32)],
        compiler_params=_cparams(("arbitrary",)),
        name="mlstm",
    )(xm, op, gates, conv_w, conv_b, wq, wk, wkt, wv, fb, ng)


ROW_TILE = D_MODEL // LANES


def _store_token_tiles(ref, val):
    n = val.shape[0]
    for s in range(ROW_TILE):
        ref[pl.ds(s, n, stride=ROW_TILE), :] = val[:, s * LANES:(s + 1) * LANES]


def _load_token_tiles(ref, n):
    return jnp.concatenate([ref[pl.ds(s, n, stride=ROW_TILE), :] for s in range(ROW_TILE)], axis=1)


def _prep_proj_a(proj_a):
    pa = proj_a.reshape(N_HEADS, HEAD_DIM, D_MODEL)
    return jnp.concatenate([pa, jnp.zeros_like(pa)], axis=1).reshape(QP_W, D_MODEL).astype(BF16)


def _mix_kernel(o_ref, y_ref, gm_ref, x_ref, mod_ref, pa_ref, pb_ref, wo_ref, g2_ref, rwh_ref, rwl_ref, rb_ref,
                x1_ref, h2_ref, e_ref, w_ref, *, ts):
    nt = (((1,), (1,)), ((), ()))
    a = jnp.dot(o_ref[0], pa_ref[...], preferred_element_type=F32)
    bm = jnp.dot(y_ref[0], pb_ref[...], preferred_element_type=F32)
    ga = jax.nn.sigmoid(gm_ref[0, :, :D_MODEL].astype(F32))
    gb = jax.nn.sigmoid(gm_ref[0, :, D_MODEL:].astype(F32))
    mixed = jnp.dot((ga * a + gb * bm).astype(BF16), wo_ref[...], preferred_element_type=F32)
    x1 = x_ref[0] + mod_ref[0, 2:3, :] * mixed
    x1_ref[0] = x1
    ms = jnp.mean(x1 * x1, axis=-1, keepdims=True)
    h2 = x1 * lax.rsqrt(ms + RMS_EPS) * g2_ref[...] * (1.0 + mod_ref[0, 4:5, :]) + mod_ref[0, 3:4, :]
    _store_token_tiles(h2_ref, h2)
    hh, hl = _split_bf16(h2)
    logit = (lax.dot_general(rwh_ref[...], hh, nt, preferred_element_type=F32)
             + lax.dot_general(rwh_ref[...], hl, nt, preferred_element_type=F32)
             + lax.dot_general(rwl_ref[...], hh, nt, preferred_element_type=F32)) + rb_ref[...]
    eio = lax.broadcasted_iota(I32, (N_EXPERTS, ts), 0).astype(F32)
    vals, idxs = [], []
    for _ in range(TOP_K):
        mx = jnp.max(logit, axis=0, keepdims=True)
        ix = jnp.min(jnp.where(logit == mx, eio, 1e9), axis=0, keepdims=True)
        vals.append(mx)
        idxs.append(ix)
        logit = jnp.where(eio == ix, -3e38, logit)
    ex = [jnp.exp(v - vals[0]) for v in vals]
    tot = ex[0] + ex[1] + ex[2] + ex[3]
    e_ref[...] = jnp.concatenate(idxs, axis=0).astype(I32)
    wrows = jnp.concatenate([e_ / tot for e_ in ex] + [jnp.zeros((LANES - TOP_K, ts), F32)], axis=0)
    w_ref[...] = jnp.transpose(wrows)


def _mix(o_nsa, y_ml, gm, x, mod, pa, pb, wo, g2, rwh, rwl, rb, ts=512):
    B, S, _ = x.shape
    T = B * S
    nst = S // ts
    blk = lambda wd: pl.BlockSpec((1, ts, wd), lambda b_, s: (b_, s, 0))
    full = lambda a: pl.BlockSpec(a.shape, lambda b_, s: (0,) * a.ndim)
    return pl.pallas_call(
        functools.partial(_mix_kernel, ts=ts),
        grid=(B, nst),
        in_specs=[blk(QP_W), blk(ML_WIDTH), blk(2 * D_MODEL), blk(D_MODEL),
                  pl.BlockSpec((1, 6, D_MODEL), lambda b_, s: (b_, 0, 0)),
                  full(pa), full(pb), full(wo), full(g2), full(rwh), full(rwl), full(rb)],
        out_specs=[blk(D_MODEL), pl.BlockSpec((ts * ROW_TILE, LANES), lambda b_, s: (b_ * nst + s, 0)),
                   pl.BlockSpec((TOP_K, ts), lambda b_, s: (0, b_ * nst + s)),
                   pl.BlockSpec((ts, LANES), lambda b_, s: (b_ * nst + s, 0))],
        out_shape=[jax.ShapeDtypeStruct((B, S, D_MODEL), F32), jax.ShapeDtypeStruct((T * ROW_TILE, LANES), F32),
                   jax.ShapeDtypeStruct((TOP_K, T), I32), jax.ShapeDtypeStruct((T, LANES), F32)],
        compiler_params=_cparams(("arbitrary", "arbitrary")),
        name="mix_router",
    )(o_nsa, y_ml, gm, x, mod, pa, pb, wo, g2, rwh, rwl, rb)


MOE_TM = 256


def _moe_tiles(T):
    r = T * TOP_K
    return (r + N_EXPERTS * (MOE_TM - 1) + MOE_TM - 1) // MOE_TM


def _route_kernel(e_ref, dest_ref, te_ref, na_ref, *, ntp):
    nrow = e_ref.shape[0]
    eio = lax.broadcasted_iota(I32, (N_EXPERTS, LANES), 0)
    row = lax.broadcasted_iota(I32, (LANES, LANES), 0)
    col = lax.broadcasted_iota(I32, (LANES, LANES), 1)
    triu = jnp.where(row <= col, 1.0, 0.0).astype(BF16)
    ones = jnp.ones((LANES, LANES), BF16)

    def rank_body(i, counts):
        oh = eio == e_ref[pl.ds(i, 1), :]
        ohb = jnp.where(oh, 1.0, 0.0).astype(BF16)
        cum = jnp.dot(ohb, triu, preferred_element_type=F32)
        rank = jnp.sum(jnp.where(oh, cum - 1.0 + counts, 0.0), axis=0, keepdims=True)
        dest_ref[pl.ds(i, 1), :] = rank.astype(I32)
        return counts + jnp.dot(ohb, ones, preferred_element_type=F32)

    counts = lax.fori_loop(0, nrow, rank_body, jnp.zeros((N_EXPERTS, LANES), F32)).astype(I32)
    tiles = (counts + (MOE_TM - 1)) // MOE_TM
    incl = tiles
    for s in (1, 2, 4, 8, 16):
        incl = incl + jnp.where(eio >= s, pltpu.roll(incl, s, 0), 0)
    off = (incl - tiles) * MOE_TM

    def dest_body(i, carry):
        oh = eio == e_ref[pl.ds(i, 1), :]
        base = jnp.sum(jnp.where(oh, off, 0), axis=0, keepdims=True)
        dest_ref[pl.ds(i, 1), :] = dest_ref[pl.ds(i, 1), :] + base
        return carry

    lax.fori_loop(0, nrow, dest_body, 0)
    ends = jnp.concatenate([incl] * (ntp // LANES), axis=1)
    tix = lax.broadcasted_iota(I32, (N_EXPERTS, ntp), 1)
    te = jnp.sum(jnp.where(ends <= tix, 1, 0), axis=0, keepdims=True)
    te_ref[...] = jnp.minimum(te, N_EXPERTS - 1)
    na_ref[...] = incl[N_EXPERTS - 1:N_EXPERTS, :]


def _route(e_t):
    T = e_t.shape[1]
    nrow = TOP_K * T // LANES
    ntp = -(-_moe_tiles(T) // LANES) * LANES
    dest, te, na = pl.pallas_call(
        functools.partial(_route_kernel, ntp=ntp),
        out_shape=[jax.ShapeDtypeStruct((nrow, LANES), I32), jax.ShapeDtypeStruct((1, ntp), I32),
                   jax.ShapeDtypeStruct((1, LANES), I32)],
        compiler_params=pltpu.CompilerParams(vmem_limit_bytes=VMEM_LIMIT),
        name="route",
    )(e_t.reshape(nrow, LANES))
    return dest.reshape(TOP_K, T), te.reshape(ntp), na[0, :1]


MOE_TT = 256


def _row_copy(src_ref, s, dst_ref, d, sem):
    src = src_ref.at[pl.ds(pl.multiple_of(s * ROW_TILE, ROW_TILE), ROW_TILE), :]
    dst = dst_ref.at[pl.ds(pl.multiple_of(d * ROW_TILE, ROW_TILE), ROW_TILE), :]
    return pltpu.make_async_copy(src, dst, sem)


MOE_UNROLL = 4


def _dispatch_kernel(dest_ref, h_ref, xs_in_ref, xs_ref, sem):
    del xs_in_ref

    def issue(j, carry):
        for u in range(MOE_UNROLL):
            i = j * MOE_UNROLL + u
            for k in range(TOP_K):
                _row_copy(h_ref, i, xs_ref, dest_ref[k, i], sem).start(priority=k % 2)
        return carry

    lax.fori_loop(0, MOE_TT // MOE_UNROLL, issue, 0)
    for k in range(TOP_K):
        pltpu.make_async_copy(h_ref, xs_ref.at[pl.ds(0, MOE_TT * ROW_TILE), :], sem).wait()


def _dispatch(h2, dest, n_rows):
    T = h2.shape[0] // ROW_TILE
    xs0 = jnp.zeros((n_rows * ROW_TILE, LANES), F32)
    return pl.pallas_call(
        _dispatch_kernel,
        grid=(T // MOE_TT,),
        in_specs=[pl.BlockSpec((TOP_K, MOE_TT), lambda i: (0, i), memory_space=pltpu.SMEM),
                  pl.BlockSpec((MOE_TT * ROW_TILE, LANES), lambda i: (i, 0)),
                  pl.BlockSpec(memory_space=pl.ANY)],
        out_specs=pl.BlockSpec(memory_space=pl.ANY),
        out_shape=jax.ShapeDtypeStruct((n_rows * ROW_TILE, LANES), F32),
        scratch_shapes=[pltpu.SemaphoreType.DMA],
        input_output_aliases={2: 0},
        compiler_params=_cparams(("arbitrary",)),
        name="dispatch",
    )(dest, h2, xs0)


def _ffn_kernel(te_ref, na_ref, x_ref, wu_ref, bu_ref, wd_ref, bd_ref, y_ref, wu_sc, wd_sc):
    i = pl.program_id(0)
    fresh = jnp.logical_or(i == 0, te_ref[i] != te_ref[jnp.maximum(i - 1, 0)])

    @pl.when(fresh)
    def _():
        wu_sc[...] = wu_ref[0].astype(BF16)
        wd_sc[...] = wd_ref[0].astype(BF16)

    @pl.when(i < na_ref[0])
    def _():
        x = _load_token_tiles(x_ref, MOE_TM).astype(BF16)
        gu = jnp.dot(x, wu_sc[...], preferred_element_type=F32) + bu_ref[0]
        gl = jnp.minimum(gu[:, :D_MODEL], SWIGLU_LIMIT)
        lin = jnp.clip(gu[:, D_MODEL:], -SWIGLU_LIMIT, SWIGLU_LIMIT)
        a = gl * jax.nn.sigmoid(SWIGLU_ALPHA * gl) * (lin + 1.0)
        _store_token_tiles(y_ref, jnp.dot(a.astype(BF16), wd_sc[...], preferred_element_type=F32) + bd_ref[0])

    @pl.when(i >= na_ref[0])
    def _():
        y_ref[...] = jnp.zeros(y_ref.shape, F32)


def _ffn(xs, te, na, w_up, b_up, w_down, b_down):
    n_rows = xs.shape[0] // ROW_TILE
    nt = n_rows // MOE_TM
    de = w_up.shape[2]
    row_spec = pl.BlockSpec((MOE_TM * ROW_TILE, LANES), lambda i, te_, na_: (i, 0))
    grid_spec = pltpu.PrefetchScalarGridSpec(
        num_scalar_prefetch=2,
        grid=(nt,),
        in_specs=[row_spec,
                  pl.BlockSpec((1, D_MODEL, de), lambda i, te_, na_: (te_[i], 0, 0)),
                  pl.BlockSpec((1, 1, de), lambda i, te_, na_: (te_[i], 0, 0)),
                  pl.BlockSpec((1, de // 2, D_MODEL), lambda i, te_, na_: (te_[i], 0, 0)),
                  pl.BlockSpec((1, 1, D_MODEL), lambda i, te_, na_: (te_[i], 0, 0))],
        out_specs=row_spec,
        scratch_shapes=[pltpu.VMEM((D_MODEL, de), BF16), pltpu.VMEM((de // 2, D_MODEL), BF16)],
    )
    return pl.pallas_call(
        _ffn_kernel,
        grid_spec=grid_spec,
        out_shape=jax.ShapeDtypeStruct(xs.shape, F32),
        compiler_params=_cparams(("arbitrary",)),
        name="expert_ffn",
    )(te, na, xs, w_up, b_up.reshape(N_EXPERTS, 1, de), w_down, b_down.reshape(N_EXPERTS, 1, D_MODEL))


def _combine_kernel(dest_ref, y_ref, w_ref, x1_ref, mod_ref, fg_ref, o_ref, buf, sem):
    def issue(j, carry):
        for u in range(MOE_UNROLL):
            i = j * MOE_UNROLL + u
            for k in range(TOP_K):
                _row_copy(y_ref, dest_ref[k, i], buf.at[k], i, sem).start(priority=k % 2)
        return carry

    lax.fori_loop(0, MOE_TT // MOE_UNROLL, issue, 0)
    for k in range(TOP_K):
        pltpu.make_async_copy(y_ref.at[pl.ds(0, MOE_TT * ROW_TILE), :], buf.at[k], sem).wait()
    moe = w_ref[:, 0:1] * _load_token_tiles(buf.at[0], MOE_TT)
    for k in range(1, TOP_K):
        moe = moe + w_ref[:, k:k + 1] * _load_token_tiles(buf.at[k], MOE_TT)
    x2 = x1_ref[...] + mod_ref[0, 5:6, :] * moe
    ms = jnp.mean(x2 * x2, axis=-1, keepdims=True)
    o_ref[...] = x2 * lax.rsqrt(ms + RMS_EPS) * fg_ref[...]


def _combine(y, dest, wcol, x1, mod, final_g, S):
    T = x1.shape[0]
    per_b = S // MOE_TT
    return pl.pallas_call(
        _combine_kernel,
        grid=(T // MOE_TT,),
        in_specs=[pl.BlockSpec((TOP_K, MOE_TT), lambda i: (0, i), memory_space=pltpu.SMEM),
                  pl.BlockSpec(memory_space=pl.ANY),
                  pl.BlockSpec((MOE_TT, LANES), lambda i: (i, 0)),
                  pl.BlockSpec((MOE_TT, D_MODEL), lambda i: (i, 0)),
                  pl.BlockSpec((1, 6, D_MODEL), lambda i: (i // per_b, 0, 0)),
                  pl.BlockSpec((1, D_MODEL), lambda i: (0, 0))],
        out_specs=pl.BlockSpec((MOE_TT, D_MODEL), lambda i: (i, 0)),
        out_shape=jax.ShapeDtypeStruct((T, D_MODEL), F32),
        scratch_shapes=[pltpu.VMEM((TOP_K, MOE_TT * ROW_TILE, LANES), F32), pltpu.SemaphoreType.DMA],
        compiler_params=_cparams(("arbitrary",)),
        name="combine",
    )(dest, y, wcol, x1, mod, final_g.reshape(1, D_MODEL))


def _layer(x, c, ada_w, ada_b, norm1_g, w_in, b_in, cmp_pe_k, cmp_w1_k, cmp_w2_k, cmp_pe_v, cmp_w1_v, cmp_w2_v,
           ml_conv_w, ml_conv_b, ml_wq, ml_wk, ml_wv, ml_f_bias, ml_norm_g, proj_a, proj_b, w_out, norm2_g,
           router_w, router_b, exp_w_up, exp_b_up, exp_w_down, exp_b_down, final_g):
    B, S, D = x.shape
    T = B * S
    mod = _adaln(c, ada_w, ada_b).reshape(B, 6, D)
    w, b = _prep_w_in(w_in, b_in)
    q, kc_in, vc_in, kv6, xm, op, gm, small = _inproj(x, mod, norm1_g, w, b)
    prep = [_prep_compress(cmp_pe_k, cmp_w1_k, cmp_w2_k), _prep_compress(cmp_pe_v, cmp_w1_v, cmp_w2_v)]
    wk, pe, w2 = [jnp.stack([prep[0][i], prep[1][i]]) for i in range(3)]
    kc, vc = _compress(kc_in, vc_in, wk, pe, w2)
    ovt, ex = _nsa_consts(S)
    o_nsa = _nsa(q, kc, vc, kv6, small, ovt, ex)
    gates = jnp.transpose(small[:, :, 3 * N_HEADS:3 * N_HEADS + 2 * ML_HEADS], (0, 2, 1))
    gates = gates.reshape(B, 2 * ML_HEADS, S // ML_CHUNK, ML_CHUNK)
    y_ml = _mlstm(xm, op, gates, ml_conv_w, ml_conv_b.reshape(1, ML_WIDTH), ml_wq.astype(BF16), ml_wk.astype(BF16),
                  jnp.transpose(ml_wk, (0, 2, 1)).astype(BF16), ml_wv.astype(BF16),
                  jnp.broadcast_to(ml_f_bias[:, None], (ML_HEADS, LANES)), ml_norm_g.reshape(1, ML_WIDTH))
    rwt = jnp.transpose(router_w)
    rwh = rwt.astype(BF16)
    rwl = (rwt - rwh.astype(F32)).astype(BF16)
    x1, h2, e_t, wcol = _mix(o_nsa, y_ml, gm, x, mod, _prep_proj_a(proj_a), proj_b.astype(BF16), w_out.astype(BF16),
                             norm2_g.reshape(1, D), rwh, rwl, router_b.reshape(N_EXPERTS, 1))
    dest, te, na = _route(e_t)
    n_rows = _moe_tiles(T) * MOE_TM
    xs = _dispatch(h2, dest, n_rows)
    y = _ffn(xs, te, na, exp_w_up, exp_b_up, exp_w_down, exp_b_down)
    out = _combine(y, dest, wcol, x1.reshape(T, D), mod, final_g, S)
    return out.reshape(B, S, D)


def kernel(x, c, ada_w, ada_b, norm1_g, w_in, b_in, cmp_pe_k, cmp_w1_k, cmp_w2_k, cmp_pe_v, cmp_w1_v, cmp_w2_v, ml_conv_w, ml_conv_b, ml_wq, ml_wk, ml_wv, ml_f_bias, ml_norm_g, proj_a, proj_b, w_out, norm2_g, router_w, router_b, exp_w_up, exp_b_up, exp_w_down, exp_b_down, final_g):
    assert ada_w.shape[0] == 1, "one layer: the final RMSNorm is fused into the layer's last kernel"
    layer = (ada_w, ada_b, norm1_g, w_in, b_in, cmp_pe_k, cmp_w1_k, cmp_w2_k, cmp_pe_v, cmp_w1_v, cmp_w2_v,
             ml_conv_w, ml_conv_b, ml_wq, ml_wk, ml_wv, ml_f_bias, ml_norm_g, proj_a, proj_b, w_out, norm2_g,
             router_w, router_b, exp_w_up, exp_b_up, exp_w_down, exp_b_down)
    return _layer(x, c, *[p[0] for p in layer], final_g)
```

```python
import functools
import math

import numpy as np
import jax
import jax.numpy as jnp
from jax import lax
from jax.experimental import pallas as pl
from jax.experimental.pallas import tpu as pltpu

F32 = jnp.float32
BF16 = jnp.bfloat16
I32 = jnp.int32

D_MODEL = 1024
N_HEADS = 8
N_GROUPS = 2
HPG = N_HEADS // N_GROUPS
HEAD_DIM = 64
CMP_BLOCK = 32
CMP_STRIDE = 16
CMP_HIDDEN = 128
SEL_BLOCK = 64
SEL_TOPN = 8
WINDOW = 256
ATTN_SCALE = HEAD_DIM ** -0.5
ML_HEADS = 4
ML_DH = 128
ML_WIDTH = ML_HEADS * ML_DH
ML_CHUNK = 128
CONV_W = 4
N_EXPERTS = 32
TOP_K = 4
SWIGLU_LIMIT = 7.0
SWIGLU_ALPHA = 1.702
RMS_EPS = 1e-5
LN_EPS = 1e-5

LANES = 128
NEG = -1e30
VMEM_LIMIT = 56 * 1024 * 1024

QP_W = N_HEADS * LANES
C_Q = 0
C_KC = C_Q + QP_W
C_VC = C_KC + LANES
C_KV6 = C_VC + LANES
C_XM = C_KV6 + 6 * LANES
C_OP = C_XM + ML_WIDTH
C_GM = C_OP + ML_WIDTH
C_SM = C_GM + 2 * D_MODEL
IN_W = C_SM + LANES


def _cparams(sem):
    return pltpu.CompilerParams(dimension_semantics=sem, vmem_limit_bytes=VMEM_LIMIT)


def _adaln_kernel(c_ref, w_ref, b_ref, o_ref):
    c = c_ref[...]
    sc = (c * jax.nn.sigmoid(c)).astype(BF16)
    o_ref[...] = jnp.dot(sc, w_ref[...].astype(BF16), preferred_element_type=F32) + b_ref[...]


def _adaln(c, ada_w, ada_b):
    B = c.shape[0]
    tn = 1024
    n = ada_w.shape[1]
    return pl.pallas_call(
        _adaln_kernel,
        grid=(n // tn,),
        in_specs=[pl.BlockSpec((B, D_MODEL), lambda j: (0, 0)),
                  pl.BlockSpec((D_MODEL, tn), lambda j: (0, j)),
                  pl.BlockSpec((1, tn), lambda j: (0, j))],
        out_specs=pl.BlockSpec((B, tn), lambda j: (0, j)),
        out_shape=jax.ShapeDtypeStruct((B, n), F32),
        compiler_params=_cparams(("arbitrary",)),
        name="adaln",
    )(c, ada_w, ada_b.reshape(1, n))


def _prep_w_in(w_in, b_in):
    def pad_q(a):
        lead = a.shape[:-1]
        a = a.reshape(lead + (N_HEADS, HEAD_DIM))
        z = jnp.zeros_like(a[..., :HPG, :])
        lo = jnp.concatenate([a[..., :HPG, :], z], axis=-1)
        hi = jnp.concatenate([z, a[..., HPG:, :]], axis=-1)
        return jnp.concatenate([lo, hi], axis=-2).reshape(lead + (QP_W,))

    def v_aug(a, fill):
        f = jnp.full(a.shape[:-1] + (HEAD_DIM,), fill, a.dtype)
        return jnp.concatenate([a[..., :HEAD_DIM], f, a[..., HEAD_DIM:], f], axis=-1)

    def cols(a):
        sl = lambda lo, hi: lax.slice_in_dim(a, lo, hi, axis=-1)
        return dict(q=sl(0, 512), kc=sl(512, 640), vc=sl(640, 768), ks=sl(768, 896), vs=sl(896, 1024),
                    kw=sl(1024, 1152), vw=sl(1152, 1280), gn=sl(1280, 1304), xm=sl(1304, 1816),
                    op=sl(1816, 2328), fi=sl(2328, 2336), gm=sl(2336, 4384))

    def layout(c, fill):
        zs = jnp.zeros(c["q"].shape[:-1] + (LANES - 32,), c["q"].dtype)
        return jnp.concatenate([pad_q(c["q"]), c["kc"], c["vc"], c["ks"], c["kw"], v_aug(c["vs"], fill),
                                v_aug(c["vw"], fill), c["xm"], c["op"], c["gm"], c["gn"], c["fi"], zs], axis=-1)

    w = layout(cols(w_in), 0.0)
    b = layout(cols(b_in), 1.0)
    return w.astype(BF16), b.reshape(1, IN_W)


def _inproj_kernel(x_ref, mod_ref, g_ref, w_ref, b_ref,
                   q_ref, kc_ref, vc_ref, kv_ref, xm_ref, op_ref, gm_ref, sm_ref):
    x = x_ref[0]
    ms = jnp.mean(x * x, axis=-1, keepdims=True)
    y = x * lax.rsqrt(ms + RMS_EPS) * g_ref[...]
    h = (y * (1.0 + mod_ref[0, 1:2, :]) + mod_ref[0, 0:1, :]).astype(BF16)

    def seg(lo, hi):
        return jnp.dot(h, w_ref[:, lo:hi], preferred_element_type=F32) + b_ref[:, lo:hi]

    for hd in range(N_HEADS):
        q_ref[0, hd] = seg(C_Q + hd * LANES, C_Q + (hd + 1) * LANES).astype(BF16)
    kc_ref[0] = seg(C_KC, C_VC).astype(BF16)
    vc_ref[0] = seg(C_VC, C_KV6).astype(BF16)
    kv_ref[0] = seg(C_KV6, C_XM).astype(BF16)
    xm_ref[0] = seg(C_XM, C_OP)
    op_ref[0] = seg(C_OP, C_GM).astype(BF16)
    gm_ref[0] = seg(C_GM, C_SM).astype(BF16)
    sm_ref[0] = seg(C_SM, IN_W)


def _inproj(x, mod, norm_g, w, b, ts=512):
    B, S, _ = x.shape
    widths = [(LANES, BF16), (LANES, BF16), (6 * LANES, BF16), (ML_WIDTH, F32),
              (ML_WIDTH, BF16), (2 * D_MODEL, BF16), (LANES, F32)]
    q_spec = pl.BlockSpec((1, N_HEADS, ts, LANES), lambda b_, s: (b_, 0, s, 0))
    return pl.pallas_call(
        _inproj_kernel,
        grid=(B, S // ts),
        in_specs=[pl.BlockSpec((1, ts, D_MODEL), lambda b_, s: (b_, s, 0)),
                  pl.BlockSpec((1, 6, D_MODEL), lambda b_, s: (b_, 0, 0)),
                  pl.BlockSpec((1, D_MODEL), lambda b_, s: (0, 0)),
                  pl.BlockSpec((D_MODEL, IN_W), lambda b_, s: (0, 0)),
                  pl.BlockSpec((1, IN_W), lambda b_, s: (0, 0))],
        out_specs=[q_spec] + [pl.BlockSpec((1, ts, wd), lambda b_, s: (b_, s, 0)) for wd, _ in widths],
        out_shape=([jax.ShapeDtypeStruct((B, N_HEADS, S, LANES), BF16)]
                   + [jax.ShapeDtypeStruct((B, S, wd), dt) for wd, dt in widths]),
        compiler_params=_cparams(("arbitrary", "arbitrary")),
        name="inproj",
    )(x, mod, norm_g.reshape(1, D_MODEL), w, b)


def _prep_compress(pe, w1, w2):
    w1r = w1.reshape(2, CMP_STRIDE, HEAD_DIM, CMP_HIDDEN)
    z = jnp.zeros_like(w1r[0])
    cols = []
    for g in range(N_GROUPS):
        for a in range(2):
            parts = [w1r[a] if gg == g else z for gg in range(N_GROUPS)]
            cols.append(jnp.concatenate(parts, axis=1).reshape(CMP_STRIDE * LANES, CMP_HIDDEN))
    wk = jnp.concatenate(cols, axis=1)
    per = pe.reshape(2, CMP_STRIDE, HEAD_DIM)
    pe_rows = jnp.concatenate([per, per], axis=-1).reshape(2, CMP_STRIDE * LANES)
    pe_rows = jnp.concatenate([pe_rows, jnp.zeros((6, CMP_STRIDE * LANES), pe.dtype)], axis=0)
    z2 = jnp.zeros_like(w2)
    w2p = jnp.stack([jnp.concatenate([w2, z2], axis=1), jnp.concatenate([z2, w2], axis=1)])
    return wk.astype(BF16), pe_rows.astype(BF16), w2p.astype(BF16)


def _compress_kernel(ak_ref, av_ref, wk_ref, pe_ref, w2_ref, ok_ref, ov_ref):
    def hidden(i, a_ref, g):
        y = jnp.dot(a_ref[0], wk_ref[i], preferred_element_type=F32)
        yp = jnp.dot(pe_ref[i], wk_ref[i], preferred_element_type=F32)
        ca, cb = 2 * g * LANES, (2 * g + 1) * LANES
        bias = yp[0:1, ca:ca + LANES] + yp[1:2, cb:cb + LANES]
        hid = y[:, ca:ca + LANES] + pltpu.roll(y[:, cb:cb + LANES], LANES - 1, 0) + bias
        return jax.nn.gelu(hid, approximate=True).astype(BF16)

    ok_ref[0] = (jnp.dot(hidden(0, ak_ref, 0), w2_ref[0, 0], preferred_element_type=F32)
                 + jnp.dot(hidden(0, ak_ref, 1), w2_ref[0, 1], preferred_element_type=F32)).astype(BF16)
    ones_hi = jnp.where(lax.broadcasted_iota(I32, (LANES, LANES), 1) >= HEAD_DIM, 1.0, 0.0)
    for g in range(N_GROUPS):
        v = jnp.dot(hidden(1, av_ref, g), w2_ref[1, 0], preferred_element_type=F32)
        ov_ref[0, g] = (v + ones_hi).astype(BF16)


def _compress(kc_in, vc_in, wk, pe, w2):
    B, S, _ = kc_in.shape
    n = S // CMP_STRIDE
    kw = CMP_STRIDE * LANES
    a_spec = pl.BlockSpec((1, n, kw), lambda b_: (b_, 0, 0))
    full = lambda shp: pl.BlockSpec(shp, lambda b_: (0,) * len(shp))
    return pl.pallas_call(
        _compress_kernel,
        grid=(B,),
        in_specs=[a_spec, a_spec, full(wk.shape), full(pe.shape), full(w2.shape)],
        out_specs=[pl.BlockSpec((1, n, LANES), lambda b_: (b_, 0, 0)),
                   pl.BlockSpec((1, N_GROUPS, n, LANES), lambda b_: (b_, 0, 0, 0))],
        out_shape=[jax.ShapeDtypeStruct((B, n, LANES), BF16), jax.ShapeDtypeStruct((B, N_GROUPS, n, LANES), BF16)],
        compiler_params=_cparams(("arbitrary",)),
        name="compress",
    )(kc_in.reshape(B, n, kw), vc_in.reshape(B, n, kw), wk, pe, w2)


def _nsa_consts(S):
    n_cmp = (S - CMP_BLOCK) // CMP_STRIDE + 1
    n_slc = S // SEL_BLOCK
    cs = np.arange(n_cmp)[:, None] * CMP_STRIDE
    ss = np.arange(n_slc)[None, :] * SEL_BLOCK
    ov = np.clip(np.minimum(cs + CMP_BLOCK, ss + SEL_BLOCK) - np.maximum(cs, ss), 0, None) / CMP_BLOCK
    ovp = np.zeros((LANES, LANES), np.float32)
    ovp[:n_cmp, :n_slc] = ov
    nkt = S // LANES
    ex = np.zeros((nkt, LANES, LANES), np.float32)
    for kt in range(nkt):
        for l in range(LANES):
            ex[kt, (kt * LANES + l) // SEL_BLOCK, l] = 1.0
    return jnp.asarray(ovp.T, BF16), jnp.asarray(ex, BF16)


M_FLOOR = 0.5 * NEG


def _nsa_kernel(q_ref, kc_ref, vc_ref, ks_ref, kw_ref, vs_ref, vw_ref, gate_ref, ovt_ref, ex_ref,
                o_ref, s_sc, mx_sc, mb_sc, acc_sc, out_sc, *, tq):
    g = pl.program_id(1)
    t0 = pl.program_id(2) * tq
    n_slc = ks_ref.shape[1] // SEL_BLOCK
    tpos = t0 + lax.broadcasted_iota(I32, (tq, LANES), 0)
    lane = lax.broadcasted_iota(I32, (tq, LANES), 1)
    lane_row = lax.broadcasted_iota(I32, (1, LANES), 1)
    slopes = [jnp.where(g == 0, 2.0 ** -(hh + 1), 2.0 ** -(HPG + hh + 1)).astype(F32) for hh in range(HPG)]
    rows = [slice(hh * tq, (hh + 1) * tq) for hh in range(HPG)]
    nt = (((1,), (1,)), ((), ()))
    q4 = q_ref[0].reshape(HPG * tq, LANES) * ATTN_SCALE

    gates = jax.nn.sigmoid(gate_ref[0])
    gates = jnp.where(g == 0, gates, pltpu.roll(gates, LANES - HPG, 1))

    def gate(br, hh):
        c = br * N_HEADS + hh
        return jnp.broadcast_to(gates[:, c:c + 1], (tq, LANES))

    def denom(pv):
        return jnp.where(lane < HEAD_DIM, pltpu.roll(pv, HEAD_DIM, 1), pv)

    s_all = lax.dot_general(q4, kc_ref[0], nt, preferred_element_type=F32)
    cmask = jnp.where(lane * CMP_STRIDE + (CMP_BLOCK - 1) <= tpos, 0.0, NEG)
    crel = (lane_row * CMP_STRIDE - t0).astype(F32)
    vca = vc_ref[0, 0]
    psum = jnp.zeros((tq, LANES), F32)
    for hh in range(HPG):
        s = s_all[rows[hh]] + (cmask + slopes[hh] * crel)
        m = jnp.maximum(jnp.max(s, axis=-1, keepdims=True), M_FLOOR)
        p = jnp.exp(s - m)
        pv = jnp.dot(p.astype(BF16), vca, preferred_element_type=F32)
        inv = 1.0 / jnp.maximum(denom(pv), 1e-30)
        psum = psum + p * inv
        out_sc[rows[hh], :] = gate(0, hh) * (pv * inv)

    ph, plo = _split_bf16(psum)
    imp = (lax.dot_general(ovt_ref[...], ph, nt, preferred_element_type=F32)
           + lax.dot_general(ovt_ref[...], plo, nt, preferred_element_type=F32))[0:n_slc]
    blk = lax.broadcasted_iota(I32, (n_slc, tq), 0)
    cur = (t0 + lax.broadcasted_iota(I32, (n_slc, tq), 1)) // SEL_BLOCK
    blkf = blk.astype(F32)
    sc = jnp.where((blk == cur) | (blk == 0), -NEG, jnp.where(blk > cur, NEG, imp))
    selt = jnp.zeros((n_slc, tq), F32)
    for _ in range(SEL_TOPN):
        mx = jnp.max(sc, axis=0, keepdims=True)
        idx = jnp.min(jnp.where(sc == mx, blkf, 1e9), axis=0, keepdims=True)
        hit = blkf == idx
        selt = jnp.where(hit, 1.0, selt)
        sc = jnp.where(hit, -3e38, sc)
    selt = jnp.where(blk <= cur, selt, 0.0)
    sel = jnp.transpose(jnp.concatenate([selt, jnp.zeros((LANES - n_slc, tq), F32)], axis=0)).astype(BF16)

    KT = NSA_KCHUNK

    def scores(tiles):
        s_t = [lax.dot_general(q4, k_t, nt, preferred_element_type=F32) for _, k_t, _, _ in tiles]
        for hh in range(HPG):
            best = mx_sc[rows[hh], :]
            for (slot, _, kpos0, mask), s_all in zip(tiles, s_t):
                rel = (kpos0 + lane_row - t0).astype(F32)
                s = s_all[rows[hh]] + (mask + slopes[hh] * rel)
                s_sc[slot, rows[hh], :] = s
                best = jnp.maximum(best, s)
            mx_sc[rows[hh], :] = best

    def finish_max():
        for hh in range(HPG):
            m = jnp.maximum(jnp.max(mx_sc[rows[hh], :], axis=-1, keepdims=True), M_FLOOR)
            mb_sc[rows[hh], :] = jnp.broadcast_to(m, (tq, LANES))

    def weighted(c, v_c):
        for hh in range(HPG):
            s4 = jnp.concatenate([s_sc[KT * c + j, rows[hh], :] for j in range(KT)], axis=1)
            mb = mb_sc[rows[hh], :]
            p = jnp.exp(s4 - jnp.concatenate([mb] * KT, axis=1)).astype(BF16)
            acc_sc[rows[hh], :] += jnp.dot(p, v_c, preferred_element_type=F32)

    mx_sc[...] = jnp.full(mx_sc.shape, NEG, F32)
    acc_sc[...] = jnp.zeros(acc_sc.shape, F32)
    n_chunks = (t0 + tq + KT * LANES - 1) // (KT * LANES)

    def slc_scores(kp, carry):
        tiles = []
        for j in range(2):
            kt = 2 * kp + j
            r0 = pl.multiple_of(kt * LANES, LANES)
            picked = jnp.dot(sel, ex_ref[kt], preferred_element_type=F32)
            mask = jnp.where((picked > 0.5) & (r0 + lane <= tpos), 0.0, NEG)
            tiles.append((kt, ks_ref[0, pl.ds(r0, LANES), :], r0, mask))
        scores(tiles)
        return carry

    lax.fori_loop(0, (KT // 2) * n_chunks, slc_scores, 0)
    finish_max()

    def slc_weighted(c, carry):
        r0 = pl.multiple_of(c * KT * LANES, KT * LANES)
        weighted(c, vs_ref[0, pl.ds(r0, KT * LANES), :])
        return carry

    lax.fori_loop(0, n_chunks, slc_weighted, 0)
    for hh in range(HPG):
        pv = acc_sc[rows[hh], :]
        out_sc[rows[hh], :] += gate(1, hh) * (pv / denom(pv))

    mx_sc[...] = jnp.full(mx_sc.shape, NEG, F32)
    acc_sc[...] = jnp.zeros(acc_sc.shape, F32)
    v_tiles, tiles = [], []
    for r in range(KT):
        kt = t0 // LANES - WINDOW // LANES + r
        r0 = pl.multiple_of(jnp.maximum(kt, 0) * LANES, LANES)
        spos = kt * LANES + lane
        dist = tpos - spos
        mask = jnp.where((dist >= 0) & (dist < WINDOW) & (spos >= 0), 0.0, NEG)
        tiles.append((r, kw_ref[0, pl.ds(r0, LANES), :], kt * LANES, mask))
        v_tiles.append(vw_ref[0, pl.ds(r0, LANES), :])
    scores(tiles[:2])
    scores(tiles[2:])
    finish_max()
    weighted(0, jnp.concatenate(v_tiles, axis=0))
    for hh in range(HPG):
        pv = acc_sc[rows[hh], :]
        o = out_sc[rows[hh], :] + gate(2, hh) * (pv / denom(pv))
        o_ref[0, :, hh * LANES:(hh + 1) * LANES] = o.astype(BF16)


NSA_KCHUNK = 4
NSA_TQ = (NSA_KCHUNK - WINDOW // LANES) * LANES


def _nsa(q, kc, vc, kv6, small, ovt, ex):
    B, _, S, _ = q.shape
    tq = NSA_TQ
    gw = HPG * LANES
    seq = lambda f: pl.BlockSpec((1, S, LANES), f)
    return pl.pallas_call(
        functools.partial(_nsa_kernel, tq=tq),
        grid=(B, N_GROUPS, S // tq),
        in_specs=[pl.BlockSpec((1, HPG, tq, LANES), lambda b_, g, i: (b_, g, i, 0)),
                  pl.BlockSpec((1, LANES, LANES), lambda b_, g, i: (b_, 0, 0)),
                  pl.BlockSpec((1, 1, LANES, LANES), lambda b_, g, i: (b_, g, 0, 0)),
                  seq(lambda b_, g, i: (b_, 0, 0)), seq(lambda b_, g, i: (b_, 0, 1)),
                  seq(lambda b_, g, i: (b_, 0, 2 + g)), seq(lambda b_, g, i: (b_, 0, 4 + g)),
                  pl.BlockSpec((1, tq, LANES), lambda b_, g, i: (b_, i, 0)),
                  pl.BlockSpec(ovt.shape, lambda b_, g, i: (0, 0)),
                  pl.BlockSpec(ex.shape, lambda b_, g, i: (0, 0, 0))],
        out_specs=pl.BlockSpec((1, tq, gw), lambda b_, g, i: (b_, i, g)),
        out_shape=jax.ShapeDtypeStruct((B, S, QP_W), BF16),
        scratch_shapes=[pltpu.VMEM((S // LANES, HPG * tq, LANES), F32)] + [pltpu.VMEM((HPG * tq, LANES), F32)] * 4,
        compiler_params=_cparams(("arbitrary", "arbitrary", "arbitrary")),
        name="nsa",
    )(q, kc, vc, kv6, kv6, kv6, kv6, small, ovt, ex)


def _split_bf16(a):
    hi = a.astype(BF16)
    return hi, (a - hi.astype(F32)).astype(BF16)


def _mlstm_kernel(xm_ref, op_ref, gate_ref, cw_ref, cb_ref, wq_ref, wk_ref, wkt_ref, wv_ref, fb_ref, ng_ref,
                  y_ref, xpad_sc, xc_sc, c_sc, m_sc, q_sc, k_sc, v_sc, kt_sc, fr_sc, fc_sc, *, S):
    L = ML_CHUNK
    nt = (((1,), (1,)), ((), ()))
    xpad_sc[0:8, :] = jnp.zeros((8, ML_WIDTH), F32)
    xpad_sc[8:, :] = xm_ref[0]
    rb = 256
    for r in range(S // rb):
        acc = jnp.zeros((rb, ML_WIDTH), F32) + cb_ref[...]
        for j in range(CONV_W):
            o = 8 + r * rb - (CONV_W - 1) + j
            acc = acc + cw_ref[j:j + 1, :] * xpad_sc[o:o + rb, :]
        xc_sc[r * rb:(r + 1) * rb, :] = (acc * jax.nn.sigmoid(acc)).astype(BF16)

    c_sc[...] = jnp.zeros(c_sc.shape, F32)
    m_sc[...] = jnp.zeros(m_sc.shape, F32)
    row = lax.broadcasted_iota(I32, (L, L), 0)
    col = lax.broadcasted_iota(I32, (L, L), 1)
    causal = col <= row
    triu = jnp.where(row <= col, 1.0, 0.0).astype(BF16)
    ones = jnp.ones((L, L), BF16)
    rs = 1.0 / math.sqrt(ML_DH)
    nc = S // L

    for hh in range(ML_HEADS):
        hs = slice(hh * ML_DH, (hh + 1) * ML_DH)
        xc = xc_sc[:, hs]
        q_sc[:, hs] = jnp.dot(xc, wq_ref[hh], preferred_element_type=F32).astype(BF16)
        k_sc[:, hs] = (jnp.dot(xc, wk_ref[hh], preferred_element_type=F32) * rs).astype(BF16)
        v_sc[:, hs] = jnp.dot(xm_ref[0, :, hs].astype(BF16), wv_ref[hh], preferred_element_type=F32).astype(BF16)
        kt = lax.dot_general(wkt_ref[hh], xc, nt, preferred_element_type=F32) * rs
        for c in range(nc):
            kt_sc[c, hh] = kt[:, c * L:(c + 1) * L]

    def gate_sums(c, carry):
        for hh in range(ML_HEADS):
            z = gate_ref[0, ML_HEADS + hh, pl.ds(c, 1), :] + fb_ref[hh:hh + 1, :]
            lf_row = jnp.minimum(z, 0.0) - jnp.log(1.0 + jnp.exp(-jnp.abs(z)))
            rh, rl = _split_bf16(jnp.broadcast_to(lf_row, (8, L)))
            fr_sc[c, hh] = (jnp.dot(rh, triu, preferred_element_type=F32)
                            + jnp.dot(rl, triu, preferred_element_type=F32))
            lh, ll = _split_bf16(jnp.where(causal, jnp.broadcast_to(lf_row, (L, L)), 0.0))
            fc_sc[c, hh] = (jnp.dot(lh, ones, preferred_element_type=F32)
                            + jnp.dot(ll, ones, preferred_element_type=F32))
        return carry

    lax.fori_loop(0, nc, gate_sums, 0)

    def chunk(c, carry):
        r0 = pl.multiple_of(c * L, L)
        for hh in range(ML_HEADS):
            hs = slice(hh * ML_DH, (hh + 1) * ML_DH)
            q = q_sc[pl.ds(r0, L), hs]
            k = k_sc[pl.ds(r0, L), hs]
            kt = kt_sc[c, hh]
            vaug = jnp.concatenate([v_sc[pl.ds(r0, L), hs], ones], axis=1)
            i_row = gate_ref[0, hh, pl.ds(c, 1), :]
            f_r = fr_sc[c, hh, 0:1, :]
            f_c = fc_sc[c, hh]
            m_prev = m_sc[hh, 0:1, :]
            logd = jnp.where(causal, f_c - f_r + i_row, NEG)
            inter = f_c + m_prev
            m_t = jnp.maximum(inter, jnp.max(logd, axis=-1, keepdims=True))
            dm = jnp.where(causal, jnp.exp(logd - m_t), 0.0)
            wi = jnp.exp(inter - m_t)
            qk = (lax.dot_general(q, k, nt, preferred_element_type=F32) * dm).astype(BF16)
            caug = c_sc[hh]
            qc = jnp.dot(q, caug.astype(BF16), preferred_element_type=F32)
            qv = jnp.dot(qk, vaug, preferred_element_type=F32)
            num = wi * qc[:, :L] + qv[:, :L]
            den = wi * qc[:, L:] + qv[:, L:]
            h = num / jnp.maximum(jnp.abs(den), jnp.exp(-m_t))
            mu = jnp.mean(h, axis=-1, keepdims=True)
            hc = h - mu
            var = jnp.mean(hc * hc, axis=-1, keepdims=True)
            hn = hc * lax.rsqrt(var + LN_EPS) * ng_ref[:, hs]
            o = op_ref[0, pl.ds(r0, L), hs].astype(F32)
            y_ref[0, pl.ds(r0, L), hs] = (jax.nn.sigmoid(o) * hn).astype(BF16)

            f_l = f_c[L - 1:L, :]
            logw = f_l - f_r + i_row
            m_new = jnp.maximum(f_l + m_prev, jnp.max(logw, axis=-1, keepdims=True))
            decay = jnp.exp(f_l + m_prev - m_new)
            w_row = jnp.exp(logw - m_new)
            upd = jnp.dot((kt * w_row).astype(BF16), vaug, preferred_element_type=F32)
            c_sc[hh] = jnp.concatenate([decay, decay], axis=1) * caug + upd
            m_sc[hh] = jnp.broadcast_to(m_new, (8, L))
        return carry

    lax.fori_loop(0, S // L, chunk, 0)


def _mlstm(xm, op, gates, conv_w, conv_b, wq, wk, wkt, wv, fb, ng):
    B, S, _ = xm.shape
    full = lambda a: pl.BlockSpec(a.shape, lambda b_: (0,) * a.ndim)
    seq = pl.BlockSpec((1, S, ML_WIDTH), lambda b_: (b_, 0, 0))
    return pl.pallas_call(
        functools.partial(_mlstm_kernel, S=S),
        grid=(B,),
        in_specs=[seq, seq, pl.BlockSpec((1,) + gates.shape[1:], lambda b_: (b_, 0, 0, 0)),
                  full(conv_w), full(conv_b), full(wq), full(wk), full(wkt), full(wv), full(fb), full(ng)],
        out_specs=seq,
        out_shape=jax.ShapeDtypeStruct((B, S, ML_WIDTH), BF16),
        scratch_shapes=[pltpu.VMEM((S + 8, ML_WIDTH), F32), pltpu.VMEM((S, ML_WIDTH), BF16),
                        pltpu.VMEM((ML_HEADS, ML_DH, 2 * ML_DH), F32), pltpu.VMEM((ML_HEADS, 8, ML_CHUNK), F32)]
                       + [pltpu.VMEM((S, ML_WIDTH), BF16)] * 3
                       + [pltpu.VMEM((S // ML_CHUNK, ML_HEADS, ML_DH, ML_CHUNK), F32),
                          pltpu.VMEM((S // ML_CHUNK, ML_HEADS, 8, ML_CHUNK), F32),
                          pltpu.VMEM((S // ML_CHUNK, ML_HEADS, ML_CHUNK, ML_CHUNK), F32)],
        compiler_params=_cparams(("arbitrary",)),
        name="mlstm",
    )(xm, op, gates, conv_w, conv_b, wq, wk, wkt, wv, fb, ng)


ROW_TILE = D_MODEL // LANES


def _store_token_tiles(ref, val):
    n = val.shape[0]
    for s in range(ROW_TILE):
        ref[pl.ds(s, n, stride=ROW_TILE), :] = val[:, s * LANES:(s + 1) * LANES]


def _load_token_tiles(ref, n):
    return jnp.concatenate([ref[pl.ds(s, n, stride=ROW_TILE), :] for s in range(ROW_TILE)], axis=1)


def _prep_proj_a(proj_a):
    pa = proj_a.reshape(N_HEADS, HEAD_DIM, D_MODEL)
    return jnp.concatenate([pa, jnp.zeros_like(pa)], axis=1).reshape(QP_W, D_MODEL).astype(BF16)


def _mix_kernel(o_ref, y_ref, gm_ref, x_ref, mod_ref, pa_ref, pb_ref, wo_ref, g2_ref, rwh_ref, rwl_ref, rb_ref,
                x1_ref, h2_ref, e_ref, w_ref, *, ts):
    nt = (((1,), (1,)), ((), ()))
    a = jnp.dot(o_ref[0], pa_ref[...], preferred_element_type=F32)
    bm = jnp.dot(y_ref[0], pb_ref[...], preferred_element_type=F32)
    ga = jax.nn.sigmoid(gm_ref[0, :, :D_MODEL].astype(F32))
    gb = jax.nn.sigmoid(gm_ref[0, :, D_MODEL:].astype(F32))
    mixed = jnp.dot((ga * a + gb * bm).astype(BF16), wo_ref[...], preferred_element_type=F32)
    x1 = x_ref[0] + mod_ref[0, 2:3, :] * mixed
    x1_ref[0] = x1
    ms = jnp.mean(x1 * x1, axis=-1, keepdims=True)
    h2 = x1 * lax.rsqrt(ms + RMS_EPS) * g2_ref[...] * (1.0 + mod_ref[0, 4:5, :]) + mod_ref[0, 3:4, :]
    _store_token_tiles(h2_ref, h2)
    hh, hl = _split_bf16(h2)
    logit = (lax.dot_general(rwh_ref[...], hh, nt, preferred_element_type=F32)
             + lax.dot_general(rwh_ref[...], hl, nt, preferred_element_type=F32)
             + lax.dot_general(rwl_ref[...], hh, nt, preferred_element_type=F32)) + rb_ref[...]
    eio = lax.broadcasted_iota(I32, (N_EXPERTS, ts), 0).astype(F32)
    vals, idxs = [], []
    for _ in range(TOP_K):
        mx = jnp.max(logit, axis=0, keepdims=True)
        ix = jnp.min(jnp.where(logit == mx, eio, 1e9), axis=0, keepdims=True)
        vals.append(mx)
        idxs.append(ix)
        logit = jnp.where(eio == ix, -3e38, logit)
    ex = [jnp.exp(v - vals[0]) for v in vals]
    tot = ex[0] + ex[1] + ex[2] + ex[3]
    e_ref[...] = jnp.concatenate(idxs, axis=0).astype(I32)
    wrows = jnp.concatenate([e_ / tot for e_ in ex] + [jnp.zeros((LANES - TOP_K, ts), F32)], axis=0)
    w_ref[...] = jnp.transpose(wrows)


def _mix(o_nsa, y_ml, gm, x, mod, pa, pb, wo, g2, rwh, rwl, rb, ts=512):
    B, S, _ = x.shape
    T = B * S
    nst = S // ts
    blk = lambda wd: pl.BlockSpec((1, ts, wd), lambda b_, s: (b_, s, 0))
    full = lambda a: pl.BlockSpec(a.shape, lambda b_, s: (0,) * a.ndim)
    return pl.pallas_call(
        functools.partial(_mix_kernel, ts=ts),
        grid=(B, nst),
        in_specs=[blk(QP_W), blk(ML_WIDTH), blk(2 * D_MODEL), blk(D_MODEL),
                  pl.BlockSpec((1, 6, D_MODEL), lambda b_, s: (b_, 0, 0)),
                  full(pa), full(pb), full(wo), full(g2), full(rwh), full(rwl), full(rb)],
        out_specs=[blk(D_MODEL), pl.BlockSpec((ts * ROW_TILE, LANES), lambda b_, s: (b_ * nst + s, 0)),
                   pl.BlockSpec((TOP_K, ts), lambda b_, s: (0, b_ * nst + s)),
                   pl.BlockSpec((ts, LANES), lambda b_, s: (b_ * nst + s, 0))],
        out_shape=[jax.ShapeDtypeStruct((B, S, D_MODEL), F32), jax.ShapeDtypeStruct((T * ROW_TILE, LANES), F32),
                   jax.ShapeDtypeStruct((TOP_K, T), I32), jax.ShapeDtypeStruct((T, LANES), F32)],
        compiler_params=_cparams(("arbitrary", "arbitrary")),
        name="mix_router",
    )(o_nsa, y_ml, gm, x, mod, pa, pb, wo, g2, rwh, rwl, rb)


MOE_TM = 256


def _moe_tiles(T):
    r = T * TOP_K
    return (r + N_EXPERTS * (MOE_TM - 1) + MOE_TM - 1) // MOE_TM


def _route_kernel(e_ref, dest_ref, te_ref, na_ref, *, ntp):
    nrow = e_ref.shape[0]
    eio = lax.broadcasted_iota(I32, (N_EXPERTS, LANES), 0)
    row = lax.broadcasted_iota(I32, (LANES, LANES), 0)
    col = lax.broadcasted_iota(I32, (LANES, LANES), 1)
    triu = jnp.where(row <= col, 1.0, 0.0).astype(BF16)
    ones = jnp.ones((LANES, LANES), BF16)

    ru = 8

    def rank_body(i, counts):
        r0 = pl.multiple_of(i * ru, ru)
        e_blk = e_ref[pl.ds(r0, ru), :]
        ranks = []
        for r in range(ru):
            oh = eio == e_blk[r:r + 1, :]
            ohb = jnp.where(oh, 1.0, 0.0).astype(BF16)
            cum = jnp.dot(ohb, triu, preferred_element_type=F32)
            ranks.append(jnp.sum(jnp.where(oh, cum - 1.0 + counts, 0.0), axis=0, keepdims=True))
            counts = counts + jnp.dot(ohb, ones, preferred_element_type=F32)
        dest_ref[pl.ds(r0, ru), :] = jnp.concatenate(ranks, axis=0).astype(I32)
        return counts

    counts = lax.fori_loop(0, nrow // ru, rank_body, jnp.zeros((N_EXPERTS, LANES), F32)).astype(I32)
    tiles = (counts + (MOE_TM - 1)) // MOE_TM
    incl = tiles
    for s in (1, 2, 4, 8, 16):
        incl = incl + jnp.where(eio >= s, pltpu.roll(incl, s, 0), 0)
    off = (incl - tiles) * MOE_TM

    def dest_body(i, carry):
        r0 = pl.multiple_of(i * ru, ru)
        e_blk = e_ref[pl.ds(r0, ru), :]
        base = [jnp.sum(jnp.where(eio == e_blk[r:r + 1, :], off, 0), axis=0, keepdims=True) for r in range(ru)]
        dest_ref[pl.ds(r0, ru), :] = dest_ref[pl.ds(r0, ru), :] + jnp.concatenate(base, axis=0)
        return carry

    lax.fori_loop(0, nrow // ru, dest_body, 0)
    ends = jnp.concatenate([incl] * (ntp // LANES), axis=1)
    tix = lax.broadcasted_iota(I32, (N_EXPERTS, ntp), 1)
    te = jnp.sum(jnp.where(ends <= tix, 1, 0), axis=0, keepdims=True)
    te_ref[...] = jnp.minimum(te, N_EXPERTS - 1)
    na_ref[...] = incl


def _route(e_t):
    T = e_t.shape[1]
    nrow = TOP_K * T // LANES
    ntp = -(-_moe_tiles(T) // LANES) * LANES
    dest, te, ends = pl.pallas_call(
        functools.partial(_route_kernel, ntp=ntp),
        out_shape=[jax.ShapeDtypeStruct((nrow, LANES), I32), jax.ShapeDtypeStruct((1, ntp), I32),
                   jax.ShapeDtypeStruct((N_EXPERTS, LANES), I32)],
        compiler_params=pltpu.CompilerParams(vmem_limit_bytes=VMEM_LIMIT),
        name="route",
    )(e_t.reshape(nrow, LANES))
    return dest.reshape(TOP_K, T), te.reshape(ntp), ends[:, 0]


MOE_TT = 256


def _row_copy(src_ref, s, dst_ref, d, sem):
    src = src_ref.at[pl.ds(pl.multiple_of(s * ROW_TILE, ROW_TILE), ROW_TILE), :]
    dst = dst_ref.at[pl.ds(pl.multiple_of(d * ROW_TILE, ROW_TILE), ROW_TILE), :]
    return pltpu.make_async_copy(src, dst, sem)


MOE_UNROLL = 4


def _dispatch_kernel(dest_ref, h_ref, xs_in_ref, xs_ref, sem):
    del xs_in_ref

    def issue(j, carry):
        for u in range(MOE_UNROLL):
            i = j * MOE_UNROLL + u
            for k in range(TOP_K):
                _row_copy(h_ref, i, xs_ref, dest_ref[k, i], sem).start(priority=k % 2)
        return carry

    lax.fori_loop(0, MOE_TT // MOE_UNROLL, issue, 0)
    for k in range(TOP_K):
        pltpu.make_async_copy(h_ref, xs_ref.at[pl.ds(0, MOE_TT * ROW_TILE), :], sem).wait()


def _dispatch(h2, dest, xs0):
    T = h2.shape[0] // ROW_TILE
    n_rows = xs0.shape[0] // ROW_TILE
    return pl.pallas_call(
        _dispatch_kernel,
        grid=(T // MOE_TT,),
        in_specs=[pl.BlockSpec((TOP_K, MOE_TT), lambda i: (0, i), memory_space=pltpu.SMEM),
                  pl.BlockSpec((MOE_TT * ROW_TILE, LANES), lambda i: (i, 0)),
                  pl.BlockSpec(memory_space=pl.ANY)],
        out_specs=pl.BlockSpec(memory_space=pl.ANY),
        out_shape=jax.ShapeDtypeStruct((n_rows * ROW_TILE, LANES), F32),
        scratch_shapes=[pltpu.SemaphoreType.DMA],
        input_output_aliases={2: 0},
        compiler_params=_cparams(("arbitrary",)),
        name="dispatch",
    )(dest, h2, xs0)


def _ffn_kernel(te_ref, na_ref, ends_ref, x_ref, wu_hbm, bu_ref, wd_hbm, bd_ref, y_ref,
                wu_f32, wd_f32, wu_sc, wd_sc, slot_ref, sems):
    i = pl.program_id(0)
    e = te_ref[i]
    n_act = na_ref[0]
    fresh = jnp.logical_and(i < n_act, jnp.logical_or(i == 0, e != te_ref[jnp.maximum(i - 1, 0)]))

    def fetch(expert, slot):
        return (pltpu.make_async_copy(wu_hbm.at[expert], wu_f32.at[slot], sems.at[0, slot]),
                pltpu.make_async_copy(wd_hbm.at[expert], wd_f32.at[slot], sems.at[1, slot]))

    @pl.when(i == 0)
    def _():
        slot_ref[0] = 0
        for cp in fetch(e, 0):
            cp.start()

    @pl.when(fresh)
    def _():
        slot = slot_ref[0]
        nxt = ends_ref[e]

        @pl.when(nxt < n_act)
        def _():
            for cp in fetch(te_ref[nxt], 1 - slot):
                cp.start()

        for cp in fetch(e, slot):
            cp.wait()
        wu_sc[...] = wu_f32[slot].astype(BF16)
        wd_sc[...] = wd_f32[slot].astype(BF16)
        slot_ref[0] = 1 - slot

    @pl.when(i < n_act)
    def _():
        x = _load_token_tiles(x_ref, MOE_TM).astype(BF16)
        gu = jnp.dot(x, wu_sc[...], preferred_element_type=F32) + bu_ref[0]
        gl = jnp.minimum(gu[:, :D_MODEL], SWIGLU_LIMIT)
        lin = jnp.clip(gu[:, D_MODEL:], -SWIGLU_LIMIT, SWIGLU_LIMIT)
        a = gl * jax.nn.sigmoid(SWIGLU_ALPHA * gl) * (lin + 1.0)
        _store_token_tiles(y_ref, jnp.dot(a.astype(BF16), wd_sc[...], preferred_element_type=F32) + bd_ref[0])

    @pl.when(i >= n_act)
    def _():
        y_ref[...] = jnp.zeros(y_ref.shape, F32)


def _ffn(xs, te, na, ends, w_up, b_up, w_down, b_down):
    n_rows = xs.shape[0] // ROW_TILE
    nt = n_rows // MOE_TM
    de = w_up.shape[2]
    row_spec = pl.BlockSpec((MOE_TM * ROW_TILE, LANES), lambda i, te_, na_, en_: (i, 0))
    x_spec = pl.BlockSpec((MOE_TM * ROW_TILE, LANES), lambda i, te_, na_, en_: (jnp.minimum(i, na_[0] - 1), 0))
    grid_spec = pltpu.PrefetchScalarGridSpec(
        num_scalar_prefetch=3,
        grid=(nt,),
        in_specs=[x_spec,
                  pl.BlockSpec(memory_space=pl.ANY),
                  pl.BlockSpec((1, 1, de), lambda i, te_, na_, en_: (te_[i], 0, 0)),
                  pl.BlockSpec(memory_space=pl.ANY),
                  pl.BlockSpec((1, 1, D_MODEL), lambda i, te_, na_, en_: (te_[i], 0, 0))],
        out_specs=row_spec,
        scratch_shapes=[pltpu.VMEM((2, D_MODEL, de), F32), pltpu.VMEM((2, de // 2, D_MODEL), F32),
                        pltpu.VMEM((D_MODEL, de), BF16), pltpu.VMEM((de // 2, D_MODEL), BF16),
                        pltpu.SMEM((1,), I32), pltpu.SemaphoreType.DMA((2, 2))],
    )
    return pl.pallas_call(
        _ffn_kernel,
        grid_spec=grid_spec,
        out_shape=jax.ShapeDtypeStruct(xs.shape, F32),
        compiler_params=_cparams(("arbitrary",)),
        name="expert_ffn",
    )(te, na, ends, xs, w_up, b_up.reshape(N_EXPERTS, 1, de), w_down, b_down.reshape(N_EXPERTS, 1, D_MODEL))


def _combine_kernel(dest_ref, y_ref, w_ref, x1_ref, mod_ref, fg_ref, o_ref, buf, sem):
    def issue(j, carry):
        for u in range(MOE_UNROLL):
            i = j * MOE_UNROLL + u
            for k in range(TOP_K):
                _row_copy(y_ref, dest_ref[k, i], buf.at[k], i, sem).start(priority=k % 2)
        return carry

    lax.fori_loop(0, MOE_TT // MOE_UNROLL, issue, 0)
    for k in range(TOP_K):
        pltpu.make_async_copy(y_ref.at[pl.ds(0, MOE_TT * ROW_TILE), :], buf.at[k], sem).wait()
    moe = w_ref[:, 0:1] * _load_token_tiles(buf.at[0], MOE_TT)
    for k in range(1, TOP_K):
        moe = moe + w_ref[:, k:k + 1] * _load_token_tiles(buf.at[k], MOE_TT)
    x2 = x1_ref[...] + mod_ref[0, 5:6, :] * moe
    ms = jnp.mean(x2 * x2, axis=-1, keepdims=True)
    o_ref[...] = x2 * lax.rsqrt(ms + RMS_EPS) * fg_ref[...]


def _combine(y, dest, wcol, x1, mod, final_g, S):
    T = x1.shape[0]
    per_b = S // MOE_TT
    return pl.pallas_call(
        _combine_kernel,
        grid=(T // MOE_TT,),
        in_specs=[pl.BlockSpec((TOP_K, MOE_TT), lambda i: (0, i), memory_space=pltpu.SMEM),
                  pl.BlockSpec(memory_space=pl.ANY),
                  pl.BlockSpec((MOE_TT, LANES), lambda i: (i, 0)),
                  pl.BlockSpec((MOE_TT, D_MODEL), lambda i: (i, 0)),
                  pl.BlockSpec((1, 6, D_MODEL), lambda i: (i // per_b, 0, 0)),
                  pl.BlockSpec((1, D_MODEL), lambda i: (0, 0))],
        out_specs=pl.BlockSpec((MOE_TT, D_MODEL), lambda i: (i, 0)),
        out_shape=jax.ShapeDtypeStruct((T, D_MODEL), F32),
        scratch_shapes=[pltpu.VMEM((TOP_K, MOE_TT * ROW_TILE, LANES), F32), pltpu.SemaphoreType.DMA],
        compiler_params=_cparams(("arbitrary",)),
        name="combine",
    )(dest, y, wcol, x1, mod, final_g.reshape(1, D_MODEL))


def _layer(x, c, ada_w, ada_b, norm1_g, w_in, b_in, cmp_pe_k, cmp_w1_k, cmp_w2_k, cmp_pe_v, cmp_w1_v, cmp_w2_v,
           ml_conv_w, ml_conv_b, ml_wq, ml_wk, ml_wv, ml_f_bias, ml_norm_g, proj_a, proj_b, w_out, norm2_g,
           router_w, router_b, exp_w_up, exp_b_up, exp_w_down, exp_b_down, final_g):
    B, S, D = x.shape
    T = B * S
    mod = _adaln(c, ada_w, ada_b).reshape(B, 6, D)
    w, b = _prep_w_in(w_in, b_in)
    q, kc_in, vc_in, kv6, xm, op, gm, small = _inproj(x, mod, norm1_g, w, b)
    prep = [_prep_compress(cmp_pe_k, cmp_w1_k, cmp_w2_k), _prep_compress(cmp_pe_v, cmp_w1_v, cmp_w2_v)]
    wk, pe, w2 = [jnp.stack([prep[0][i], prep[1][i]]) for i in range(3)]
    kc, vc = _compress(kc_in, vc_in, wk, pe, w2)
    ovt, ex = _nsa_consts(S)
    o_nsa = _nsa(q, kc, vc, kv6, small, ovt, ex)
    gates = jnp.transpose(small[:, :, 3 * N_HEADS:3 * N_HEADS + 2 * ML_HEADS], (0, 2, 1))
    gates = gates.reshape(B, 2 * ML_HEADS, S // ML_CHUNK, ML_CHUNK)
    y_ml = _mlstm(xm, op, gates, ml_conv_w, ml_conv_b.reshape(1, ML_WIDTH), ml_wq.astype(BF16), ml_wk.astype(BF16),
                  jnp.transpose(ml_wk, (0, 2, 1)).astype(BF16), ml_wv.astype(BF16),
                  jnp.broadcast_to(ml_f_bias[:, None], (ML_HEADS, LANES)), ml_norm_g.reshape(1, ML_WIDTH))
    rwt = jnp.transpose(router_w)
    rwh = rwt.astype(BF16)
    rwl = (rwt - rwh.astype(F32)).astype(BF16)
    x1, h2, e_t, wcol = _mix(o_nsa, y_ml, gm, x, mod, _prep_proj_a(proj_a), proj_b.astype(BF16), w_out.astype(BF16),
                             norm2_g.reshape(1, D), rwh, rwl, router_b.reshape(N_EXPERTS, 1))
    dest, te, ends = _route(e_t)
    n_rows = _moe_tiles(T) * MOE_TM
    xs = _dispatch(h2, dest, jnp.zeros((n_rows * ROW_TILE, LANES), F32))
    y = _ffn(xs, te, ends[N_EXPERTS - 1:], ends, exp_w_up, exp_b_up, exp_w_down, exp_b_down)
    out = _combine(y, dest, wcol, x1.reshape(T, D), mod, final_g, S)
    return out.reshape(B, S, D)


def kernel(x, c, ada_w, ada_b, norm1_g, w_in, b_in, cmp_pe_k, cmp_w1_k, cmp_w2_k, cmp_pe_v, cmp_w1_v, cmp_w2_v, ml_conv_w, ml_conv_b, ml_wq, ml_wk, ml_wv, ml_f_bias, ml_norm_g, proj_a, proj_b, w_out, norm2_g, router_w, router_b, exp_w_up, exp_b_up, exp_w_down, exp_b_down, final_g):
    assert ada_w.shape[0] == 1, "one layer: the final RMSNorm is fused into the layer's last kernel"
    layer = (ada_w, ada_b, norm1_g, w_in, b_in, cmp_pe_k, cmp_w1_k, cmp_w2_k, cmp_pe_v, cmp_w1_v, cmp_w2_v,
             ml_conv_w, ml_conv_b, ml_wq, ml_wk, ml_wv, ml_f_bias, ml_norm_g, proj_a, proj_b, w_out, norm2_g,
             router_w, router_b, exp_w_up, exp_b_up, exp_w_down, exp_b_down)
    return _layer(x, c, *[p[0] for p in layer], final_g)
```

```python
import functools
import math

import numpy as np
import jax
import jax.numpy as jnp
from jax import lax
from jax.experimental import pallas as pl
from jax.experimental.pallas import tpu as pltpu

F32 = jnp.float32
BF16 = jnp.bfloat16
I32 = jnp.int32

D_MODEL = 1024
N_HEADS = 8
N_GROUPS = 2
HPG = N_HEADS // N_GROUPS
HEAD_DIM = 64
CMP_BLOCK = 32
CMP_STRIDE = 16
CMP_HIDDEN = 128
SEL_BLOCK = 64
SEL_TOPN = 8
WINDOW = 256
ATTN_SCALE = HEAD_DIM ** -0.5
ML_HEADS = 4
ML_DH = 128
ML_WIDTH = ML_HEADS * ML_DH
ML_CHUNK = 128
CONV_W = 4
N_EXPERTS = 32
TOP_K = 4
SWIGLU_LIMIT = 7.0
SWIGLU_ALPHA = 1.702
RMS_EPS = 1e-5
LN_EPS = 1e-5

LANES = 128
NEG = -1e30
VMEM_LIMIT = 56 * 1024 * 1024

QP_W = N_HEADS * LANES
C_Q = 0
C_KC = C_Q + QP_W
C_VC = C_KC + LANES
C_KV6 = C_VC + LANES
C_XM = C_KV6 + 6 * LANES
C_OP = C_XM + ML_WIDTH
C_GM = C_OP + ML_WIDTH
C_SM = C_GM + 2 * D_MODEL
IN_W = C_SM + LANES


def _cparams(sem):
    return pltpu.CompilerParams(dimension_semantics=sem, vmem_limit_bytes=VMEM_LIMIT)


def _adaln_kernel(c_ref, w_ref, b_ref, o_ref):
    c = c_ref[...]
    sc = (c * jax.nn.sigmoid(c)).astype(BF16)
    o_ref[...] = jnp.dot(sc, w_ref[...].astype(BF16), preferred_element_type=F32) + b_ref[...]


def _adaln(c, ada_w, ada_b):
    B = c.shape[0]
    tn = 1024
    n = ada_w.shape[1]
    return pl.pallas_call(
        _adaln_kernel,
        grid=(n // tn,),
        in_specs=[pl.BlockSpec((B, D_MODEL), lambda j: (0, 0)),
                  pl.BlockSpec((D_MODEL, tn), lambda j: (0, j)),
                  pl.BlockSpec((1, tn), lambda j: (0, j))],
        out_specs=pl.BlockSpec((B, tn), lambda j: (0, j)),
        out_shape=jax.ShapeDtypeStruct((B, n), F32),
        compiler_params=_cparams(("arbitrary",)),
        name="adaln",
    )(c, ada_w, ada_b.reshape(1, n))


def _prep_w_in(w_in, b_in):
    def pad_q(a):
        lead = a.shape[:-1]
        a = a.reshape(lead + (N_HEADS, HEAD_DIM))
        z = jnp.zeros_like(a[..., :HPG, :])
        lo = jnp.concatenate([a[..., :HPG, :], z], axis=-1)
        hi = jnp.concatenate([z, a[..., HPG:, :]], axis=-1)
        return jnp.concatenate([lo, hi], axis=-2).reshape(lead + (QP_W,))

    def v_aug(a, fill):
        f = jnp.full(a.shape[:-1] + (HEAD_DIM,), fill, a.dtype)
        return jnp.concatenate([a[..., :HEAD_DIM], f, a[..., HEAD_DIM:], f], axis=-1)

    def cols(a):
        sl = lambda lo, hi: lax.slice_in_dim(a, lo, hi, axis=-1)
        return dict(q=sl(0, 512), kc=sl(512, 640), vc=sl(640, 768), ks=sl(768, 896), vs=sl(896, 1024),
                    kw=sl(1024, 1152), vw=sl(1152, 1280), gn=sl(1280, 1304), xm=sl(1304, 1816),
                    op=sl(1816, 2328), fi=sl(2328, 2336), gm=sl(2336, 4384))

    def layout(c, fill):
        zs = jnp.zeros(c["q"].shape[:-1] + (LANES - 32,), c["q"].dtype)
        return jnp.concatenate([pad_q(c["q"]), c["kc"], c["vc"], c["ks"], c["kw"], v_aug(c["vs"], fill),
                                v_aug(c["vw"], fill), c["xm"], c["op"], c["gm"], c["gn"], c["fi"], zs], axis=-1)

    w = layout(cols(w_in), 0.0)
    b = layout(cols(b_in), 1.0)
    return w.astype(BF16), b.reshape(1, IN_W)


def _inproj_kernel(x_ref, mod_ref, g_ref, w_ref, b_ref,
                   q_ref, kc_ref, vc_ref, kv_ref, xm_ref, op_ref, gm_ref, sm_ref):
    x = x_ref[0]
    ms = jnp.mean(x * x, axis=-1, keepdims=True)
    y = x * lax.rsqrt(ms + RMS_EPS) * g_ref[...]
    h = (y * (1.0 + mod_ref[0, 1:2, :]) + mod_ref[0, 0:1, :]).astype(BF16)

    def seg(lo, hi):
        return jnp.dot(h, w_ref[:, lo:hi], preferred_element_type=F32) + b_ref[:, lo:hi]

    for hd in range(N_HEADS):
        q_ref[0, hd] = seg(C_Q + hd * LANES, C_Q + (hd + 1) * LANES).astype(BF16)
    kc_ref[0] = seg(C_KC, C_VC).astype(BF16)
    vc_ref[0] = seg(C_VC, C_KV6).astype(BF16)
    kv_ref[0] = seg(C_KV6, C_XM).astype(BF16)
    xm_ref[0] = seg(C_XM, C_OP)
    op_ref[0] = seg(C_OP, C_GM).astype(BF16)
    gm_ref[0] = seg(C_GM, C_SM).astype(BF16)
    sm_ref[0] = seg(C_SM, IN_W)


def _inproj(x, mod, norm_g, w, b, ts=512):
    B, S, _ = x.shape
    widths = [(LANES, BF16), (LANES, BF16), (6 * LANES, BF16), (ML_WIDTH, F32),
              (ML_WIDTH, BF16), (2 * D_MODEL, BF16), (LANES, F32)]
    q_spec = pl.BlockSpec((1, N_HEADS, ts, LANES), lambda b_, s: (b_, 0, s, 0))
    return pl.pallas_call(
        _inproj_kernel,
        grid=(B, S // ts),
        in_specs=[pl.BlockSpec((1, ts, D_MODEL), lambda b_, s: (b_, s, 0)),
                  pl.BlockSpec((1, 6, D_MODEL), lambda b_, s: (b_, 0, 0)),
                  pl.BlockSpec((1, D_MODEL), lambda b_, s: (0, 0)),
                  pl.BlockSpec((D_MODEL, IN_W), lambda b_, s: (0, 0)),
                  pl.BlockSpec((1, IN_W), lambda b_, s: (0, 0))],
        out_specs=[q_spec] + [pl.BlockSpec((1, ts, wd), lambda b_, s: (b_, s, 0)) for wd, _ in widths],
        out_shape=([jax.ShapeDtypeStruct((B, N_HEADS, S, LANES), BF16)]
                   + [jax.ShapeDtypeStruct((B, S, wd), dt) for wd, dt in widths]),
        compiler_params=_cparams(("arbitrary", "arbitrary")),
        name="inproj",
    )(x, mod, norm_g.reshape(1, D_MODEL), w, b)


def _prep_compress(pe, w1, w2):
    w1r = w1.reshape(2, CMP_STRIDE, HEAD_DIM, CMP_HIDDEN)
    z = jnp.zeros_like(w1r[0])
    cols = []
    for g in range(N_GROUPS):
        for a in range(2):
            parts = [w1r[a] if gg == g else z for gg in range(N_GROUPS)]
            cols.append(jnp.concatenate(parts, axis=1).reshape(CMP_STRIDE * LANES, CMP_HIDDEN))
    wk = jnp.concatenate(cols, axis=1)
    per = pe.reshape(2, CMP_STRIDE, HEAD_DIM)
    pe_rows = jnp.concatenate([per, per], axis=-1).reshape(2, CMP_STRIDE * LANES)
    pe_rows = jnp.concatenate([pe_rows, jnp.zeros((6, CMP_STRIDE * LANES), pe.dtype)], axis=0)
    z2 = jnp.zeros_like(w2)
    w2p = jnp.stack([jnp.concatenate([w2, z2], axis=1), jnp.concatenate([z2, w2], axis=1)])
    return wk.astype(BF16), pe_rows.astype(BF16), w2p.astype(BF16)


def _compress_kernel(ak_ref, av_ref, wk_ref, pe_ref, w2_ref, ok_ref, ov_ref):
    def hidden(i, a_ref, g):
        y = jnp.dot(a_ref[0], wk_ref[i], preferred_element_type=F32)
        yp = jnp.dot(pe_ref[i], wk_ref[i], preferred_element_type=F32)
        ca, cb = 2 * g * LANES, (2 * g + 1) * LANES
        bias = yp[0:1, ca:ca + LANES] + yp[1:2, cb:cb + LANES]
        hid = y[:, ca:ca + LANES] + pltpu.roll(y[:, cb:cb + LANES], LANES - 1, 0) + bias
        return jax.nn.gelu(hid, approximate=True).astype(BF16)

    ok_ref[0] = (jnp.dot(hidden(0, ak_ref, 0), w2_ref[0, 0], preferred_element_type=F32)
                 + jnp.dot(hidden(0, ak_ref, 1), w2_ref[0, 1], preferred_element_type=F32)).astype(BF16)
    ones_hi = jnp.where(lax.broadcasted_iota(I32, (LANES, LANES), 1) >= HEAD_DIM, 1.0, 0.0)
    for g in range(N_GROUPS):
        v = jnp.dot(hidden(1, av_ref, g), w2_ref[1, 0], preferred_element_type=F32)
        ov_ref[0, g] = (v + ones_hi).astype(BF16)


def _compress(kc_in, vc_in, wk, pe, w2):
    B, S, _ = kc_in.shape
    n = S // CMP_STRIDE
    kw = CMP_STRIDE * LANES
    a_spec = pl.BlockSpec((1, n, kw), lambda b_: (b_, 0, 0))
    full = lambda shp: pl.BlockSpec(shp, lambda b_: (0,) * len(shp))
    return pl.pallas_call(
        _compress_kernel,
        grid=(B,),
        in_specs=[a_spec, a_spec, full(wk.shape), full(pe.shape), full(w2.shape)],
        out_specs=[pl.BlockSpec((1, n, LANES), lambda b_: (b_, 0, 0)),
                   pl.BlockSpec((1, N_GROUPS, n, LANES), lambda b_: (b_, 0, 0, 0))],
        out_shape=[jax.ShapeDtypeStruct((B, n, LANES), BF16), jax.ShapeDtypeStruct((B, N_GROUPS, n, LANES), BF16)],
        compiler_params=_cparams(("arbitrary",)),
        name="compress",
    )(kc_in.reshape(B, n, kw), vc_in.reshape(B, n, kw), wk, pe, w2)


def _nsa_consts(S):
    n_cmp = (S - CMP_BLOCK) // CMP_STRIDE + 1
    n_slc = S // SEL_BLOCK
    cs = np.arange(n_cmp)[:, None] * CMP_STRIDE
    ss = np.arange(n_slc)[None, :] * SEL_BLOCK
    ov = np.clip(np.minimum(cs + CMP_BLOCK, ss + SEL_BLOCK) - np.maximum(cs, ss), 0, None) / CMP_BLOCK
    ovp = np.zeros((LANES, LANES), np.float32)
    ovp[:n_cmp, :n_slc] = ov
    nkt = S // LANES
    ex = np.zeros((nkt, LANES, LANES), np.float32)
    for kt in range(nkt):
        for l in range(LANES):
            ex[kt, (kt * LANES + l) // SEL_BLOCK, l] = 1.0
    return jnp.asarray(ovp.T, BF16), jnp.asarray(ex, BF16)


M_FLOOR = 0.5 * NEG


def _nsa_kernel(q_ref, kc_ref, vc_ref, ks_ref, kw_ref, vs_ref, vw_ref, gate_ref, ovt_ref, ex_ref,
                o_ref, s_sc, mx_sc, mb_sc, acc_sc, out_sc, *, tq):
    g = pl.program_id(1)
    t0 = pl.program_id(2) * tq
    n_slc = ks_ref.shape[1] // SEL_BLOCK
    tpos = t0 + lax.broadcasted_iota(I32, (tq, LANES), 0)
    lane = lax.broadcasted_iota(I32, (tq, LANES), 1)
    lane_row = lax.broadcasted_iota(I32, (1, LANES), 1)
    slopes = [jnp.where(g == 0, 2.0 ** -(hh + 1), 2.0 ** -(HPG + hh + 1)).astype(F32) for hh in range(HPG)]
    rows = [slice(hh * tq, (hh + 1) * tq) for hh in range(HPG)]
    nt = (((1,), (1,)), ((), ()))
    q4 = q_ref[0].reshape(HPG * tq, LANES) * ATTN_SCALE

    gates = jax.nn.sigmoid(gate_ref[0])
    gates = jnp.where(g == 0, gates, pltpu.roll(gates, LANES - HPG, 1))

    def gate(br, hh):
        c = br * N_HEADS + hh
        return jnp.broadcast_to(gates[:, c:c + 1], (tq, LANES))

    def denom(pv):
        return jnp.where(lane < HEAD_DIM, pltpu.roll(pv, HEAD_DIM, 1), pv)

    s_all = lax.dot_general(q4, kc_ref[0], nt, preferred_element_type=F32)
    cmask = jnp.where(lane * CMP_STRIDE + (CMP_BLOCK - 1) <= tpos, 0.0, NEG)
    crel = (lane_row * CMP_STRIDE - t0).astype(F32)
    vca = vc_ref[0, 0]
    psum = jnp.zeros((tq, LANES), F32)
    for hh in range(HPG):
        s = s_all[rows[hh]] + (cmask + slopes[hh] * crel)
        m = jnp.maximum(jnp.max(s, axis=-1, keepdims=True), M_FLOOR)
        p = jnp.exp(s - m)
        pv = jnp.dot(p.astype(BF16), vca, preferred_element_type=F32)
        inv = 1.0 / jnp.maximum(denom(pv), 1e-30)
        psum = psum + p * inv
        out_sc[rows[hh], :] = gate(0, hh) * (pv * inv)

    ph, plo = _split_bf16(psum)
    imp = (lax.dot_general(ovt_ref[...], ph, nt, preferred_element_type=F32)
           + lax.dot_general(ovt_ref[...], plo, nt, preferred_element_type=F32))[0:n_slc]
    blk = lax.broadcasted_iota(I32, (n_slc, tq), 0)
    cur = (t0 + lax.broadcasted_iota(I32, (n_slc, tq), 1)) // SEL_BLOCK
    blkf = blk.astype(F32)
    sc = jnp.where((blk == cur) | (blk == 0), -NEG, jnp.where(blk > cur, NEG, imp))
    selt = jnp.zeros((n_slc, tq), F32)
    for _ in range(SEL_TOPN):
        mx = jnp.max(sc, axis=0, keepdims=True)
        idx = jnp.min(jnp.where(sc == mx, blkf, 1e9), axis=0, keepdims=True)
        hit = blkf == idx
        selt = jnp.where(hit, 1.0, selt)
        sc = jnp.where(hit, -3e38, sc)
    selt = jnp.where(blk <= cur, selt, 0.0)
    sel = jnp.transpose(jnp.concatenate([selt, jnp.zeros((LANES - n_slc, tq), F32)], axis=0)).astype(BF16)

    KT = NSA_KCHUNK

    def scores(tiles):
        s_t = [lax.dot_general(q4, k_t, nt, preferred_element_type=F32) for _, k_t, _, _ in tiles]
        for hh in range(HPG):
            best = mx_sc[rows[hh], :]
            for (slot, _, kpos0, mask), s_all in zip(tiles, s_t):
                rel = (kpos0 + lane_row - t0).astype(F32)
                s = s_all[rows[hh]] + (mask + slopes[hh] * rel)
                s_sc[slot, rows[hh], :] = s
                best = jnp.maximum(best, s)
            mx_sc[rows[hh], :] = best

    def finish_max():
        for hh in range(HPG):
            m = jnp.maximum(jnp.max(mx_sc[rows[hh], :], axis=-1, keepdims=True), M_FLOOR)
            mb_sc[rows[hh], :] = jnp.broadcast_to(m, (tq, LANES))

    def weighted(c, v_c):
        for hh in range(HPG):
            s4 = jnp.concatenate([s_sc[KT * c + j, rows[hh], :] for j in range(KT)], axis=1)
            mb = mb_sc[rows[hh], :]
            p = jnp.exp(s4 - jnp.concatenate([mb] * KT, axis=1)).astype(BF16)
            acc_sc[rows[hh], :] += jnp.dot(p, v_c, preferred_element_type=F32)

    mx_sc[...] = jnp.full(mx_sc.shape, NEG, F32)
    acc_sc[...] = jnp.zeros(acc_sc.shape, F32)
    n_chunks = (t0 + tq + KT * LANES - 1) // (KT * LANES)

    def slc_scores(kp, carry):
        tiles = []
        for j in range(2):
            kt = 2 * kp + j
            r0 = pl.multiple_of(kt * LANES, LANES)
            picked = jnp.dot(sel, ex_ref[kt], preferred_element_type=F32)
            mask = jnp.where((picked > 0.5) & (r0 + lane <= tpos), 0.0, NEG)
            tiles.append((kt, ks_ref[0, pl.ds(r0, LANES), :], r0, mask))
        scores(tiles)
        return carry

    lax.fori_loop(0, (KT // 2) * n_chunks, slc_scores, 0)
    finish_max()

    def slc_weighted(c, carry):
        r0 = pl.multiple_of(c * KT * LANES, KT * LANES)
        weighted(c, vs_ref[0, pl.ds(r0, KT * LANES), :])
        return carry

    lax.fori_loop(0, n_chunks, slc_weighted, 0)
    for hh in range(HPG):
        pv = acc_sc[rows[hh], :]
        out_sc[rows[hh], :] += gate(1, hh) * (pv / denom(pv))

    mx_sc[...] = jnp.full(mx_sc.shape, NEG, F32)
    acc_sc[...] = jnp.zeros(acc_sc.shape, F32)
    v_tiles, tiles = [], []
    for r in range(KT):
        kt = t0 // LANES - WINDOW // LANES + r
        r0 = pl.multiple_of(jnp.maximum(kt, 0) * LANES, LANES)
        spos = kt * LANES + lane
        dist = tpos - spos
        mask = jnp.where((dist >= 0) & (dist < WINDOW) & (spos >= 0), 0.0, NEG)
        tiles.append((r, kw_ref[0, pl.ds(r0, LANES), :], kt * LANES, mask))
        v_tiles.append(vw_ref[0, pl.ds(r0, LANES), :])
    scores(tiles[:2])
    scores(tiles[2:])
    finish_max()
    weighted(0, jnp.concatenate(v_tiles, axis=0))
    for hh in range(HPG):
        pv = acc_sc[rows[hh], :]
        o = out_sc[rows[hh], :] + gate(2, hh) * (pv / denom(pv))
        o_ref[0, :, hh * LANES:(hh + 1) * LANES] = o.astype(BF16)


NSA_KCHUNK = 4
NSA_TQ = (NSA_KCHUNK - WINDOW // LANES) * LANES


def _nsa(q, kc, vc, kv6, small, ovt, ex):
    B, _, S, _ = q.shape
    tq = NSA_TQ
    gw = HPG * LANES
    seq = lambda f: pl.BlockSpec((1, S, LANES), f)
    return pl.pallas_call(
        functools.partial(_nsa_kernel, tq=tq),
        grid=(B, N_GROUPS, S // tq),
        in_specs=[pl.BlockSpec((1, HPG, tq, LANES), lambda b_, g, i: (b_, g, i, 0)),
                  pl.BlockSpec((1, LANES, LANES), lambda b_, g, i: (b_, 0, 0)),
                  pl.BlockSpec((1, 1, LANES, LANES), lambda b_, g, i: (b_, g, 0, 0)),
                  seq(lambda b_, g, i: (b_, 0, 0)), seq(lambda b_, g, i: (b_, 0, 1)),
                  seq(lambda b_, g, i: (b_, 0, 2 + g)), seq(lambda b_, g, i: (b_, 0, 4 + g)),
                  pl.BlockSpec((1, tq, LANES), lambda b_, g, i: (b_, i, 0)),
                  pl.BlockSpec(ovt.shape, lambda b_, g, i: (0, 0)),
                  pl.BlockSpec(ex.shape, lambda b_, g, i: (0, 0, 0))],
        out_specs=pl.BlockSpec((1, tq, gw), lambda b_, g, i: (b_, i, g)),
        out_shape=jax.ShapeDtypeStruct((B, S, QP_W), BF16),
        scratch_shapes=[pltpu.VMEM((S // LANES, HPG * tq, LANES), F32)] + [pltpu.VMEM((HPG * tq, LANES), F32)] * 4,
        compiler_params=_cparams(("arbitrary", "arbitrary", "arbitrary")),
        name="nsa",
    )(q, kc, vc, kv6, kv6, kv6, kv6, small, ovt, ex)


def _split_bf16(a):
    hi = a.astype(BF16)
    return hi, (a - hi.astype(F32)).astype(BF16)


def _mlstm_kernel(xm_ref, op_ref, gate_ref, cw_ref, cb_ref, wq_ref, wk_ref, wkt_ref, wv_ref, fb_ref, ng_ref,
                  y_ref, xpad_sc, xc_sc, c_sc, m_sc, q_sc, k_sc, v_sc, kt_sc, fr_sc, fc_sc, *, S):
    L = ML_CHUNK
    nt = (((1,), (1,)), ((), ()))
    xpad_sc[0:8, :] = jnp.zeros((8, ML_WIDTH), F32)
    xpad_sc[8:, :] = xm_ref[0]
    rb = 256
    for r in range(S // rb):
        acc = jnp.zeros((rb, ML_WIDTH), F32) + cb_ref[...]
        for j in range(CONV_W):
            o = 8 + r * rb - (CONV_W - 1) + j
            acc = acc + cw_ref[j:j + 1, :] * xpad_sc[o:o + rb, :]
        xc_sc[r * rb:(r + 1) * rb, :] = (acc * jax.nn.sigmoid(acc)).astype(BF16)

    c_sc[...] = jnp.zeros(c_sc.shape, F32)
    m_sc[...] = jnp.zeros(m_sc.shape, F32)
    row = lax.broadcasted_iota(I32, (L, L), 0)
    col = lax.broadcasted_iota(I32, (L, L), 1)
    causal = col <= row
    triu = jnp.where(row <= col, 1.0, 0.0).astype(BF16)
    ones = jnp.ones((L, L), BF16)
    rs = 1.0 / math.sqrt(ML_DH)
    nc = S // L

    for hh in range(ML_HEADS):
        hs = slice(hh * ML_DH, (hh + 1) * ML_DH)
        xc = xc_sc[:, hs]
        q_sc[:, hs] = jnp.dot(xc, wq_ref[hh], preferred_element_type=F32).astype(BF16)
        k_sc[:, hs] = (jnp.dot(xc, wk_ref[hh], preferred_element_type=F32) * rs).astype(BF16)
        v_sc[:, hs] = jnp.dot(xm_ref[0, :, hs].astype(BF16), wv_ref[hh], preferred_element_type=F32).astype(BF16)
        kt = lax.dot_general(wkt_ref[hh], xc, nt, preferred_element_type=F32) * rs
        for c in range(nc):
            kt_sc[c, hh] = kt[:, c * L:(c + 1) * L]

    def gate_sums(c, carry):
        for hh in range(ML_HEADS):
            z = gate_ref[0, ML_HEADS + hh, pl.ds(c, 1), :] + fb_ref[hh:hh + 1, :]
            lf_row = jnp.minimum(z, 0.0) - jnp.log(1.0 + jnp.exp(-jnp.abs(z)))
            rh, rl = _split_bf16(jnp.broadcast_to(lf_row, (8, L)))
            fr_sc[c, hh] = (jnp.dot(rh, triu, preferred_element_type=F32)
                            + jnp.dot(rl, triu, preferred_element_type=F32))
            lh, ll = _split_bf16(jnp.where(causal, jnp.broadcast_to(lf_row, (L, L)), 0.0))
            fc_sc[c, hh] = (jnp.dot(lh, ones, preferred_element_type=F32)
                            + jnp.dot(ll, ones, preferred_element_type=F32))
        return carry

    lax.fori_loop(0, nc, gate_sums, 0)

    def chunk(c, carry):
        r0 = pl.multiple_of(c * L, L)
        for hh in range(ML_HEADS):
            hs = slice(hh * ML_DH, (hh + 1) * ML_DH)
            q = q_sc[pl.ds(r0, L), hs]
            k = k_sc[pl.ds(r0, L), hs]
            kt = kt_sc[c, hh]
            vaug = jnp.concatenate([v_sc[pl.ds(r0, L), hs], ones], axis=1)
            i_row = gate_ref[0, hh, pl.ds(c, 1), :]
            f_r = fr_sc[c, hh, 0:1, :]
            f_c = fc_sc[c, hh]
            m_prev = m_sc[hh, 0:1, :]
            logd = jnp.where(causal, f_c - f_r + i_row, NEG)
            inter = f_c + m_prev
            m_t = jnp.maximum(inter, jnp.max(logd, axis=-1, keepdims=True))
            dm = jnp.where(causal, jnp.exp(logd - m_t), 0.0)
            wi = jnp.exp(inter - m_t)
            qk = (lax.dot_general(q, k, nt, preferred_element_type=F32) * dm).astype(BF16)
            caug = c_sc[hh]
            qc = jnp.dot(q, caug.astype(BF16), preferred_element_type=F32)
            qv = jnp.dot(qk, vaug, preferred_element_type=F32)
            num = wi * qc[:, :L] + qv[:, :L]
            den = wi * qc[:, L:] + qv[:, L:]
            h = num / jnp.maximum(jnp.abs(den), jnp.exp(-m_t))
            mu = jnp.mean(h, axis=-1, keepdims=True)
            hc = h - mu
            var = jnp.mean(hc * hc, axis=-1, keepdims=True)
            hn = hc * lax.rsqrt(var + LN_EPS) * ng_ref[:, hs]
            o = op_ref[0, pl.ds(r0, L), hs].astype(F32)
            y_ref[0, pl.ds(r0, L), hs] = (jax.nn.sigmoid(o) * hn).astype(BF16)

            f_l = f_c[L - 1:L, :]
            logw = f_l - f_r + i_row
            m_new = jnp.maximum(f_l + m_prev, jnp.max(logw, axis=-1, keepdims=True))
            decay = jnp.exp(f_l + m_prev - m_new)
            w_row = jnp.exp(logw - m_new)
            upd = jnp.dot((kt * w_row).astype(BF16), vaug, preferred_element_type=F32)
            c_sc[hh] = jnp.concatenate([decay, decay], axis=1) * caug + upd
            m_sc[hh] = jnp.broadcast_to(m_new, (8, L))
        return carry

    def chunk_pair(j, carry):
        return chunk(2 * j + 1, chunk(2 * j, carry))

    lax.fori_loop(0, nc // 2, chunk_pair, 0)


def _mlstm(xm, op, gates, conv_w, conv_b, wq, wk, wkt, wv, fb, ng):
    B, S, _ = xm.shape
    full = lambda a: pl.BlockSpec(a.shape, lambda b_: (0,) * a.ndim)
    seq = pl.BlockSpec((1, S, ML_WIDTH), lambda b_: (b_, 0, 0))
    return pl.pallas_call(
        functools.partial(_mlstm_kernel, S=S),
        grid=(B,),
        in_specs=[seq, seq, pl.BlockSpec((1,) + gates.shape[1:], lambda b_: (b_, 0, 0, 0)),
                  full(conv_w), full(conv_b), full(wq), full(wk), full(wkt), full(wv), full(fb), full(ng)],
        out_specs=seq,
        out_shape=jax.ShapeDtypeStruct((B, S, ML_WIDTH), BF16),
        scratch_shapes=[pltpu.VMEM((S + 8, ML_WIDTH), F32), pltpu.VMEM((S, ML_WIDTH), BF16),
                        pltpu.VMEM((ML_HEADS, ML_DH, 2 * ML_DH), F32), pltpu.VMEM((ML_HEADS, 8, ML_CHUNK), F32)]
                       + [pltpu.VMEM((S, ML_WIDTH), BF16)] * 3
                       + [pltpu.VMEM((S // ML_CHUNK, ML_HEADS, ML_DH, ML_CHUNK), F32),
                          pltpu.VMEM((S // ML_CHUNK, ML_HEADS, 8, ML_CHUNK), F32),
                          pltpu.VMEM((S // ML_CHUNK, ML_HEADS, ML_CHUNK, ML_CHUNK), F32)],
        compiler_params=_cparams(("arbitrary",)),
        name="mlstm",
    )(xm, op, gates, conv_w, conv_b, wq, wk, wkt, wv, fb, ng)


ROW_TILE = D_MODEL // LANES


def _store_token_tiles(ref, val):
    n = val.shape[0]
    for s in range(ROW_TILE):
        ref[pl.ds(s, n, stride=ROW_TILE), :] = val[:, s * LANES:(s + 1) * LANES]


def _load_token_tiles(ref, n):
    return jnp.concatenate([ref[pl.ds(s, n, stride=ROW_TILE), :] for s in range(ROW_TILE)], axis=1)


def _prep_proj_a(proj_a):
    pa = proj_a.reshape(N_HEADS, HEAD_DIM, D_MODEL)
    return jnp.concatenate([pa, jnp.zeros_like(pa)], axis=1).reshape(QP_W, D_MODEL).astype(BF16)


def _mix_kernel(o_ref, y_ref, gm_ref, x_ref, mod_ref, pa_ref, pb_ref, wo_ref, g2_ref, rwh_ref, rwl_ref, rb_ref,
                x1_ref, h2_ref, e_ref, w_ref, *, ts):
    nt = (((1,), (1,)), ((), ()))
    a = jnp.dot(o_ref[0], pa_ref[...], preferred_element_type=F32)
    bm = jnp.dot(y_ref[0], pb_ref[...], preferred_element_type=F32)
    ga = jax.nn.sigmoid(gm_ref[0, :, :D_MODEL].astype(F32))
    gb = jax.nn.sigmoid(gm_ref[0, :, D_MODEL:].astype(F32))
    mixed = jnp.dot((ga * a + gb * bm).astype(BF16), wo_ref[...], preferred_element_type=F32)
    x1 = x_ref[0] + mod_ref[0, 2:3, :] * mixed
    x1_ref[0] = x1
    ms = jnp.mean(x1 * x1, axis=-1, keepdims=True)
    h2 = x1 * lax.rsqrt(ms + RMS_EPS) * g2_ref[...] * (1.0 + mod_ref[0, 4:5, :]) + mod_ref[0, 3:4, :]
    _store_token_tiles(h2_ref, h2)
    hh, hl = _split_bf16(h2)
    logit = (lax.dot_general(rwh_ref[...], hh, nt, preferred_element_type=F32)
             + lax.dot_general(rwh_ref[...], hl, nt, preferred_element_type=F32)
             + lax.dot_general(rwl_ref[...], hh, nt, preferred_element_type=F32)) + rb_ref[...]
    eio = lax.broadcasted_iota(I32, (N_EXPERTS, ts), 0).astype(F32)
    vals, idxs = [], []
    for _ in range(TOP_K):
        mx = jnp.max(logit, axis=0, keepdims=True)
        ix = jnp.min(jnp.where(logit == mx, eio, 1e9), axis=0, keepdims=True)
        vals.append(mx)
        idxs.append(ix)
        logit = jnp.where(eio == ix, -3e38, logit)
    ex = [jnp.exp(v - vals[0]) for v in vals]
    tot = ex[0] + ex[1] + ex[2] + ex[3]
    e_ref[...] = jnp.concatenate(idxs, axis=0).astype(I32)
    wrows = jnp.concatenate([e_ / tot for e_ in ex] + [jnp.zeros((LANES - TOP_K, ts), F32)], axis=0)
    w_ref[...] = jnp.transpose(wrows)


def _mix(o_nsa, y_ml, gm, x, mod, pa, pb, wo, g2, rwh, rwl, rb, ts=512):
    B, S, _ = x.shape
    T = B * S
    nst = S // ts
    blk = lambda wd: pl.BlockSpec((1, ts, wd), lambda b_, s: (b_, s, 0))
    full = lambda a: pl.BlockSpec(a.shape, lambda b_, s: (0,) * a.ndim)
    return pl.pallas_call(
        functools.partial(_mix_kernel, ts=ts),
        grid=(B, nst),
        in_specs=[blk(QP_W), blk(ML_WIDTH), blk(2 * D_MODEL), blk(D_MODEL),
                  pl.BlockSpec((1, 6, D_MODEL), lambda b_, s: (b_, 0, 0)),
                  full(pa), full(pb), full(wo), full(g2), full(rwh), full(rwl), full(rb)],
        out_specs=[blk(D_MODEL), pl.BlockSpec((ts * ROW_TILE, LANES), lambda b_, s: (b_ * nst + s, 0)),
                   pl.BlockSpec((TOP_K, ts), lambda b_, s: (0, b_ * nst + s)),
                   pl.BlockSpec((ts, LANES), lambda b_, s: (b_ * nst + s, 0))],
        out_shape=[jax.ShapeDtypeStruct((B, S, D_MODEL), F32), jax.ShapeDtypeStruct((T * ROW_TILE, LANES), F32),
                   jax.ShapeDtypeStruct((TOP_K, T), I32), jax.ShapeDtypeStruct((T, LANES), F32)],
        compiler_params=_cparams(("arbitrary", "arbitrary")),
        name="mix_router",
    )(o_nsa, y_ml, gm, x, mod, pa, pb, wo, g2, rwh, rwl, rb)


MOE_TM = 256


def _moe_tiles(T):
    r = T * TOP_K
    return (r + N_EXPERTS * (MOE_TM - 1) + MOE_TM - 1) // MOE_TM


def _route_kernel(e_ref, dest_ref, te_ref, na_ref, *, ntp):
    nrow = e_ref.shape[0]
    eio = lax.broadcasted_iota(I32, (N_EXPERTS, LANES), 0)
    row = lax.broadcasted_iota(I32, (LANES, LANES), 0)
    col = lax.broadcasted_iota(I32, (LANES, LANES), 1)
    triu = jnp.where(row <= col, 1.0, 0.0).astype(BF16)
    ones = jnp.ones((LANES, LANES), BF16)

    ru = 8

    def rank_body(i, counts):
        r0 = pl.multiple_of(i * ru, ru)
        e_blk = e_ref[pl.ds(r0, ru), :]
        ranks = []
        for r in range(ru):
            oh = eio == e_blk[r:r + 1, :]
            ohb = jnp.where(oh, 1.0, 0.0).astype(BF16)
            cum = jnp.dot(ohb, triu, preferred_element_type=F32)
            ranks.append(jnp.sum(jnp.where(oh, cum - 1.0 + counts, 0.0), axis=0, keepdims=True))
            counts = counts + jnp.dot(ohb, ones, preferred_element_type=F32)
        dest_ref[pl.ds(r0, ru), :] = jnp.concatenate(ranks, axis=0).astype(I32)
        return counts

    counts = lax.fori_loop(0, nrow // ru, rank_body, jnp.zeros((N_EXPERTS, LANES), F32)).astype(I32)
    tiles = (counts + (MOE_TM - 1)) // MOE_TM
    incl = tiles
    for s in (1, 2, 4, 8, 16):
        incl = incl + jnp.where(eio >= s, pltpu.roll(incl, s, 0), 0)
    off = (incl - tiles) * MOE_TM

    def dest_body(i, carry):
        r0 = pl.multiple_of(i * ru, ru)
        e_blk = e_ref[pl.ds(r0, ru), :]
        base = [jnp.sum(jnp.where(eio == e_blk[r:r + 1, :], off, 0), axis=0, keepdims=True) for r in range(ru)]
        dest_ref[pl.ds(r0, ru), :] = dest_ref[pl.ds(r0, ru), :] + jnp.concatenate(base, axis=0)
        return carry

    lax.fori_loop(0, nrow // ru, dest_body, 0)
    ends = jnp.concatenate([incl] * (ntp // LANES), axis=1)
    tix = lax.broadcasted_iota(I32, (N_EXPERTS, ntp), 1)
    te = jnp.sum(jnp.where(ends <= tix, 1, 0), axis=0, keepdims=True)
    te_ref[...] = jnp.minimum(te, N_EXPERTS - 1)
    na_ref[...] = jnp.concatenate([incl, counts, off], axis=0)


def _route(e_t):
    T = e_t.shape[1]
    nrow = TOP_K * T // LANES
    ntp = -(-_moe_tiles(T) // LANES) * LANES
    dest, te, meta = pl.pallas_call(
        functools.partial(_route_kernel, ntp=ntp),
        out_shape=[jax.ShapeDtypeStruct((nrow, LANES), I32), jax.ShapeDtypeStruct((1, ntp), I32),
                   jax.ShapeDtypeStruct((3 * N_EXPERTS, LANES), I32)],
        compiler_params=pltpu.CompilerParams(vmem_limit_bytes=VMEM_LIMIT),
        name="route",
    )(e_t.reshape(nrow, LANES))
    return dest.reshape(TOP_K, T), te.reshape(ntp), meta[:, 0]


MOE_TT = 256


def _row_copy(src_ref, s, dst_ref, d, sem):
    src = src_ref.at[pl.ds(pl.multiple_of(s * ROW_TILE, ROW_TILE), ROW_TILE), :]
    dst = dst_ref.at[pl.ds(pl.multiple_of(d * ROW_TILE, ROW_TILE), ROW_TILE), :]
    return pltpu.make_async_copy(src, dst, sem)


MOE_UNROLL = 4


def _dispatch_kernel(meta_ref, dest_ref, h_ref, xs_ref, zero_sc, sem, zsem, *, n_tiles):
    @pl.when(pl.program_id(0) == 0)
    def _():
        zero_sc[...] = jnp.zeros(zero_sc.shape, F32)

        def per_expert(e, carry):
            lo = meta_ref[2 * N_EXPERTS + e] + meta_ref[N_EXPERTS + e]
            hi = meta_ref[e] * MOE_TM

            def fill(p, c):
                _row_copy(zero_sc, 0, xs_ref, p, zsem).start()
                return c

            def drain(p, c):
                _row_copy(zero_sc, 0, xs_ref, p, zsem).wait()
                return c

            lax.fori_loop(lo, hi, fill, 0)
            lax.fori_loop(lo, hi, drain, 0)
            return carry

        lax.fori_loop(0, N_EXPERTS, per_expert, 0)

        def tile_fill(t, carry):
            r0 = pl.multiple_of(t * (MOE_TM * ROW_TILE), MOE_TM * ROW_TILE)
            cp = pltpu.make_async_copy(zero_sc, xs_ref.at[pl.ds(r0, MOE_TM * ROW_TILE), :], zsem)
            cp.start()
            cp.wait()
            return carry

        lax.fori_loop(meta_ref[N_EXPERTS - 1], n_tiles, tile_fill, 0)

    def issue(j, carry):
        for u in range(MOE_UNROLL):
            i = j * MOE_UNROLL + u
            for k in range(TOP_K):
                _row_copy(h_ref, i, xs_ref, dest_ref[k, i], sem).start(priority=k % 2)
        return carry

    lax.fori_loop(0, MOE_TT // MOE_UNROLL, issue, 0)
    for k in range(TOP_K):
        pltpu.make_async_copy(h_ref, xs_ref.at[pl.ds(0, MOE_TT * ROW_TILE), :], sem).wait()


def _dispatch(h2, dest, meta, n_tiles):
    T = h2.shape[0] // ROW_TILE
    grid_spec = pltpu.PrefetchScalarGridSpec(
        num_scalar_prefetch=1,
        grid=(T // MOE_TT,),
        in_specs=[pl.BlockSpec((TOP_K, MOE_TT), lambda i, m: (0, i), memory_space=pltpu.SMEM),
                  pl.BlockSpec((MOE_TT * ROW_TILE, LANES), lambda i, m: (i, 0))],
        out_specs=pl.BlockSpec(memory_space=pl.ANY),
        scratch_shapes=[pltpu.VMEM((MOE_TM * ROW_TILE, LANES), F32), pltpu.SemaphoreType.DMA,
                        pltpu.SemaphoreType.DMA],
    )
    return pl.pallas_call(
        functools.partial(_dispatch_kernel, n_tiles=n_tiles),
        grid_spec=grid_spec,
        out_shape=jax.ShapeDtypeStruct((n_tiles * MOE_TM * ROW_TILE, LANES), F32),
        compiler_params=_cparams(("arbitrary",)),
        name="dispatch",
    )(meta, dest, h2)


def _ffn_kernel(te_ref, na_ref, ends_ref, x_ref, wu_hbm, bu_ref, wd_hbm, bd_ref, y_ref,
                wu_f32, wd_f32, wu_sc, wd_sc, slot_ref, sems):
    i = pl.program_id(0)
    e = te_ref[i]
    n_act = na_ref[0]
    fresh = jnp.logical_and(i < n_act, jnp.logical_or(i == 0, e != te_ref[jnp.maximum(i - 1, 0)]))

    def fetch(expert, slot):
        return (pltpu.make_async_copy(wu_hbm.at[expert], wu_f32.at[slot], sems.at[0, slot]),
                pltpu.make_async_copy(wd_hbm.at[expert], wd_f32.at[slot], sems.at[1, slot]))

    @pl.when(i == 0)
    def _():
        slot_ref[0] = 0
        for cp in fetch(e, 0):
            cp.start()

    @pl.when(fresh)
    def _():
        slot = slot_ref[0]
        nxt = ends_ref[e]

        @pl.when(nxt < n_act)
        def _():
            for cp in fetch(te_ref[nxt], 1 - slot):
                cp.start()

        for cp in fetch(e, slot):
            cp.wait()
        wu_sc[...] = wu_f32[slot].astype(BF16)
        wd_sc[...] = wd_f32[slot].astype(BF16)
        slot_ref[0] = 1 - slot

    @pl.when(i < n_act)
    def _():
        x = _load_token_tiles(x_ref, MOE_TM).astype(BF16)
        gu = jnp.dot(x, wu_sc[...], preferred_element_type=F32) + bu_ref[0]
        gl = jnp.minimum(gu[:, :D_MODEL], SWIGLU_LIMIT)
        lin = jnp.clip(gu[:, D_MODEL:], -SWIGLU_LIMIT, SWIGLU_LIMIT)
        a = gl * jax.nn.sigmoid(SWIGLU_ALPHA * gl) * (lin + 1.0)
        _store_token_tiles(y_ref, jnp.dot(a.astype(BF16), wd_sc[...], preferred_element_type=F32) + bd_ref[0])

    @pl.when(i >= n_act)
    def _():
        y_ref[...] = jnp.zeros(y_ref.shape, F32)


def _ffn(xs, te, na, ends, w_up, b_up, w_down, b_down):
    n_rows = xs.shape[0] // ROW_TILE
    nt = n_rows // MOE_TM
    de = w_up.shape[2]
    row_spec = pl.BlockSpec((MOE_TM * ROW_TILE, LANES), lambda i, te_, na_, en_: (i, 0))
    x_spec = pl.BlockSpec((MOE_TM * ROW_TILE, LANES), lambda i, te_, na_, en_: (jnp.minimum(i, na_[0] - 1), 0))
    grid_spec = pltpu.PrefetchScalarGridSpec(
        num_scalar_prefetch=3,
        grid=(nt,),
        in_specs=[x_spec,
                  pl.BlockSpec(memory_space=pl.ANY),
                  pl.BlockSpec((1, 1, de), lambda i, te_, na_, en_: (te_[i], 0, 0)),
                  pl.BlockSpec(memory_space=pl.ANY),
                  pl.BlockSpec((1, 1, D_MODEL), lambda i, te_, na_, en_: (te_[i], 0, 0))],
        out_specs=row_spec,
        scratch_shapes=[pltpu.VMEM((2, D_MODEL, de), F32), pltpu.VMEM((2, de // 2, D_MODEL), F32),
                        pltpu.VMEM((D_MODEL, de), BF16), pltpu.VMEM((de // 2, D_MODEL), BF16),
                        pltpu.SMEM((1,), I32), pltpu.SemaphoreType.DMA((2, 2))],
    )
    return pl.pallas_call(
        _ffn_kernel,
        grid_spec=grid_spec,
        out_shape=jax.ShapeDtypeStruct(xs.shape, F32),
        compiler_params=_cparams(("arbitrary",)),
        name="expert_ffn",
    )(te, na, ends, xs, w_up, b_up.reshape(N_EXPERTS, 1, de), w_down, b_down.reshape(N_EXPERTS, 1, D_MODEL))


def _combine_kernel(dest_ref, dnext_ref, y_ref, w_ref, x1_ref, mod_ref, fg_ref, o_ref, buf, sems):
    i = pl.program_id(0)
    n = pl.num_programs(0)
    slot = i % 2
    nslot = 1 - slot
    grp = 32

    def gather(d_ref, tok, s):
        for k in range(TOP_K):
            _row_copy(y_ref, d_ref[k, tok], buf.at[s, k], tok, sems.at[s]).start(priority=k % 2)

    def retire(s):
        for k in range(TOP_K):
            pltpu.make_async_copy(y_ref.at[pl.ds(0, MOE_TT * ROW_TILE), :], buf.at[s, k], sems.at[s]).wait()

    @pl.when(i == 0)
    def _():
        def first(tok, carry):
            gather(dest_ref, tok, 0)
            return carry
        lax.fori_loop(0, MOE_TT, first, 0)

    retire(slot)
    g2 = mod_ref[0, 5:6, :]
    fg = fg_ref[...]

    def body(j, carry):
        base = pl.multiple_of(j * grp, grp)
        w8 = w_ref[pl.ds(base, grp), :]
        moe = None
        for k in range(TOP_K):
            rows = jnp.concatenate(
                [buf[slot, k, pl.ds(base * ROW_TILE + s, grp, stride=ROW_TILE), :] for s in range(ROW_TILE)], axis=1)
            term = w8[:, k:k + 1] * rows
            moe = term if moe is None else moe + term
        x2 = x1_ref[pl.ds(base, grp), :] + g2 * moe
        ms = jnp.mean(x2 * x2, axis=-1, keepdims=True)
        o_ref[pl.ds(base, grp), :] = x2 * lax.rsqrt(ms + RMS_EPS) * fg
        for u in range(grp):
            gather(dnext_ref, base + u, nslot)
        return carry

    lax.fori_loop(0, MOE_TT // grp, body, 0)

    @pl.when(i == n - 1)
    def _():
        retire(nslot)


def _combine(y, dest, wcol, x1, mod, final_g, S):
    T = x1.shape[0]
    per_b = S // MOE_TT
    n = T // MOE_TT
    return pl.pallas_call(
        _combine_kernel,
        grid=(n,),
        in_specs=[pl.BlockSpec((TOP_K, MOE_TT), lambda i: (0, i), memory_space=pltpu.SMEM),
                  pl.BlockSpec((TOP_K, MOE_TT), lambda i: (0, jnp.minimum(i + 1, n - 1)), memory_space=pltpu.SMEM),
                  pl.BlockSpec(memory_space=pl.ANY),
                  pl.BlockSpec((MOE_TT, LANES), lambda i: (i, 0)),
                  pl.BlockSpec((MOE_TT, D_MODEL), lambda i: (i, 0)),
                  pl.BlockSpec((1, 6, D_MODEL), lambda i: (i // per_b, 0, 0)),
                  pl.BlockSpec((1, D_MODEL), lambda i: (0, 0))],
        out_specs=pl.BlockSpec((MOE_TT, D_MODEL), lambda i: (i, 0)),
        out_shape=jax.ShapeDtypeStruct((T, D_MODEL), F32),
        scratch_shapes=[pltpu.VMEM((2, TOP_K, MOE_TT * ROW_TILE, LANES), F32), pltpu.SemaphoreType.DMA((2,))],
        compiler_params=_cparams(("arbitrary",)),
        name="combine",
    )(dest, dest, y, wcol, x1, mod, final_g.reshape(1, D_MODEL))


def _layer(x, c, ada_w, ada_b, norm1_g, w_in, b_in, cmp_pe_k, cmp_w1_k, cmp_w2_k, cmp_pe_v, cmp_w1_v, cmp_w2_v,
           ml_conv_w, ml_conv_b, ml_wq, ml_wk, ml_wv, ml_f_bias, ml_norm_g, proj_a, proj_b, w_out, norm2_g,
           router_w, router_b, exp_w_up, exp_b_up, exp_w_down, exp_b_down, final_g):
    B, S, D = x.shape
    T = B * S
    mod = _adaln(c, ada_w, ada_b).reshape(B, 6, D)
    w, b = _prep_w_in(w_in, b_in)
    q, kc_in, vc_in, kv6, xm, op, gm, small = _inproj(x, mod, norm1_g, w, b)
    prep = [_prep_compress(cmp_pe_k, cmp_w1_k, cmp_w2_k), _prep_compress(cmp_pe_v, cmp_w1_v, cmp_w2_v)]
    wk, pe, w2 = [jnp.stack([prep[0][i], prep[1][i]]) for i in range(3)]
    kc, vc = _compress(kc_in, vc_in, wk, pe, w2)
    ovt, ex = _nsa_consts(S)
    o_nsa = _nsa(q, kc, vc, kv6, small, ovt, ex)
    gates = jnp.transpose(small[:, :, 3 * N_HEADS:3 * N_HEADS + 2 * ML_HEADS], (0, 2, 1))
    gates = gates.reshape(B, 2 * ML_HEADS, S // ML_CHUNK, ML_CHUNK)
    y_ml = _mlstm(xm, op, gates, ml_conv_w, ml_conv_b.reshape(1, ML_WIDTH), ml_wq.astype(BF16), ml_wk.astype(BF16),
                  jnp.transpose(ml_wk, (0, 2, 1)).astype(BF16), ml_wv.astype(BF16),
                  jnp.broadcast_to(ml_f_bias[:, None], (ML_HEADS, LANES)), ml_norm_g.reshape(1, ML_WIDTH))
    rwt = jnp.transpose(router_w)
    rwh = rwt.astype(BF16)
    rwl = (rwt - rwh.astype(F32)).astype(BF16)
    x1, h2, e_t, wcol = _mix(o_nsa, y_ml, gm, x, mod, _prep_proj_a(proj_a), proj_b.astype(BF16), w_out.astype(BF16),
                             norm2_g.reshape(1, D), rwh, rwl, router_b.reshape(N_EXPERTS, 1))
    dest, te, meta = _route(e_t)
    ends = meta[:N_EXPERTS]
    xs = _dispatch(h2, dest, meta, _moe_tiles(T))
    y = _ffn(xs, te, ends[N_EXPERTS - 1:], ends, exp_w_up, exp_b_up, exp_w_down, exp_b_down)
    out = _combine(y, dest, wcol, x1.reshape(T, D), mod, final_g, S)
    return out.reshape(B, S, D)


def kernel(x, c, ada_w, ada_b, norm1_g, w_in, b_in, cmp_pe_k, cmp_w1_k, cmp_w2_k, cmp_pe_v, cmp_w1_v, cmp_w2_v, ml_conv_w, ml_conv_b, ml_wq, ml_wk, ml_wv, ml_f_bias, ml_norm_g, proj_a, proj_b, w_out, norm2_g, router_w, router_b, exp_w_up, exp_b_up, exp_w_down, exp_b_down, final_g):
    assert ada_w.shape[0] == 1, "one layer: the final RMSNorm is fused into the layer's last kernel"
    layer = (ada_w, ada_b, norm1_g, w_in, b_in, cmp_pe_k, cmp_w1_k, cmp_w2_k, cmp_pe_v, cmp_w1_v, cmp_w2_v,
             ml_conv_w, ml_conv_b, ml_wq, ml_wk, ml_wv, ml_f_bias, ml_norm_g, proj_a, proj_b, w_out, norm2_g,
             router_w, router_b, exp_w_up, exp_b_up, exp_w_down, exp_b_down)
    return _layer(x, c, *[p[0] for p in layer], final_g)
```

```python
import functools
import math

import numpy as np
import jax
import jax.numpy as jnp
from jax import lax
from jax.experimental import pallas as pl
from jax.experimental.pallas import tpu as pltpu

F32 = jnp.float32
BF16 = jnp.bfloat16
I32 = jnp.int32

D_MODEL = 1024
N_HEADS = 8
N_GROUPS = 2
HPG = N_HEADS // N_GROUPS
HEAD_DIM = 64
CMP_BLOCK = 32
CMP_STRIDE = 16
CMP_HIDDEN = 128
SEL_BLOCK = 64
SEL_TOPN = 8
WINDOW = 256
ATTN_SCALE = HEAD_DIM ** -0.5
ML_HEADS = 4
ML_DH = 128
ML_WIDTH = ML_HEADS * ML_DH
ML_CHUNK = 128
CONV_W = 4
N_EXPERTS = 32
TOP_K = 4
SWIGLU_LIMIT = 7.0
SWIGLU_ALPHA = 1.702
RMS_EPS = 1e-5
LN_EPS = 1e-5

LANES = 128
NEG = -1e30
VMEM_LIMIT = 56 * 1024 * 1024

QP_W = N_HEADS * LANES
C_Q = 0
C_KC = C_Q + QP_W
C_VC = C_KC + LANES
C_KV6 = C_VC + LANES
C_XM = C_KV6 + 6 * LANES
C_OP = C_XM + ML_WIDTH
C_GM = C_OP + ML_WIDTH
C_SM = C_GM + 2 * D_MODEL
IN_W = C_SM + LANES


def _cparams(sem):
    return pltpu.CompilerParams(dimension_semantics=sem, vmem_limit_bytes=VMEM_LIMIT)


def _adaln_kernel(c_ref, w_ref, b_ref, o_ref):
    c = c_ref[...]
    sc = (c * jax.nn.sigmoid(c)).astype(BF16)
    o_ref[...] = jnp.dot(sc, w_ref[...].astype(BF16), preferred_element_type=F32) + b_ref[...]


def _adaln(c, ada_w, ada_b):
    B = c.shape[0]
    tn = 1024
    n = ada_w.shape[1]
    return pl.pallas_call(
        _adaln_kernel,
        grid=(n // tn,),
        in_specs=[pl.BlockSpec((B, D_MODEL), lambda j: (0, 0)),
                  pl.BlockSpec((D_MODEL, tn), lambda j: (0, j)),
                  pl.BlockSpec((1, tn), lambda j: (0, j))],
        out_specs=pl.BlockSpec((B, tn), lambda j: (0, j)),
        out_shape=jax.ShapeDtypeStruct((B, n), F32),
        compiler_params=_cparams(("arbitrary",)),
        name="adaln",
    )(c, ada_w, ada_b.reshape(1, n))


def _prep_w_in(w_in, b_in):
    def pad_q(a):
        lead = a.shape[:-1]
        a = a.reshape(lead + (N_HEADS, HEAD_DIM))
        z = jnp.zeros_like(a[..., :HPG, :])
        lo = jnp.concatenate([a[..., :HPG, :], z], axis=-1)
        hi = jnp.concatenate([z, a[..., HPG:, :]], axis=-1)
        return jnp.concatenate([lo, hi], axis=-2).reshape(lead + (QP_W,))

    def v_aug(a, fill):
        f = jnp.full(a.shape[:-1] + (HEAD_DIM,), fill, a.dtype)
        return jnp.concatenate([a[..., :HEAD_DIM], f, a[..., HEAD_DIM:], f], axis=-1)

    def cols(a):
        sl = lambda lo, hi: lax.slice_in_dim(a, lo, hi, axis=-1)
        return dict(q=sl(0, 512), kc=sl(512, 640), vc=sl(640, 768), ks=sl(768, 896), vs=sl(896, 1024),
                    kw=sl(1024, 1152), vw=sl(1152, 1280), gn=sl(1280, 1304), xm=sl(1304, 1816),
                    op=sl(1816, 2328), fi=sl(2328, 2336), gm=sl(2336, 4384))

    def layout(c, fill):
        zs = jnp.zeros(c["q"].shape[:-1] + (LANES - 32,), c["q"].dtype)
        return jnp.concatenate([pad_q(c["q"]), c["kc"], c["vc"], c["ks"], c["kw"], v_aug(c["vs"], fill),
                                v_aug(c["vw"], fill), c["xm"], c["op"], c["gm"], c["gn"], c["fi"], zs], axis=-1)

    w = layout(cols(w_in), 0.0)
    b = layout(cols(b_in), 1.0)
    return w.astype(BF16), b.reshape(1, IN_W)


def _inproj_kernel(x_ref, mod_ref, g_ref, w_ref, b_ref,
                   q_ref, kc_ref, vc_ref, kv_ref, xm_ref, op_ref, gm_ref, sm_ref):
    x = x_ref[0]
    ms = jnp.mean(x * x, axis=-1, keepdims=True)
    y = x * lax.rsqrt(ms + RMS_EPS) * g_ref[...]
    h = (y * (1.0 + mod_ref[0, 1:2, :]) + mod_ref[0, 0:1, :]).astype(BF16)

    def seg(lo, hi):
        return jnp.dot(h, w_ref[:, lo:hi], preferred_element_type=F32) + b_ref[:, lo:hi]

    q = seg(C_Q, C_KC).astype(BF16)
    for hd in range(N_HEADS):
        q_ref[0, hd] = q[:, hd * LANES:(hd + 1) * LANES]
    kv = seg(C_KC, C_XM).astype(BF16)
    kc_ref[0] = kv[:, 0:LANES]
    vc_ref[0] = kv[:, LANES:2 * LANES]
    kv_ref[0] = kv[:, 2 * LANES:]
    xm_ref[0] = seg(C_XM, C_OP)
    op_ref[0] = seg(C_OP, C_GM).astype(BF16)
    gs = seg(C_GM, IN_W)
    gm_ref[0] = gs[:, :C_SM - C_GM].astype(BF16)
    sm_ref[0] = gs[:, C_SM - C_GM:]


def _inproj(x, mod, norm_g, w, b, ts=512):
    B, S, _ = x.shape
    widths = [(LANES, BF16), (LANES, BF16), (6 * LANES, BF16), (ML_WIDTH, F32),
              (ML_WIDTH, BF16), (2 * D_MODEL, BF16), (LANES, F32)]
    q_spec = pl.BlockSpec((1, N_HEADS, ts, LANES), lambda b_, s: (b_, 0, s, 0))
    return pl.pallas_call(
        _inproj_kernel,
        grid=(B, S // ts),
        in_specs=[pl.BlockSpec((1, ts, D_MODEL), lambda b_, s: (b_, s, 0)),
                  pl.BlockSpec((1, 6, D_MODEL), lambda b_, s: (b_, 0, 0)),
                  pl.BlockSpec((1, D_MODEL), lambda b_, s: (0, 0)),
                  pl.BlockSpec((D_MODEL, IN_W), lambda b_, s: (0, 0)),
                  pl.BlockSpec((1, IN_W), lambda b_, s: (0, 0))],
        out_specs=[q_spec] + [pl.BlockSpec((1, ts, wd), lambda b_, s: (b_, s, 0)) for wd, _ in widths],
        out_shape=([jax.ShapeDtypeStruct((B, N_HEADS, S, LANES), BF16)]
                   + [jax.ShapeDtypeStruct((B, S, wd), dt) for wd, dt in widths]),
        compiler_params=_cparams(("arbitrary", "arbitrary")),
        name="inproj",
    )(x, mod, norm_g.reshape(1, D_MODEL), w, b)


def _prep_compress(pe, w1, w2):
    w1r = w1.reshape(2, CMP_STRIDE, HEAD_DIM, CMP_HIDDEN)
    z = jnp.zeros_like(w1r[0])
    cols = []
    for g in range(N_GROUPS):
        for a in range(2):
            parts = [w1r[a] if gg == g else z for gg in range(N_GROUPS)]
            cols.append(jnp.concatenate(parts, axis=1).reshape(CMP_STRIDE * LANES, CMP_HIDDEN))
    wk = jnp.concatenate(cols, axis=1)
    per = pe.reshape(2, CMP_STRIDE, HEAD_DIM)
    pe_rows = jnp.concatenate([per, per], axis=-1).reshape(2, CMP_STRIDE * LANES)
    pe_rows = jnp.concatenate([pe_rows, jnp.zeros((6, CMP_STRIDE * LANES), pe.dtype)], axis=0)
    z2 = jnp.zeros_like(w2)
    w2p = jnp.stack([jnp.concatenate([w2, z2], axis=1), jnp.concatenate([z2, w2], axis=1)])
    return wk.astype(BF16), pe_rows.astype(BF16), w2p.astype(BF16)


def _compress_kernel(ak_ref, av_ref, wk_ref, pe_ref, w2_ref, ok_ref, ov_ref):
    def hidden(i, a_ref, g):
        y = jnp.dot(a_ref[0], wk_ref[i], preferred_element_type=F32)
        yp = jnp.dot(pe_ref[i], wk_ref[i], preferred_element_type=F32)
        ca, cb = 2 * g * LANES, (2 * g + 1) * LANES
        bias = yp[0:1, ca:ca + LANES] + yp[1:2, cb:cb + LANES]
        hid = y[:, ca:ca + LANES] + pltpu.roll(y[:, cb:cb + LANES], LANES - 1, 0) + bias
        return jax.nn.gelu(hid, approximate=True).astype(BF16)

    ok_ref[0] = (jnp.dot(hidden(0, ak_ref, 0), w2_ref[0, 0], preferred_element_type=F32)
                 + jnp.dot(hidden(0, ak_ref, 1), w2_ref[0, 1], preferred_element_type=F32)).astype(BF16)
    ones_hi = jnp.where(lax.broadcasted_iota(I32, (LANES, LANES), 1) >= HEAD_DIM, 1.0, 0.0)
    for g in range(N_GROUPS):
        v = jnp.dot(hidden(1, av_ref, g), w2_ref[1, 0], preferred_element_type=F32)
        ov_ref[0, g] = (v + ones_hi).astype(BF16)


def _compress(kc_in, vc_in, wk, pe, w2):
    B, S, _ = kc_in.shape
    n = S // CMP_STRIDE
    kw = CMP_STRIDE * LANES
    a_spec = pl.BlockSpec((1, n, kw), lambda b_: (b_, 0, 0))
    full = lambda shp: pl.BlockSpec(shp, lambda b_: (0,) * len(shp))
    return pl.pallas_call(
        _compress_kernel,
        grid=(B,),
        in_specs=[a_spec, a_spec, full(wk.shape), full(pe.shape), full(w2.shape)],
        out_specs=[pl.BlockSpec((1, n, LANES), lambda b_: (b_, 0, 0)),
                   pl.BlockSpec((1, N_GROUPS, n, LANES), lambda b_: (b_, 0, 0, 0))],
        out_shape=[jax.ShapeDtypeStruct((B, n, LANES), BF16), jax.ShapeDtypeStruct((B, N_GROUPS, n, LANES), BF16)],
        compiler_params=_cparams(("arbitrary",)),
        name="compress",
    )(kc_in.reshape(B, n, kw), vc_in.reshape(B, n, kw), wk, pe, w2)


def _nsa_consts(S):
    n_cmp = (S - CMP_BLOCK) // CMP_STRIDE + 1
    n_slc = S // SEL_BLOCK
    cs = np.arange(n_cmp)[:, None] * CMP_STRIDE
    ss = np.arange(n_slc)[None, :] * SEL_BLOCK
    ov = np.clip(np.minimum(cs + CMP_BLOCK, ss + SEL_BLOCK) - np.maximum(cs, ss), 0, None) / CMP_BLOCK
    ovp = np.zeros((LANES, LANES), np.float32)
    ovp[:n_cmp, :n_slc] = ov
    npair = S // (2 * LANES)
    ex = np.zeros((npair, LANES, 2 * LANES), np.float32)
    for kp in range(npair):
        for l in range(2 * LANES):
            ex[kp, (kp * 2 * LANES + l) // SEL_BLOCK, l] = 1.0
    return jnp.asarray(ovp.T, BF16), jnp.asarray(ex, BF16)


M_FLOOR = 0.5 * NEG


def _nsa_kernel(q_ref, kc_ref, vc_ref, ks_ref, kw_ref, vs_ref, vw_ref, gate_ref, ovt_ref, ex_ref,
                o_ref, s_sc, mx_sc, mb_sc, acc_sc, out_sc, *, tq):
    g = pl.program_id(1)
    t0 = pl.program_id(2) * tq
    n_slc = ks_ref.shape[1] // SEL_BLOCK
    tpos = t0 + lax.broadcasted_iota(I32, (tq, LANES), 0)
    lane = lax.broadcasted_iota(I32, (tq, LANES), 1)
    lane_row = lax.broadcasted_iota(I32, (1, LANES), 1)
    slopes = [jnp.where(g == 0, 2.0 ** -(hh + 1), 2.0 ** -(HPG + hh + 1)).astype(F32) for hh in range(HPG)]
    rows = [slice(hh * tq, (hh + 1) * tq) for hh in range(HPG)]
    nt = (((1,), (1,)), ((), ()))
    q4 = q_ref[0].reshape(HPG * tq, LANES) * ATTN_SCALE

    gates = jax.nn.sigmoid(gate_ref[0])
    gates = jnp.where(g == 0, gates, pltpu.roll(gates, LANES - HPG, 1))

    def inv_denom(pv):
        return 1.0 / jnp.maximum(jnp.where(lane < HEAD_DIM, pltpu.roll(pv, HEAD_DIM, 1), pv), 1e-30)

    def gated(br, hh, pv):
        c = br * N_HEADS + hh
        return jnp.broadcast_to(gates[:, c:c + 1], (tq, LANES)) * (pv * inv_denom(pv))

    s_all = lax.dot_general(q4, kc_ref[0], nt, preferred_element_type=F32)
    cmask = jnp.where(lane * CMP_STRIDE + (CMP_BLOCK - 1) <= tpos, 0.0, NEG)
    crel = (lane_row * CMP_STRIDE - t0).astype(F32)
    vca = vc_ref[0, 0]
    psum = jnp.zeros((tq, LANES), F32)
    for hh in range(HPG):
        s = s_all[rows[hh]] + (cmask + slopes[hh] * crel)
        m = jnp.maximum(jnp.max(s, axis=-1, keepdims=True), M_FLOOR)
        p = jnp.exp(s - m)
        pv = jnp.dot(p.astype(BF16), vca, preferred_element_type=F32)
        psum = psum + p * inv_denom(pv)
        out_sc[rows[hh], :] = gated(0, hh, pv)

    ph, plo = _split_bf16(psum)
    imp = (lax.dot_general(ovt_ref[...], ph, nt, preferred_element_type=F32)
           + lax.dot_general(ovt_ref[...], plo, nt, preferred_element_type=F32))[0:n_slc]
    blk = lax.broadcasted_iota(I32, (n_slc, tq), 0)
    cur = (t0 + lax.broadcasted_iota(I32, (n_slc, tq), 1)) // SEL_BLOCK
    blkf = blk.astype(F32)
    sc = jnp.where((blk == cur) | (blk == 0), -NEG, jnp.where(blk > cur, NEG, imp))
    selt = jnp.zeros((n_slc, tq), F32)
    for _ in range(SEL_TOPN):
        mx = jnp.max(sc, axis=0, keepdims=True)
        idx = jnp.min(jnp.where(sc == mx, blkf, 1e9), axis=0, keepdims=True)
        hit = blkf == idx
        selt = jnp.where(hit, 1.0, selt)
        sc = jnp.where(hit, -3e38, sc)
    selt = jnp.where(blk <= cur, selt, 0.0)
    sel = jnp.transpose(jnp.concatenate([selt, jnp.zeros((LANES - n_slc, tq), F32)], axis=0)).astype(BF16)

    KT = NSA_KCHUNK

    def scores(slots, k2, kpos, mask2):
        s2 = lax.dot_general(q4, k2, nt, preferred_element_type=F32)
        rel2 = jnp.concatenate([(kp + lane_row - t0).astype(F32) for kp in kpos], axis=1)
        for hh in range(HPG):
            s = s2[rows[hh]] + (mask2 + slopes[hh] * rel2)
            for j in range(2):
                s_sc[slots[j], rows[hh], :] = s[:, j * LANES:(j + 1) * LANES]
            mx_sc[rows[hh], :] = jnp.maximum(mx_sc[rows[hh], :], jnp.maximum(s[:, :LANES], s[:, LANES:]))

    def finish_max():
        for hh in range(HPG):
            m = jnp.maximum(jnp.max(mx_sc[rows[hh], :], axis=-1, keepdims=True), M_FLOOR)
            mb_sc[rows[hh], :] = jnp.broadcast_to(m, (tq, LANES))

    def weighted(c, v_c):
        for hh in range(HPG):
            s4 = jnp.concatenate([s_sc[KT * c + j, rows[hh], :] for j in range(KT)], axis=1)
            mb = mb_sc[rows[hh], :]
            p = jnp.exp(s4 - jnp.concatenate([mb] * KT, axis=1)).astype(BF16)
            acc_sc[rows[hh], :] += jnp.dot(p, v_c, preferred_element_type=F32)

    mx_sc[...] = jnp.full(mx_sc.shape, NEG, F32)
    acc_sc[...] = jnp.zeros(acc_sc.shape, F32)
    n_chunks = (t0 + tq + KT * LANES - 1) // (KT * LANES)

    tpos2 = t0 + lax.broadcasted_iota(I32, (tq, 2 * LANES), 0)
    lane2 = lax.broadcasted_iota(I32, (tq, 2 * LANES), 1)

    def slc_scores(kp, carry):
        r0 = pl.multiple_of(kp * 2 * LANES, 2 * LANES)
        picked = jnp.dot(sel, ex_ref[kp], preferred_element_type=F32)
        mask2 = jnp.where((picked > 0.5) & (r0 + lane2 <= tpos2), 0.0, NEG)
        scores((2 * kp, 2 * kp + 1), ks_ref[0, pl.ds(r0, 2 * LANES), :], (r0, r0 + LANES), mask2)
        return carry

    lax.fori_loop(0, (KT // 2) * n_chunks, slc_scores, 0)
    finish_max()

    def slc_weighted(c, carry):
        r0 = pl.multiple_of(c * KT * LANES, KT * LANES)
        weighted(c, vs_ref[0, pl.ds(r0, KT * LANES), :])
        return carry

    lax.fori_loop(0, n_chunks, slc_weighted, 0)
    for hh in range(HPG):
        pv = acc_sc[rows[hh], :]
        out_sc[rows[hh], :] += gated(1, hh, pv)

    mx_sc[...] = jnp.full(mx_sc.shape, NEG, F32)
    acc_sc[...] = jnp.zeros(acc_sc.shape, F32)
    v_tiles = []
    for half in range(KT // 2):
        k_parts, kpos, masks = [], [], []
        for r in (2 * half, 2 * half + 1):
            kt = t0 // LANES - WINDOW // LANES + r
            r0 = pl.multiple_of(jnp.maximum(kt, 0) * LANES, LANES)
            spos = kt * LANES + lane
            dist = tpos - spos
            masks.append(jnp.where((dist >= 0) & (dist < WINDOW) & (spos >= 0), 0.0, NEG))
            k_parts.append(kw_ref[0, pl.ds(r0, LANES), :])
            kpos.append(kt * LANES)
            v_tiles.append(vw_ref[0, pl.ds(r0, LANES), :])
        scores((2 * half, 2 * half + 1), jnp.concatenate(k_parts, axis=0), kpos, jnp.concatenate(masks, axis=1))
    finish_max()
    weighted(0, jnp.concatenate(v_tiles, axis=0))
    for hh in range(HPG):
        pv = acc_sc[rows[hh], :]
        o = out_sc[rows[hh], :] + gated(2, hh, pv)
        o_ref[0, :, hh * LANES:(hh + 1) * LANES] = o.astype(BF16)


NSA_KCHUNK = 4
NSA_TQ = (NSA_KCHUNK - WINDOW // LANES) * LANES


def _nsa(q, kc, vc, kv6, small, consts):
    ovt, ex = consts
    B, _, S, _ = q.shape
    tq = NSA_TQ
    gw = HPG * LANES
    seq = lambda f: pl.BlockSpec((1, S, LANES), f)
    return pl.pallas_call(
        functools.partial(_nsa_kernel, tq=tq),
        grid=(B, N_GROUPS, S // tq),
        in_specs=[pl.BlockSpec((1, HPG, tq, LANES), lambda b_, g, i: (b_, g, i, 0)),
                  pl.BlockSpec((1, LANES, LANES), lambda b_, g, i: (b_, 0, 0)),
                  pl.BlockSpec((1, 1, LANES, LANES), lambda b_, g, i: (b_, g, 0, 0)),
                  seq(lambda b_, g, i: (b_, 0, 0)), seq(lambda b_, g, i: (b_, 0, 1)),
                  seq(lambda b_, g, i: (b_, 0, 2 + g)), seq(lambda b_, g, i: (b_, 0, 4 + g)),
                  pl.BlockSpec((1, tq, LANES), lambda b_, g, i: (b_, i, 0)),
                  pl.BlockSpec(ovt.shape, lambda b_, g, i: (0, 0)),
                  pl.BlockSpec(ex.shape, lambda b_, g, i: (0, 0, 0))],
        out_specs=pl.BlockSpec((1, tq, gw), lambda b_, g, i: (b_, i, g)),
        out_shape=jax.ShapeDtypeStruct((B, S, QP_W), BF16),
        scratch_shapes=[pltpu.VMEM((S // LANES, HPG * tq, LANES), F32)] + [pltpu.VMEM((HPG * tq, LANES), F32)] * 4,
        compiler_params=_cparams(("arbitrary", "arbitrary", "arbitrary")),
        name="nsa",
    )(q, kc, vc, kv6, kv6, kv6, kv6, small, ovt, ex)


def _split_bf16(a):
    hi = a.astype(BF16)
    return hi, (a - hi.astype(F32)).astype(BF16)


def _mlstm_kernel(xm_ref, op_ref, gate_ref, cw_ref, cb_ref, wq_ref, wk_ref, wkt_ref, wv_ref, fb_ref, ng_ref,
                  y_ref, xpad_sc, xc_sc, c_sc, m_sc, q_sc, k_sc, v_sc, kt_sc, fr_sc, fc_sc, *, S):
    L = ML_CHUNK
    nt = (((1,), (1,)), ((), ()))
    xpad_sc[0:8, :] = jnp.zeros((8, ML_WIDTH), F32)
    xpad_sc[8:, :] = xm_ref[0]
    rb = 256
    for r in range(S // rb):
        acc = jnp.zeros((rb, ML_WIDTH), F32) + cb_ref[...]
        for j in range(CONV_W):
            o = 8 + r * rb - (CONV_W - 1) + j
            acc = acc + cw_ref[j:j + 1, :] * xpad_sc[o:o + rb, :]
        xc_sc[r * rb:(r + 1) * rb, :] = (acc * jax.nn.sigmoid(acc)).astype(BF16)

    c_sc[...] = jnp.zeros(c_sc.shape, F32)
    m_sc[...] = jnp.zeros(m_sc.shape, F32)
    row = lax.broadcasted_iota(I32, (L, L), 0)
    col = lax.broadcasted_iota(I32, (L, L), 1)
    causal = col <= row
    triu = jnp.where(row <= col, 1.0, 0.0).astype(BF16)
    ones = jnp.ones((L, L), BF16)
    rs = 1.0 / math.sqrt(ML_DH)
    nc = S // L

    for hh in range(ML_HEADS):
        hs = slice(hh * ML_DH, (hh + 1) * ML_DH)
        xc = xc_sc[:, hs]
        q_sc[:, hs] = jnp.dot(xc, wq_ref[hh], preferred_element_type=F32).astype(BF16)
        k_sc[:, hs] = (jnp.dot(xc, wk_ref[hh], preferred_element_type=F32) * rs).astype(BF16)
        v_sc[:, hs] = jnp.dot(xm_ref[0, :, hs].astype(BF16), wv_ref[hh], preferred_element_type=F32).astype(BF16)
        kt = lax.dot_general(wkt_ref[hh], xc, nt, preferred_element_type=F32) * rs
        for c in range(nc):
            kt_sc[c, hh] = kt[:, c * L:(c + 1) * L]

    def gate_sums(c, carry):
        for hh in range(ML_HEADS):
            z = gate_ref[0, ML_HEADS + hh, pl.ds(c, 1), :] + fb_ref[hh:hh + 1, :]
            lf_row = jnp.minimum(z, 0.0) - jnp.log(1.0 + jnp.exp(-jnp.abs(z)))
            rh, rl = _split_bf16(jnp.broadcast_to(lf_row, (8, L)))
            fr_sc[c, hh] = (jnp.dot(rh, triu, preferred_element_type=F32)
                            + jnp.dot(rl, triu, preferred_element_type=F32))
            lh, ll = _split_bf16(jnp.where(causal, jnp.broadcast_to(lf_row, (L, L)), 0.0))
            fc_sc[c, hh] = (jnp.dot(lh, ones, preferred_element_type=F32)
                            + jnp.dot(ll, ones, preferred_element_type=F32))
        return carry

    lax.fori_loop(0, nc, gate_sums, 0)

    def chunk(c, carry):
        r0 = pl.multiple_of(c * L, L)
        for hh in range(ML_HEADS):
            hs = slice(hh * ML_DH, (hh + 1) * ML_DH)
            q = q_sc[pl.ds(r0, L), hs]
            k = k_sc[pl.ds(r0, L), hs]
            kt = kt_sc[c, hh]
            vaug = jnp.concatenate([v_sc[pl.ds(r0, L), hs], ones], axis=1)
            i_row = gate_ref[0, hh, pl.ds(c, 1), :]
            f_r = fr_sc[c, hh, 0:1, :]
            f_c = fc_sc[c, hh]
            m_prev = m_sc[hh, 0:1, :]
            logd = jnp.where(causal, f_c - f_r + i_row, NEG)
            inter = f_c + m_prev
            m_t = jnp.maximum(inter, jnp.max(logd, axis=-1, keepdims=True))
            dm = jnp.where(causal, jnp.exp(logd - m_t), 0.0)
            wi = jnp.exp(inter - m_t)
            qk = (lax.dot_general(q, k, nt, preferred_element_type=F32) * dm).astype(BF16)
            caug = c_sc[hh]
            qc = jnp.dot(q, caug.astype(BF16), preferred_element_type=F32)
            qv = jnp.dot(qk, vaug, preferred_element_type=F32)
            num = wi * qc[:, :L] + qv[:, :L]
            den = wi * qc[:, L:] + qv[:, L:]
            h = num / jnp.maximum(jnp.abs(den), jnp.exp(-m_t))
            mu = jnp.mean(h, axis=-1, keepdims=True)
            hc = h - mu
            var = jnp.mean(hc * hc, axis=-1, keepdims=True)
            hn = hc * lax.rsqrt(var + LN_EPS) * ng_ref[:, hs]
            o = op_ref[0, pl.ds(r0, L), hs].astype(F32)
            y_ref[0, pl.ds(r0, L), hs] = (jax.nn.sigmoid(o) * hn).astype(BF16)

            f_l = f_c[L - 1:L, :]
            logw = f_l - f_r + i_row
            m_new = jnp.maximum(f_l + m_prev, jnp.max(logw, axis=-1, keepdims=True))
            decay = jnp.exp(f_l + m_prev - m_new)
            w_row = jnp.exp(logw - m_new)
            upd = jnp.dot((kt * w_row).astype(BF16), vaug, preferred_element_type=F32)
            c_sc[hh] = jnp.concatenate([decay, decay], axis=1) * caug + upd
            m_sc[hh] = jnp.broadcast_to(m_new, (8, L))
        return carry

    def chunk_pair(j, carry):
        return chunk(2 * j + 1, chunk(2 * j, carry))

    lax.fori_loop(0, nc // 2, chunk_pair, 0)


def _mlstm(xm, op, gates, conv_w, conv_b, wq, wk, wkt, wv, fb, ng):
    B, S, _ = xm.shape
    full = lambda a: pl.BlockSpec(a.shape, lambda b_: (0,) * a.ndim)
    seq = pl.BlockSpec((1, S, ML_WIDTH), lambda b_: (b_, 0, 0))
    return pl.pallas_call(
        functools.partial(_mlstm_kernel, S=S),
        grid=(B,),
        in_specs=[seq, seq, pl.BlockSpec((1,) + gates.shape[1:], lambda b_: (b_, 0, 0, 0)),
                  full(conv_w), full(conv_b), full(wq), full(wk), full(wkt), full(wv), full(fb), full(ng)],
        out_specs=seq,
        out_shape=jax.ShapeDtypeStruct((B, S, ML_WIDTH), BF16),
        scratch_shapes=[pltpu.VMEM((S + 8, ML_WIDTH), F32), pltpu.VMEM((S, ML_WIDTH), BF16),
                        pltpu.VMEM((ML_HEADS, ML_DH, 2 * ML_DH), F32), pltpu.VMEM((ML_HEADS, 8, ML_CHUNK), F32)]
                       + [pltpu.VMEM((S, ML_WIDTH), BF16)] * 3
                       + [pltpu.VMEM((S // ML_CHUNK, ML_HEADS, ML_DH, ML_CHUNK), F32),
                          pltpu.VMEM((S // ML_CHUNK, ML_HEADS, 8, ML_CHUNK), F32),
                          pltpu.VMEM((S // ML_CHUNK, ML_HEADS, ML_CHUNK, ML_CHUNK), F32)],
        compiler_params=_cparams(("arbitrary",)),
        name="mlstm",
    )(xm, op, gates, conv_w, conv_b, wq, wk, wkt, wv, fb, ng)


ROW_TILE = D_MODEL // LANES


def _store_token_tiles(ref, val):
    n = val.shape[0]
    for s in range(ROW_TILE):
        ref[pl.ds(s, n, stride=ROW_TILE), :] = val[:, s * LANES:(s + 1) * LANES]


def _load_token_tiles(ref, n):
    return jnp.concatenate([ref[pl.ds(s, n, stride=ROW_TILE), :] for s in range(ROW_TILE)], axis=1)


def _prep_proj_a(proj_a):
    pa = proj_a.reshape(N_HEADS, HEAD_DIM, D_MODEL)
    return jnp.concatenate([pa, jnp.zeros_like(pa)], axis=1).reshape(QP_W, D_MODEL).astype(BF16)


def _mix_kernel(o_ref, y_ref, gm_ref, x_ref, mod_ref, pa_ref, pb_ref, wo_ref, g2_ref, rwh_ref, rwl_ref, rb_ref,
                x1_ref, h2_ref, e_ref, w_ref, *, ts):
    nt = (((1,), (1,)), ((), ()))
    a = jnp.dot(o_ref[0], pa_ref[...], preferred_element_type=F32)
    bm = jnp.dot(y_ref[0], pb_ref[...], preferred_element_type=F32)
    ga = jax.nn.sigmoid(gm_ref[0, :, :D_MODEL].astype(F32))
    gb = jax.nn.sigmoid(gm_ref[0, :, D_MODEL:].astype(F32))
    mixed = jnp.dot((ga * a + gb * bm).astype(BF16), wo_ref[...], preferred_element_type=F32)
    x1 = x_ref[0] + mod_ref[0, 2:3, :] * mixed
    x1_ref[0] = x1
    ms = jnp.mean(x1 * x1, axis=-1, keepdims=True)
    h2 = x1 * lax.rsqrt(ms + RMS_EPS) * g2_ref[...] * (1.0 + mod_ref[0, 4:5, :]) + mod_ref[0, 3:4, :]
    _store_token_tiles(h2_ref, h2)
    hh, hl = _split_bf16(h2)
    logit = (lax.dot_general(rwh_ref[...], hh, nt, preferred_element_type=F32)
             + lax.dot_general(rwh_ref[...], hl, nt, preferred_element_type=F32)
             + lax.dot_general(rwl_ref[...], hh, nt, preferred_element_type=F32)) + rb_ref[...]
    eio = lax.broadcasted_iota(I32, (N_EXPERTS, ts), 0).astype(F32)
    vals, idxs = [], []
    for _ in range(TOP_K):
        mx = jnp.max(logit, axis=0, keepdims=True)
        ix = jnp.min(jnp.where(logit == mx, eio, 1e9), axis=0, keepdims=True)
        vals.append(mx)
        idxs.append(ix)
        logit = jnp.where(eio == ix, -3e38, logit)
    ex = [jnp.exp(v - vals[0]) for v in vals]
    tot = ex[0] + ex[1] + ex[2] + ex[3]
    e_ref[...] = jnp.concatenate(idxs, axis=0).astype(I32)
    wrows = jnp.concatenate([e_ / tot for e_ in ex] + [jnp.zeros((LANES - TOP_K, ts), F32)], axis=0)
    w_ref[...] = jnp.transpose(wrows)


def _mix(o_nsa, y_ml, gm, x, mod, pa, pb, wo, g2, rwh, rwl, rb, ts=512):
    B, S, _ = x.shape
    T = B * S
    nst = S // ts
    blk = lambda wd: pl.BlockSpec((1, ts, wd), lambda b_, s: (b_, s, 0))
    full = lambda a: pl.BlockSpec(a.shape, lambda b_, s: (0,) * a.ndim)
    return pl.pallas_call(
        functools.partial(_mix_kernel, ts=ts),
        grid=(B, nst),
        in_specs=[blk(QP_W), blk(ML_WIDTH), blk(2 * D_MODEL), blk(D_MODEL),
                  pl.BlockSpec((1, 6, D_MODEL), lambda b_, s: (b_, 0, 0)),
                  full(pa), full(pb), full(wo), full(g2), full(rwh), full(rwl), full(rb)],
        out_specs=[blk(D_MODEL), pl.BlockSpec((ts * ROW_TILE, LANES), lambda b_, s: (b_ * nst + s, 0)),
                   pl.BlockSpec((TOP_K, ts), lambda b_, s: (0, b_ * nst + s)),
                   pl.BlockSpec((ts, LANES), lambda b_, s: (b_ * nst + s, 0))],
        out_shape=[jax.ShapeDtypeStruct((B, S, D_MODEL), F32), jax.ShapeDtypeStruct((T * ROW_TILE, LANES), F32),
                   jax.ShapeDtypeStruct((TOP_K, T), I32), jax.ShapeDtypeStruct((T, LANES), F32)],
        compiler_params=_cparams(("arbitrary", "arbitrary")),
        name="mix_router",
    )(o_nsa, y_ml, gm, x, mod, pa, pb, wo, g2, rwh, rwl, rb)


MOE_TM = 256


def _moe_tiles(T):
    r = T * TOP_K
    return (r + N_EXPERTS * (MOE_TM - 1) + MOE_TM - 1) // MOE_TM


def _route_kernel(e_ref, dest_ref, te_ref, na_ref, *, ntp):
    nrow = e_ref.shape[0]
    eio = lax.broadcasted_iota(I32, (N_EXPERTS, LANES), 0)
    row = lax.broadcasted_iota(I32, (LANES, LANES), 0)
    col = lax.broadcasted_iota(I32, (LANES, LANES), 1)
    triu = jnp.where(row <= col, 1.0, 0.0).astype(BF16)
    ones = jnp.ones((LANES, LANES), BF16)

    ru = 8

    def rank_body(i, counts):
        r0 = pl.multiple_of(i * ru, ru)
        e_blk = e_ref[pl.ds(r0, ru), :]
        ranks = []
        for r in range(ru):
            oh = eio == e_blk[r:r + 1, :]
            ohb = jnp.where(oh, 1.0, 0.0).astype(BF16)
            cum = jnp.dot(ohb, triu, preferred_element_type=F32)
            ranks.append(jnp.sum(jnp.where(oh, cum - 1.0 + counts, 0.0), axis=0, keepdims=True))
            counts = counts + jnp.dot(ohb, ones, preferred_element_type=F32)
        dest_ref[pl.ds(r0, ru), :] = jnp.concatenate(ranks, axis=0).astype(I32)
        return counts

    counts = lax.fori_loop(0, nrow // ru, rank_body, jnp.zeros((N_EXPERTS, LANES), F32)).astype(I32)
    tiles = (counts + (MOE_TM - 1)) // MOE_TM
    incl = tiles
    for s in (1, 2, 4, 8, 16):
        incl = incl + jnp.where(eio >= s, pltpu.roll(incl, s, 0), 0)
    off = (incl - tiles) * MOE_TM

    def dest_body(i, carry):
        r0 = pl.multiple_of(i * ru, ru)
        e_blk = e_ref[pl.ds(r0, ru), :]
        base = [jnp.sum(jnp.where(eio == e_blk[r:r + 1, :], off, 0), axis=0, keepdims=True) for r in range(ru)]
        dest_ref[pl.ds(r0, ru), :] = (dest_ref[pl.ds(r0, ru), :] + jnp.concatenate(base, axis=0)) * ROW_TILE
        return carry

    lax.fori_loop(0, nrow // ru, dest_body, 0)
    ends = jnp.concatenate([incl] * (ntp // LANES), axis=1)
    tix = lax.broadcasted_iota(I32, (N_EXPERTS, ntp), 1)
    te = jnp.sum(jnp.where(ends <= tix, 1, 0), axis=0, keepdims=True)
    te_ref[...] = jnp.minimum(te, N_EXPERTS - 1)
    na_ref[...] = jnp.concatenate([incl, counts, off], axis=0)


def _route(e_t):
    T = e_t.shape[1]
    nrow = TOP_K * T // LANES
    ntp = -(-_moe_tiles(T) // LANES) * LANES
    dest, te, meta = pl.pallas_call(
        functools.partial(_route_kernel, ntp=ntp),
        out_shape=[jax.ShapeDtypeStruct((nrow, LANES), I32), jax.ShapeDtypeStruct((1, ntp), I32),
                   jax.ShapeDtypeStruct((3 * N_EXPERTS, LANES), I32)],
        compiler_params=pltpu.CompilerParams(vmem_limit_bytes=VMEM_LIMIT),
        name="route",
    )(e_t.reshape(nrow, LANES))
    return dest.reshape(TOP_K, T), te.reshape(ntp), meta[:, 0]


MOE_TT = 256


def _row_copy(src_ref, s8, dst_ref, d8, sem):
    src = src_ref.at[pl.ds(pl.multiple_of(s8, ROW_TILE), ROW_TILE), :]
    dst = dst_ref.at[pl.ds(pl.multiple_of(d8, ROW_TILE), ROW_TILE), :]
    return pltpu.make_async_copy(src, dst, sem)


MOE_UNROLL = 4


def _dispatch_kernel(meta_ref, dest_ref, h_ref, xs_ref, zero_sc, sem, zsem, *, n_tiles):
    @pl.when(pl.program_id(0) == 0)
    def _():
        zero_sc[...] = jnp.zeros(zero_sc.shape, F32)

        def pad_range(e):
            lo = meta_ref[2 * N_EXPERTS + e] + meta_ref[N_EXPERTS + e]
            hi = meta_ref[e] * MOE_TM
            return lo, hi

        def fill_expert(e, carry):
            def fill(p, c):
                _row_copy(zero_sc, 0, xs_ref, p * ROW_TILE, zsem).start()
                return c
            return lax.fori_loop(*pad_range(e), fill, carry)

        def drain_expert(e, carry):
            def drain(p, c):
                _row_copy(zero_sc, 0, xs_ref, p * ROW_TILE, zsem).wait()
                return c
            return lax.fori_loop(*pad_range(e), drain, carry)

        lax.fori_loop(0, N_EXPERTS, fill_expert, 0)
        lax.fori_loop(0, N_EXPERTS, drain_expert, 0)

        def tile_fill(t, carry):
            r0 = pl.multiple_of(t * (MOE_TM * ROW_TILE), MOE_TM * ROW_TILE)
            cp = pltpu.make_async_copy(zero_sc, xs_ref.at[pl.ds(r0, MOE_TM * ROW_TILE), :], zsem)
            cp.start()
            cp.wait()
            return carry

        lax.fori_loop(meta_ref[N_EXPERTS - 1], n_tiles, tile_fill, 0)

    def issue(j, carry):
        i0 = j * MOE_UNROLL
        r0 = i0 * ROW_TILE
        for u in range(MOE_UNROLL):
            for k in range(TOP_K):
                _row_copy(h_ref, r0 + u * ROW_TILE, xs_ref, dest_ref[k, i0 + u], sem).start(priority=k % 2)
        return carry

    lax.fori_loop(0, MOE_TT // MOE_UNROLL, issue, 0)
    for k in range(TOP_K):
        pltpu.make_async_copy(h_ref, xs_ref.at[pl.ds(0, MOE_TT * ROW_TILE), :], sem).wait()


def _dispatch(h2, dest, meta, n_tiles):
    T = h2.shape[0] // ROW_TILE
    grid_spec = pltpu.PrefetchScalarGridSpec(
        num_scalar_prefetch=1,
        grid=(T // MOE_TT,),
        in_specs=[pl.BlockSpec((TOP_K, MOE_TT), lambda i, m: (0, i), memory_space=pltpu.SMEM),
                  pl.BlockSpec((MOE_TT * ROW_TILE, LANES), lambda i, m: (i, 0))],
        out_specs=pl.BlockSpec(memory_space=pl.ANY),
        scratch_shapes=[pltpu.VMEM((MOE_TM * ROW_TILE, LANES), F32), pltpu.SemaphoreType.DMA,
                        pltpu.SemaphoreType.DMA],
    )
    return pl.pallas_call(
        functools.partial(_dispatch_kernel, n_tiles=n_tiles),
        grid_spec=grid_spec,
        out_shape=jax.ShapeDtypeStruct((n_tiles * MOE_TM * ROW_TILE, LANES), F32),
        compiler_params=_cparams(("arbitrary",)),
        name="dispatch",
    )(meta, dest, h2)


def _ffn_kernel(te_ref, na_ref, ends_ref, x_ref, wu_hbm, bu_ref, wd_hbm, bd_ref, y_ref,
                wu_f32, wd_f32, wu_sc, wd_sc, slot_ref, sems):
    n_act = na_ref[0]
    rows = MOE_TM * ROW_TILE

    def fetch(expert, slot):
        return (pltpu.make_async_copy(wu_hbm.at[expert], wu_f32.at[slot], sems.at[0, slot]),
                pltpu.make_async_copy(wd_hbm.at[expert], wd_f32.at[slot], sems.at[1, slot]))

    for part in range(FFN_TILES):
        t = pl.program_id(0) * FFN_TILES + part
        e = te_ref[t]
        fresh = jnp.logical_and(t < n_act, jnp.logical_or(t == 0, e != te_ref[jnp.maximum(t - 1, 0)]))
        x_part = x_ref.at[part * rows:(part + 1) * rows, :]
        y_part = y_ref.at[part * rows:(part + 1) * rows, :]

        @pl.when(t == 0)
        def _():
            slot_ref[0] = 0
            for cp in fetch(e, 0):
                cp.start()

        @pl.when(fresh)
        def _():
            slot = slot_ref[0]
            nxt = ends_ref[e]

            @pl.when(nxt < n_act)
            def _():
                for cp in fetch(te_ref[nxt], 1 - slot):
                    cp.start()

            for cp in fetch(e, slot):
                cp.wait()
            wu_sc[...] = wu_f32[slot].astype(BF16)
            wd_sc[...] = wd_f32[slot].astype(BF16)
            slot_ref[0] = 1 - slot

        @pl.when(t < n_act)
        def _():
            x = _load_token_tiles(x_part, MOE_TM).astype(BF16)
            gu = jnp.dot(x, wu_sc[...], preferred_element_type=F32) + bu_ref[e]
            gl = jnp.minimum(gu[:, :D_MODEL], SWIGLU_LIMIT)
            lin = jnp.clip(gu[:, D_MODEL:], -SWIGLU_LIMIT, SWIGLU_LIMIT)
            a = gl * jax.nn.sigmoid(SWIGLU_ALPHA * gl) * (lin + 1.0)
            _store_token_tiles(y_part, jnp.dot(a.astype(BF16), wd_sc[...], preferred_element_type=F32) + bd_ref[e])

        @pl.when(t >= n_act)
        def _():
            y_part[...] = jnp.zeros(y_part.shape, F32)


FFN_TILES = 2


def _ffn(xs, te, na, ends, w_up, b_up, w_down, b_down):
    n_rows = xs.shape[0] // ROW_TILE
    nt = n_rows // MOE_TM
    assert nt % FFN_TILES == 0
    de = w_up.shape[2]
    blk = FFN_TILES * MOE_TM * ROW_TILE
    row_spec = pl.BlockSpec((blk, LANES), lambda i, te_, na_, en_: (i, 0))
    x_spec = pl.BlockSpec((blk, LANES), lambda i, te_, na_, en_: (jnp.minimum(i, (na_[0] - 1) // FFN_TILES), 0))
    grid_spec = pltpu.PrefetchScalarGridSpec(
        num_scalar_prefetch=3,
        grid=(nt // FFN_TILES,),
        in_specs=[x_spec,
                  pl.BlockSpec(memory_space=pl.ANY),
                  pl.BlockSpec((N_EXPERTS, 1, de), lambda i, te_, na_, en_: (0, 0, 0)),
                  pl.BlockSpec(memory_space=pl.ANY),
                  pl.BlockSpec((N_EXPERTS, 1, D_MODEL), lambda i, te_, na_, en_: (0, 0, 0))],
        out_specs=row_spec,
        scratch_shapes=[pltpu.VMEM((2, D_MODEL, de), F32), pltpu.VMEM((2, de // 2, D_MODEL), F32),
                        pltpu.VMEM((D_MODEL, de), BF16), pltpu.VMEM((de // 2, D_MODEL), BF16),
                        pltpu.SMEM((1,), I32), pltpu.SemaphoreType.DMA((2, 2))],
    )
    return pl.pallas_call(
        _ffn_kernel,
        grid_spec=grid_spec,
        out_shape=jax.ShapeDtypeStruct(xs.shape, F32),
        compiler_params=_cparams(("arbitrary",)),
        name="expert_ffn",
    )(te, na, ends, xs, w_up, b_up.reshape(N_EXPERTS, 1, de), w_down, b_down.reshape(N_EXPERTS, 1, D_MODEL))


def _combine_kernel(dest_ref, dnext_ref, y_ref, w_ref, x1_ref, mod_ref, fg_ref, o_ref, buf, sems):
    i = pl.program_id(0)
    n = pl.num_programs(0)
    slot = i % 2
    nslot = 1 - slot
    grp = 32

    def gather(d_ref, tok, tok8, s):
        for k in range(TOP_K):
            _row_copy(y_ref, d_ref[k, tok], buf.at[s, k], tok8, sems.at[s]).start(priority=k % 2)

    def retire(s):
        for k in range(TOP_K):
            pltpu.make_async_copy(y_ref.at[pl.ds(0, MOE_TT * ROW_TILE), :], buf.at[s, k], sems.at[s]).wait()

    @pl.when(i == 0)
    def _():
        def first(tok, carry):
            gather(dest_ref, tok, tok * ROW_TILE, 0)
            return carry
        lax.fori_loop(0, MOE_TT, first, 0)

    retire(slot)
    g2 = mod_ref[0, 5:6, :]
    fg = fg_ref[...]

    def body(j, carry):
        base = pl.multiple_of(j * grp, grp)
        w8 = w_ref[pl.ds(base, grp), :]
        moe = None
        for k in range(TOP_K):
            rows = jnp.concatenate(
                [buf[slot, k, pl.ds(base * ROW_TILE + s, grp, stride=ROW_TILE), :] for s in range(ROW_TILE)], axis=1)
            term = w8[:, k:k + 1] * rows
            moe = term if moe is None else moe + term
        x2 = x1_ref[pl.ds(base, grp), :] + g2 * moe
        ms = jnp.mean(x2 * x2, axis=-1, keepdims=True)
        o_ref[pl.ds(base, grp), :] = x2 * lax.rsqrt(ms + RMS_EPS) * fg
        base8 = base * ROW_TILE
        for u in range(grp):
            gather(dnext_ref, base + u, base8 + u * ROW_TILE, nslot)
        return carry

    lax.fori_loop(0, MOE_TT // grp, body, 0)

    @pl.when(i == n - 1)
    def _():
        retire(nslot)


def _combine(y, dest, wcol, x1, mod, final_g, S):
    T = x1.shape[0]
    per_b = S // MOE_TT
    n = T // MOE_TT
    return pl.pallas_call(
        _combine_kernel,
        grid=(n,),
        in_specs=[pl.BlockSpec((TOP_K, MOE_TT), lambda i: (0, i), memory_space=pltpu.SMEM),
                  pl.BlockSpec((TOP_K, MOE_TT), lambda i: (0, jnp.minimum(i + 1, n - 1)), memory_space=pltpu.SMEM),
                  pl.BlockSpec(memory_space=pl.ANY),
                  pl.BlockSpec((MOE_TT, LANES), lambda i: (i, 0)),
                  pl.BlockSpec((MOE_TT, D_MODEL), lambda i: (i, 0)),
                  pl.BlockSpec((1, 6, D_MODEL), lambda i: (i // per_b, 0, 0)),
                  pl.BlockSpec((1, D_MODEL), lambda i: (0, 0))],
        out_specs=pl.BlockSpec((MOE_TT, D_MODEL), lambda i: (i, 0)),
        out_shape=jax.ShapeDtypeStruct((T, D_MODEL), F32),
        scratch_shapes=[pltpu.VMEM((2, TOP_K, MOE_TT * ROW_TILE, LANES), F32), pltpu.SemaphoreType.DMA((2,))],
        compiler_params=_cparams(("arbitrary",)),
        name="combine",
    )(dest, dest, y, wcol, x1, mod, final_g.reshape(1, D_MODEL))


def _layer(x, c, ada_w, ada_b, norm1_g, w_in, b_in, cmp_pe_k, cmp_w1_k, cmp_w2_k, cmp_pe_v, cmp_w1_v, cmp_w2_v,
           ml_conv_w, ml_conv_b, ml_wq, ml_wk, ml_wv, ml_f_bias, ml_norm_g, proj_a, proj_b, w_out, norm2_g,
           router_w, router_b, exp_w_up, exp_b_up, exp_w_down, exp_b_down, final_g):
    B, S, D = x.shape
    T = B * S
    mod = _adaln(c, ada_w, ada_b).reshape(B, 6, D)
    w, b = _prep_w_in(w_in, b_in)
    q, kc_in, vc_in, kv6, xm, op, gm, small = _inproj(x, mod, norm1_g, w, b)
    prep = [_prep_compress(cmp_pe_k, cmp_w1_k, cmp_w2_k), _prep_compress(cmp_pe_v, cmp_w1_v, cmp_w2_v)]
    wk, pe, w2 = [jnp.stack([prep[0][i], prep[1][i]]) for i in range(3)]
    kc, vc = _compress(kc_in, vc_in, wk, pe, w2)
    o_nsa = _nsa(q, kc, vc, kv6, small, _nsa_consts(S))
    gates = jnp.transpose(small[:, :, 3 * N_HEADS:3 * N_HEADS + 2 * ML_HEADS], (0, 2, 1))
    gates = gates.reshape(B, 2 * ML_HEADS, S // ML_CHUNK, ML_CHUNK)
    y_ml = _mlstm(xm, op, gates, ml_conv_w, ml_conv_b.reshape(1, ML_WIDTH), ml_wq.astype(BF16), ml_wk.astype(BF16),
                  jnp.transpose(ml_wk, (0, 2, 1)).astype(BF16), ml_wv.astype(BF16),
                  jnp.broadcast_to(ml_f_bias[:, None], (ML_HEADS, LANES)), ml_norm_g.reshape(1, ML_WIDTH))
    rwt = jnp.transpose(router_w)
    rwh = rwt.astype(BF16)
    rwl = (rwt - rwh.astype(F32)).astype(BF16)
    x1, h2, e_t, wcol = _mix(o_nsa, y_ml, gm, x, mod, _prep_proj_a(proj_a), proj_b.astype(BF16), w_out.astype(BF16),
                             norm2_g.reshape(1, D), rwh, rwl, router_b.reshape(N_EXPERTS, 1))
    dest, te, meta = _route(e_t)
    ends = meta[:N_EXPERTS]
    xs = _dispatch(h2, dest, meta, _moe_tiles(T))
    y = _ffn(xs, te, ends[N_EXPERTS - 1:], ends, exp_w_up, exp_b_up, exp_w_down, exp_b_down)
    out = _combine(y, dest, wcol, x1.reshape(T, D), mod, final_g, S)
    return out.reshape(B, S, D)


def kernel(x, c, ada_w, ada_b, norm1_g, w_in, b_in, cmp_pe_k, cmp_w1_k, cmp_w2_k, cmp_pe_v, cmp_w1_v, cmp_w2_v, ml_conv_w, ml_conv_b, ml_wq, ml_wk, ml_wv, ml_f_bias, ml_norm_g, proj_a, proj_b, w_out, norm2_g, router_w, router_b, exp_w_up, exp_b_up, exp_w_down, exp_b_down, final_g):
    assert ada_w.shape[0] == 1, "one layer: the final RMSNorm is fused into the layer's last kernel"
    layer = (ada_w, ada_b, norm1_g, w_in, b_in, cmp_pe_k, cmp_w1_k, cmp_w2_k, cmp_pe_v, cmp_w1_v, cmp_w2_v,
             ml_conv_w, ml_conv_b, ml_wq, ml_wk, ml_wv, ml_f_bias, ml_norm_g, proj_a, proj_b, w_out, norm2_g,
             router_w, router_b, exp_w_up, exp_b_up, exp_w_down, exp_b_down)
    return _layer(x, c, *[p[0] for p in layer], final_g)
```

```python
import functools
import math

import numpy as np
import jax
import jax.numpy as jnp
from jax import lax
from jax.experimental import pallas as pl
from jax.experimental.pallas import tpu as pltpu

F32 = jnp.float32
BF16 = jnp.bfloat16
I32 = jnp.int32

D_MODEL = 1024
N_HEADS = 8
N_GROUPS = 2
HPG = N_HEADS // N_GROUPS
HEAD_DIM = 64
CMP_BLOCK = 32
CMP_STRIDE = 16
CMP_HIDDEN = 128
SEL_BLOCK = 64
SEL_TOPN = 8
WINDOW = 256
ATTN_SCALE = HEAD_DIM ** -0.5
ML_HEADS = 4
ML_DH = 128
ML_WIDTH = ML_HEADS * ML_DH
ML_CHUNK = 128
CONV_W = 4
N_EXPERTS = 32
TOP_K = 4
SWIGLU_LIMIT = 7.0
SWIGLU_ALPHA = 1.702
RMS_EPS = 1e-5
LN_EPS = 1e-5

LANES = 128
NEG = -1e30
VMEM_LIMIT = 56 * 1024 * 1024

QP_W = N_HEADS * LANES
C_Q = 0
C_KC = C_Q + QP_W
C_VC = C_KC + LANES
C_KV6 = C_VC + LANES
C_XM = C_KV6 + 6 * LANES
C_OP = C_XM + ML_WIDTH
C_GM = C_OP + ML_WIDTH
C_SM = C_GM + 2 * D_MODEL
IN_W = C_SM + LANES


def _cparams(sem):
    return pltpu.CompilerParams(dimension_semantics=sem, vmem_limit_bytes=VMEM_LIMIT)


def _adaln_kernel(c_ref, w_ref, b_ref, o_ref):
    c = c_ref[...]
    sc = (c * jax.nn.sigmoid(c)).astype(BF16)
    o_ref[...] = jnp.dot(sc, w_ref[...].astype(BF16), preferred_element_type=F32) + b_ref[...]


def _adaln(c, ada_w, ada_b):
    B = c.shape[0]
    tn = 1024
    n = ada_w.shape[1]
    return pl.pallas_call(
        _adaln_kernel,
        grid=(n // tn,),
        in_specs=[pl.BlockSpec((B, D_MODEL), lambda j: (0, 0)),
                  pl.BlockSpec((D_MODEL, tn), lambda j: (0, j)),
                  pl.BlockSpec((1, tn), lambda j: (0, j))],
        out_specs=pl.BlockSpec((B, tn), lambda j: (0, j)),
        out_shape=jax.ShapeDtypeStruct((B, n), F32),
        compiler_params=_cparams(("arbitrary",)),
        name="adaln",
    )(c, ada_w, ada_b.reshape(1, n))


def _prep_w_in(w_in, b_in):
    def pad_q(a):
        lead = a.shape[:-1]
        a = a.reshape(lead + (N_HEADS, HEAD_DIM))
        z = jnp.zeros_like(a[..., :HPG, :])
        lo = jnp.concatenate([a[..., :HPG, :], z], axis=-1)
        hi = jnp.concatenate([z, a[..., HPG:, :]], axis=-1)
        return jnp.concatenate([lo, hi], axis=-2).reshape(lead + (QP_W,))

    def v_aug(a, fill):
        f = jnp.full(a.shape[:-1] + (HEAD_DIM,), fill, a.dtype)
        return jnp.concatenate([a[..., :HEAD_DIM], f, a[..., HEAD_DIM:], f], axis=-1)

    def cols(a):
        sl = lambda lo, hi: lax.slice_in_dim(a, lo, hi, axis=-1)
        return dict(q=sl(0, 512), kc=sl(512, 640), vc=sl(640, 768), ks=sl(768, 896), vs=sl(896, 1024),
                    kw=sl(1024, 1152), vw=sl(1152, 1280), gn=sl(1280, 1304), xm=sl(1304, 1816),
                    op=sl(1816, 2328), fi=sl(2328, 2336), gm=sl(2336, 4384))

    def layout(c, fill):
        zs = jnp.zeros(c["q"].shape[:-1] + (LANES - 32,), c["q"].dtype)
        return jnp.concatenate([pad_q(c["q"]), c["kc"], c["vc"], c["ks"], c["kw"], v_aug(c["vs"], fill),
                                v_aug(c["vw"], fill), c["xm"], c["op"], c["gm"], c["gn"], c["fi"], zs], axis=-1)

    w = layout(cols(w_in), 0.0)
    b = layout(cols(b_in), 1.0)
    return w.astype(BF16), b.reshape(1, IN_W)


def _inproj_kernel(x_ref, mod_ref, g_ref, w_ref, b_ref,
                   q_ref, kc_ref, vc_ref, kv_ref, xm_ref, op_ref, gm_ref, sm_ref, gt_ref):
    x = x_ref[0]
    ms = jnp.mean(x * x, axis=-1, keepdims=True)
    y = x * lax.rsqrt(ms + RMS_EPS) * g_ref[...]
    h = (y * (1.0 + mod_ref[0, 1:2, :]) + mod_ref[0, 0:1, :]).astype(BF16)

    def seg(lo, hi):
        return jnp.dot(h, w_ref[:, lo:hi], preferred_element_type=F32) + b_ref[:, lo:hi]

    q = seg(C_Q, C_KC).astype(BF16)
    for hd in range(N_HEADS):
        q_ref[0, hd] = q[:, hd * LANES:(hd + 1) * LANES]
    kv = seg(C_KC, C_XM)
    kc_ref[0] = kv[:, 0:LANES]
    vc_ref[0] = kv[:, LANES:2 * LANES]
    kv_ref[0] = kv[:, 2 * LANES:].astype(BF16)
    xm_ref[0] = seg(C_XM, C_OP)
    op_ref[0] = seg(C_OP, C_GM).astype(BF16)
    gs = seg(C_GM, IN_W)
    gm_ref[0] = gs[:, :C_SM - C_GM].astype(BF16)
    sm = gs[:, C_SM - C_GM:]
    sm_ref[0] = sm
    gt_ref[0] = jnp.transpose(sm)[3 * N_HEADS:3 * N_HEADS + 2 * ML_HEADS, :]


def _inproj(x, mod, norm_g, w, b, ts=512):
    B, S, _ = x.shape
    widths = [(LANES, F32), (LANES, F32), (6 * LANES, BF16), (ML_WIDTH, F32),
              (ML_WIDTH, BF16), (2 * D_MODEL, BF16), (LANES, F32)]
    q_spec = pl.BlockSpec((1, N_HEADS, ts, LANES), lambda b_, s: (b_, 0, s, 0))
    gt_spec = pl.BlockSpec((1, 2 * ML_HEADS, ts), lambda b_, s: (b_, 0, s))
    return pl.pallas_call(
        _inproj_kernel,
        grid=(B, S // ts),
        in_specs=[pl.BlockSpec((1, ts, D_MODEL), lambda b_, s: (b_, s, 0)),
                  pl.BlockSpec((1, 6, D_MODEL), lambda b_, s: (b_, 0, 0)),
                  pl.BlockSpec((1, D_MODEL), lambda b_, s: (0, 0)),
                  pl.BlockSpec((D_MODEL, IN_W), lambda b_, s: (0, 0)),
                  pl.BlockSpec((1, IN_W), lambda b_, s: (0, 0))],
        out_specs=([q_spec] + [pl.BlockSpec((1, ts, wd), lambda b_, s: (b_, s, 0)) for wd, _ in widths]
                   + [gt_spec]),
        out_shape=([jax.ShapeDtypeStruct((B, N_HEADS, S, LANES), BF16)]
                   + [jax.ShapeDtypeStruct((B, S, wd), dt) for wd, dt in widths]
                   + [jax.ShapeDtypeStruct((B, 2 * ML_HEADS, S), F32)]),
        compiler_params=_cparams(("arbitrary", "arbitrary")),
        name="inproj",
    )(x, mod, norm_g.reshape(1, D_MODEL), w, b)


def _prep_compress(pe, w1, w2):
    w1r = w1.reshape(2, CMP_STRIDE, HEAD_DIM, CMP_HIDDEN)
    z = jnp.zeros_like(w1r[0])
    cols = []
    for g in range(N_GROUPS):
        for a in range(2):
            parts = [w1r[a] if gg == g else z for gg in range(N_GROUPS)]
            cols.append(jnp.concatenate(parts, axis=1).reshape(CMP_STRIDE * LANES, CMP_HIDDEN))
    wk = jnp.concatenate(cols, axis=1)
    per = pe.reshape(2, CMP_STRIDE, HEAD_DIM)
    pe_rows = jnp.concatenate([per, per], axis=-1).reshape(2, CMP_STRIDE * LANES)
    pe_rows = jnp.concatenate([pe_rows, jnp.zeros((6, CMP_STRIDE * LANES), pe.dtype)], axis=0)
    z2 = jnp.zeros_like(w2)
    w2p = jnp.stack([jnp.concatenate([w2, z2], axis=1), jnp.concatenate([z2, w2], axis=1)])
    return wk.astype(BF16), pe_rows.astype(BF16), w2p.astype(BF16)


def _compress_kernel(ak_ref, av_ref, wk_ref, pe_ref, w2_ref, ok_ref, ov_ref):
    n = ok_ref.shape[1]

    def hidden(i, a_ref, g):
        a = jnp.concatenate([a_ref.at[0][pl.ds(r, n, stride=CMP_STRIDE), :] for r in range(CMP_STRIDE)], axis=1)
        y = jnp.dot(a.astype(BF16), wk_ref[i], preferred_element_type=F32)
        yp = jnp.dot(pe_ref[i], wk_ref[i], preferred_element_type=F32)
        ca, cb = 2 * g * LANES, (2 * g + 1) * LANES
        bias = yp[0:1, ca:ca + LANES] + yp[1:2, cb:cb + LANES]
        hid = y[:, ca:ca + LANES] + pltpu.roll(y[:, cb:cb + LANES], LANES - 1, 0) + bias
        return jax.nn.gelu(hid, approximate=True).astype(BF16)

    ok_ref[0] = (jnp.dot(hidden(0, ak_ref, 0), w2_ref[0, 0], preferred_element_type=F32)
                 + jnp.dot(hidden(0, ak_ref, 1), w2_ref[0, 1], preferred_element_type=F32)).astype(BF16)
    ones_hi = jnp.where(lax.broadcasted_iota(I32, (LANES, LANES), 1) >= HEAD_DIM, 1.0, 0.0)
    for g in range(N_GROUPS):
        v = jnp.dot(hidden(1, av_ref, g), w2_ref[1, 0], preferred_element_type=F32)
        ov_ref[0, g] = (v + ones_hi).astype(BF16)


def _compress(kc_in, vc_in, wk, pe, w2):
    B, S, _ = kc_in.shape
    n = S // CMP_STRIDE
    a_spec = pl.BlockSpec((1, S, LANES), lambda b_: (b_, 0, 0))
    full = lambda shp: pl.BlockSpec(shp, lambda b_: (0,) * len(shp))
    return pl.pallas_call(
        _compress_kernel,
        grid=(B,),
        in_specs=[a_spec, a_spec, full(wk.shape), full(pe.shape), full(w2.shape)],
        out_specs=[pl.BlockSpec((1, n, LANES), lambda b_: (b_, 0, 0)),
                   pl.BlockSpec((1, N_GROUPS, n, LANES), lambda b_: (b_, 0, 0, 0))],
        out_shape=[jax.ShapeDtypeStruct((B, n, LANES), BF16), jax.ShapeDtypeStruct((B, N_GROUPS, n, LANES), BF16)],
        compiler_params=_cparams(("arbitrary",)),
        name="compress",
    )(kc_in, vc_in, wk, pe, w2)


def _nsa_consts(S):
    n_cmp = (S - CMP_BLOCK) // CMP_STRIDE + 1
    n_slc = S // SEL_BLOCK
    cs = np.arange(n_cmp)[:, None] * CMP_STRIDE
    ss = np.arange(n_slc)[None, :] * SEL_BLOCK
    ov = np.clip(np.minimum(cs + CMP_BLOCK, ss + SEL_BLOCK) - np.maximum(cs, ss), 0, None) / CMP_BLOCK
    ovp = np.zeros((LANES, LANES), np.float32)
    ovp[:n_cmp, :n_slc] = ov
    npair = S // (2 * LANES)
    ex = np.zeros((npair, LANES, 2 * LANES), np.float32)
    for kp in range(npair):
        for l in range(2 * LANES):
            ex[kp, (kp * 2 * LANES + l) // SEL_BLOCK, l] = 1.0
    return jnp.asarray(ovp.T, BF16), jnp.asarray(ex, BF16)


M_FLOOR = 0.5 * NEG


def _nsa_kernel(q_ref, kc_ref, vc_ref, ks_ref, kw_ref, vs_ref, vw_ref, gate_ref, ovt_ref, ex_ref,
                o_ref, s_sc, mx_sc, mb_sc, acc_sc, out_sc, *, tq):
    g = pl.program_id(1)
    t0 = pl.program_id(2) * tq
    n_slc = ks_ref.shape[1] // SEL_BLOCK
    tpos = t0 + lax.broadcasted_iota(I32, (tq, LANES), 0)
    lane = lax.broadcasted_iota(I32, (tq, LANES), 1)
    lane_row = lax.broadcasted_iota(I32, (1, LANES), 1)
    slopes = [jnp.where(g == 0, 2.0 ** -(hh + 1), 2.0 ** -(HPG + hh + 1)).astype(F32) for hh in range(HPG)]
    rows = [slice(hh * tq, (hh + 1) * tq) for hh in range(HPG)]
    nt = (((1,), (1,)), ((), ()))
    q4 = q_ref[0].reshape(HPG * tq, LANES) * ATTN_SCALE

    gates = jax.nn.sigmoid(gate_ref[0])
    gates = jnp.where(g == 0, gates, pltpu.roll(gates, LANES - HPG, 1))

    def inv_denom(pv):
        return 1.0 / jnp.maximum(jnp.where(lane < HEAD_DIM, pltpu.roll(pv, HEAD_DIM, 1), pv), 1e-30)

    def gated(br, hh, pv):
        c = br * N_HEADS + hh
        return jnp.broadcast_to(gates[:, c:c + 1], (tq, LANES)) * (pv * inv_denom(pv))

    s_all = lax.dot_general(q4, kc_ref[0], nt, preferred_element_type=F32)
    cmask = jnp.where(lane * CMP_STRIDE + (CMP_BLOCK - 1) <= tpos, 0.0, NEG)
    crel = (lane_row * CMP_STRIDE - t0).astype(F32)
    vca = vc_ref[0, 0]
    psum = jnp.zeros((tq, LANES), F32)
    for hh in range(HPG):
        s = s_all[rows[hh]] + (cmask + slopes[hh] * crel)
        m = jnp.maximum(jnp.max(s, axis=-1, keepdims=True), M_FLOOR)
        p = jnp.exp(s - m)
        pv = jnp.dot(p.astype(BF16), vca, preferred_element_type=F32)
        psum = psum + p * inv_denom(pv)
        out_sc[rows[hh], :] = gated(0, hh, pv)

    ph, plo = _split_bf16(psum)
    imp = (lax.dot_general(ovt_ref[...], ph, nt, preferred_element_type=F32)
           + lax.dot_general(ovt_ref[...], plo, nt, preferred_element_type=F32))[0:n_slc]
    blk = lax.broadcasted_iota(I32, (n_slc, tq), 0)
    cur = (t0 + lax.broadcasted_iota(I32, (n_slc, tq), 1)) // SEL_BLOCK
    blkf = blk.astype(F32)
    sc = jnp.where((blk == cur) | (blk == 0), -NEG, jnp.where(blk > cur, NEG, imp))
    selt = jnp.zeros((n_slc, tq), F32)
    for _ in range(SEL_TOPN):
        mx = jnp.max(sc, axis=0, keepdims=True)
        idx = jnp.min(jnp.where(sc == mx, blkf, 1e9), axis=0, keepdims=True)
        hit = blkf == idx
        selt = jnp.where(hit, 1.0, selt)
        sc = jnp.where(hit, -3e38, sc)
    selt = jnp.where(blk <= cur, selt, 0.0)
    sel = jnp.transpose(jnp.concatenate([selt, jnp.zeros((LANES - n_slc, tq), F32)], axis=0)).astype(BF16)

    KT = NSA_KCHUNK

    def scores(slots, k2, kpos, mask2):
        s2 = lax.dot_general(q4, k2, nt, preferred_element_type=F32)
        rel2 = jnp.concatenate([(kp + lane_row - t0).astype(F32) for kp in kpos], axis=1)
        for hh in range(HPG):
            s = s2[rows[hh]] + (mask2 + slopes[hh] * rel2)
            for j in range(2):
                s_sc[slots[j], rows[hh], :] = s[:, j * LANES:(j + 1) * LANES]
            mx_sc[rows[hh], :] = jnp.maximum(mx_sc[rows[hh], :], jnp.maximum(s[:, :LANES], s[:, LANES:]))

    def finish_max():
        for hh in range(HPG):
            m = jnp.maximum(jnp.max(mx_sc[rows[hh], :], axis=-1, keepdims=True), M_FLOOR)
            mb_sc[rows[hh], :] = jnp.broadcast_to(m, (tq, LANES))

    def weighted(c, v_c):
        for hh in range(HPG):
            s4 = jnp.concatenate([s_sc[KT * c + j, rows[hh], :] for j in range(KT)], axis=1)
            mb = mb_sc[rows[hh], :]
            p = jnp.exp(s4 - jnp.concatenate([mb] * KT, axis=1)).astype(BF16)
            acc_sc[rows[hh], :] += jnp.dot(p, v_c, preferred_element_type=F32)

    mx_sc[...] = jnp.full(mx_sc.shape, NEG, F32)
    acc_sc[...] = jnp.zeros(acc_sc.shape, F32)
    n_chunks = (t0 + tq + KT * LANES - 1) // (KT * LANES)

    tpos2 = t0 + lax.broadcasted_iota(I32, (tq, 2 * LANES), 0)
    lane2 = lax.broadcasted_iota(I32, (tq, 2 * LANES), 1)

    def slc_scores(kp, carry):
        r0 = pl.multiple_of(kp * 2 * LANES, 2 * LANES)
        picked = jnp.dot(sel, ex_ref[kp], preferred_element_type=F32)
        mask2 = jnp.where((picked > 0.5) & (r0 + lane2 <= tpos2), 0.0, NEG)
        scores((2 * kp, 2 * kp + 1), ks_ref[0, pl.ds(r0, 2 * LANES), :], (r0, r0 + LANES), mask2)
        return carry

    lax.fori_loop(0, (KT // 2) * n_chunks, slc_scores, 0)
    finish_max()

    def slc_weighted(c, carry):
        r0 = pl.multiple_of(c * KT * LANES, KT * LANES)
        weighted(c, vs_ref[0, pl.ds(r0, KT * LANES), :])
        return carry

    lax.fori_loop(0, n_chunks, slc_weighted, 0)
    for hh in range(HPG):
        pv = acc_sc[rows[hh], :]
        out_sc[rows[hh], :] += gated(1, hh, pv)

    mx_sc[...] = jnp.full(mx_sc.shape, NEG, F32)
    acc_sc[...] = jnp.zeros(acc_sc.shape, F32)
    v_tiles = []
    for half in range(KT // 2):
        k_parts, kpos, masks = [], [], []
        for r in (2 * half, 2 * half + 1):
            kt = t0 // LANES - WINDOW // LANES + r
            r0 = pl.multiple_of(jnp.maximum(kt, 0) * LANES, LANES)
            spos = kt * LANES + lane
            dist = tpos - spos
            masks.append(jnp.where((dist >= 0) & (dist < WINDOW) & (spos >= 0), 0.0, NEG))
            k_parts.append(kw_ref[0, pl.ds(r0, LANES), :])
            kpos.append(kt * LANES)
            v_tiles.append(vw_ref[0, pl.ds(r0, LANES), :])
        scores((2 * half, 2 * half + 1), jnp.concatenate(k_parts, axis=0), kpos, jnp.concatenate(masks, axis=1))
    finish_max()
    weighted(0, jnp.concatenate(v_tiles, axis=0))
    for hh in range(HPG):
        pv = acc_sc[rows[hh], :]
        o = out_sc[rows[hh], :] + gated(2, hh, pv)
        o_ref[0, :, hh * LANES:(hh + 1) * LANES] = o.astype(BF16)


NSA_KCHUNK = 4
NSA_TQ = (NSA_KCHUNK - WINDOW // LANES) * LANES


def _nsa(q, kc, vc, kv6, small, consts):
    ovt, ex = consts
    B, _, S, _ = q.shape
    tq = NSA_TQ
    gw = HPG * LANES
    seq = lambda f: pl.BlockSpec((1, S, LANES), f)
    return pl.pallas_call(
        functools.partial(_nsa_kernel, tq=tq),
        grid=(B, N_GROUPS, S // tq),
        in_specs=[pl.BlockSpec((1, HPG, tq, LANES), lambda b_, g, i: (b_, g, i, 0)),
                  pl.BlockSpec((1, LANES, LANES), lambda b_, g, i: (b_, 0, 0)),
                  pl.BlockSpec((1, 1, LANES, LANES), lambda b_, g, i: (b_, g, 0, 0)),
                  seq(lambda b_, g, i: (b_, 0, 0)), seq(lambda b_, g, i: (b_, 0, 1)),
                  seq(lambda b_, g, i: (b_, 0, 2 + g)), seq(lambda b_, g, i: (b_, 0, 4 + g)),
                  pl.BlockSpec((1, tq, LANES), lambda b_, g, i: (b_, i, 0)),
                  pl.BlockSpec(ovt.shape, lambda b_, g, i: (0, 0)),
                  pl.BlockSpec(ex.shape, lambda b_, g, i: (0, 0, 0))],
        out_specs=pl.BlockSpec((1, tq, gw), lambda b_, g, i: (b_, i, g)),
        out_shape=jax.ShapeDtypeStruct((B, S, QP_W), BF16),
        scratch_shapes=[pltpu.VMEM((S // LANES, HPG * tq, LANES), F32)] + [pltpu.VMEM((HPG * tq, LANES), F32)] * 4,
        compiler_params=_cparams(("arbitrary", "arbitrary", "arbitrary")),
        name="nsa",
    )(q, kc, vc, kv6, kv6, kv6, kv6, small, ovt, ex)


def _split_bf16(a):
    hi = a.astype(BF16)
    return hi, (a - hi.astype(F32)).astype(BF16)


def _mlstm_kernel(xm_ref, op_ref, gate_ref, cw_ref, cb_ref, wq_ref, wk_ref, wkt_ref, wv_ref, fb_ref, ng_ref,
                  y_ref, xpad_sc, xc_sc, c_sc, m_sc, q_sc, k_sc, v_sc, kt_sc, fr_sc, fc_sc, *, S):
    L = ML_CHUNK
    nt = (((1,), (1,)), ((), ()))
    xpad_sc[0:8, :] = jnp.zeros((8, ML_WIDTH), F32)
    xpad_sc[8:, :] = xm_ref[0]
    rb = 256
    for r in range(S // rb):
        acc = jnp.zeros((rb, ML_WIDTH), F32) + cb_ref[...]
        for j in range(CONV_W):
            o = 8 + r * rb - (CONV_W - 1) + j
            acc = acc + cw_ref[j:j + 1, :] * xpad_sc[o:o + rb, :]
        xc_sc[r * rb:(r + 1) * rb, :] = (acc * jax.nn.sigmoid(acc)).astype(BF16)

    c_sc[...] = jnp.zeros(c_sc.shape, F32)
    m_sc[...] = jnp.zeros(m_sc.shape, F32)
    row = lax.broadcasted_iota(I32, (L, L), 0)
    col = lax.broadcasted_iota(I32, (L, L), 1)
    causal = col <= row
    triu = jnp.where(row <= col, 1.0, 0.0).astype(BF16)
    ones = jnp.ones((L, L), BF16)
    rs = 1.0 / math.sqrt(ML_DH)
    nc = S // L

    for hh in range(ML_HEADS):
        hs = slice(hh * ML_DH, (hh + 1) * ML_DH)
        xc = xc_sc[:, hs]
        q_sc[:, hs] = jnp.dot(xc, wq_ref[hh], preferred_element_type=F32).astype(BF16)
        k_sc[:, hs] = (jnp.dot(xc, wk_ref[hh], preferred_element_type=F32) * rs).astype(BF16)
        v_sc[:, hs] = jnp.dot(xm_ref[0, :, hs].astype(BF16), wv_ref[hh], preferred_element_type=F32).astype(BF16)
        kt = lax.dot_general(wkt_ref[hh], xc, nt, preferred_element_type=F32) * rs
        for c in range(nc):
            kt_sc[c, hh] = kt[:, c * L:(c + 1) * L]

    def gate_sums(c, carry):
        for hh in range(ML_HEADS):
            z = gate_ref[0, ML_HEADS + hh, pl.ds(c, 1), :] + fb_ref[hh:hh + 1, :]
            lf_row = jnp.minimum(z, 0.0) - jnp.log(1.0 + jnp.exp(-jnp.abs(z)))
            rh, rl = _split_bf16(jnp.broadcast_to(lf_row, (8, L)))
            fr_sc[c, hh] = (jnp.dot(rh, triu, preferred_element_type=F32)
                            + jnp.dot(rl, triu, preferred_element_type=F32))
            lh, ll = _split_bf16(jnp.where(causal, jnp.broadcast_to(lf_row, (L, L)), 0.0))
            fc_sc[c, hh] = (jnp.dot(lh, ones, preferred_element_type=F32)
                            + jnp.dot(ll, ones, preferred_element_type=F32))
        return carry

    lax.fori_loop(0, nc, gate_sums, 0)

    def chunk(c, carry):
        r0 = pl.multiple_of(c * L, L)
        for hh in range(ML_HEADS):
            hs = slice(hh * ML_DH, (hh + 1) * ML_DH)
            q = q_sc[pl.ds(r0, L), hs]
            k = k_sc[pl.ds(r0, L), hs]
            kt = kt_sc[c, hh]
            vaug = jnp.concatenate([v_sc[pl.ds(r0, L), hs], ones], axis=1)
            i_row = gate_ref[0, hh, pl.ds(c, 1), :]
            f_r = fr_sc[c, hh, 0:1, :]
            f_c = fc_sc[c, hh]
            m_prev = m_sc[hh, 0:1, :]
            logd = jnp.where(causal, f_c - f_r + i_row, NEG)
            inter = f_c + m_prev
            m_t = jnp.maximum(inter, jnp.max(logd, axis=-1, keepdims=True))
            dm = jnp.where(causal, jnp.exp(logd - m_t), 0.0)
            wi = jnp.exp(inter - m_t)
            qk = (lax.dot_general(q, k, nt, preferred_element_type=F32) * dm).astype(BF16)
            caug = c_sc[hh]
            qc = jnp.dot(q, caug.astype(BF16), preferred_element_type=F32)
            qv = jnp.dot(qk, vaug, preferred_element_type=F32)
            num = wi * qc[:, :L] + qv[:, :L]
            den = wi * qc[:, L:] + qv[:, L:]
            h = num / jnp.maximum(jnp.abs(den), jnp.exp(-m_t))
            mu = jnp.mean(h, axis=-1, keepdims=True)
            hc = h - mu
            var = jnp.mean(hc * hc, axis=-1, keepdims=True)
            hn = hc * lax.rsqrt(var + LN_EPS) * ng_ref[:, hs]
            o = op_ref[0, pl.ds(r0, L), hs].astype(F32)
            y_ref[0, pl.ds(r0, L), hs] = (jax.nn.sigmoid(o) * hn).astype(BF16)

            f_l = f_c[L - 1:L, :]
            logw = f_l - f_r + i_row
            m_new = jnp.maximum(f_l + m_prev, jnp.max(logw, axis=-1, keepdims=True))
            decay = jnp.exp(f_l + m_prev - m_new)
            w_row = jnp.exp(logw - m_new)
            upd = jnp.dot((kt * w_row).astype(BF16), vaug, preferred_element_type=F32)
            c_sc[hh] = jnp.concatenate([decay, decay], axis=1) * caug + upd
            m_sc[hh] = jnp.broadcast_to(m_new, (8, L))
        return carry

    def chunk_pair(j, carry):
        return chunk(2 * j + 1, chunk(2 * j, carry))

    lax.fori_loop(0, nc // 2, chunk_pair, 0)


def _mlstm(xm, op, gates, conv_w, conv_b, wq, wk, wkt, wv, fb, ng):
    B, S, _ = xm.shape
    full = lambda a: pl.BlockSpec(a.shape, lambda b_: (0,) * a.ndim)
    seq = pl.BlockSpec((1, S, ML_WIDTH), lambda b_: (b_, 0, 0))
    return pl.pallas_call(
        functools.partial(_mlstm_kernel, S=S),
        grid=(B,),
        in_specs=[seq, seq, pl.BlockSpec((1,) + gates.shape[1:], lambda b_: (b_, 0, 0, 0)),
                  full(conv_w), full(conv_b), full(wq), full(wk), full(wkt), full(wv), full(fb), full(ng)],
        out_specs=seq,
        out_shape=jax.ShapeDtypeStruct((B, S, ML_WIDTH), BF16),
        scratch_shapes=[pltpu.VMEM((S + 8, ML_WIDTH), F32), pltpu.VMEM((S, ML_WIDTH), BF16),
                        pltpu.VMEM((ML_HEADS, ML_DH, 2 * ML_DH), F32), pltpu.VMEM((ML_HEADS, 8, ML_CHUNK), F32)]
                       + [pltpu.VMEM((S, ML_WIDTH), BF16)] * 3
                       + [pltpu.VMEM((S // ML_CHUNK, ML_HEADS, ML_DH, ML_CHUNK), F32),
                          pltpu.VMEM((S // ML_CHUNK, ML_HEADS, 8, ML_CHUNK), F32),
                          pltpu.VMEM((S // ML_CHUNK, ML_HEADS, ML_CHUNK, ML_CHUNK), F32)],
        compiler_params=_cparams(("arbitrary",)),
        name="mlstm",
    )(xm, op, gates, conv_w, conv_b, wq, wk, wkt, wv, fb, ng)


ROW_TILE = D_MODEL // LANES


def _store_token_tiles(ref, val):
    n = val.shape[0]
    for s in range(ROW_TILE):
        ref[pl.ds(s, n, stride=ROW_TILE), :] = val[:, s * LANES:(s + 1) * LANES]


def _load_token_tiles(ref, n):
    return jnp.concatenate([ref[pl.ds(s, n, stride=ROW_TILE), :] for s in range(ROW_TILE)], axis=1)


def _prep_proj_a(proj_a):
    pa = proj_a.reshape(N_HEADS, HEAD_DIM, D_MODEL)
    return jnp.concatenate([pa, jnp.zeros_like(pa)], axis=1).reshape(QP_W, D_MODEL).astype(BF16)


def _mix_kernel(o_ref, y_ref, gm_ref, x_ref, mod_ref, pa_ref, pb_ref, wo_ref, g2_ref, rwh_ref, rwl_ref, rb_ref,
                x1_ref, h2_ref, e_ref, w_ref, *, ts):
    nt = (((1,), (1,)), ((), ()))
    a = jnp.dot(o_ref[0], pa_ref[...], preferred_element_type=F32)
    bm = jnp.dot(y_ref[0], pb_ref[...], preferred_element_type=F32)
    ga = jax.nn.sigmoid(gm_ref[0, :, :D_MODEL].astype(F32))
    gb = jax.nn.sigmoid(gm_ref[0, :, D_MODEL:].astype(F32))
    mixed = jnp.dot((ga * a + gb * bm).astype(BF16), wo_ref[...], preferred_element_type=F32)
    x1 = x_ref[0] + mod_ref[0, 2:3, :] * mixed
    x1_ref[0] = x1
    ms = jnp.mean(x1 * x1, axis=-1, keepdims=True)
    h2 = x1 * lax.rsqrt(ms + RMS_EPS) * g2_ref[...] * (1.0 + mod_ref[0, 4:5, :]) + mod_ref[0, 3:4, :]
    _store_token_tiles(h2_ref, h2)
    hh, hl = _split_bf16(h2)
    logit = (lax.dot_general(rwh_ref[...], hh, nt, preferred_element_type=F32)
             + lax.dot_general(rwh_ref[...], hl, nt, preferred_element_type=F32)
             + lax.dot_general(rwl_ref[...], hh, nt, preferred_element_type=F32)) + rb_ref[...]
    eio = lax.broadcasted_iota(I32, (N_EXPERTS, ts), 0).astype(F32)
    vals, idxs = [], []
    for _ in range(TOP_K):
        mx = jnp.max(logit, axis=0, keepdims=True)
        ix = jnp.min(jnp.where(logit == mx, eio, 1e9), axis=0, keepdims=True)
        vals.append(mx)
        idxs.append(ix)
        logit = jnp.where(eio == ix, -3e38, logit)
    ex = [jnp.exp(v - vals[0]) for v in vals]
    tot = ex[0] + ex[1] + ex[2] + ex[3]
    e_ref[...] = jnp.concatenate(idxs, axis=0).astype(I32)
    wrows = jnp.concatenate([e_ / tot for e_ in ex] + [jnp.zeros((LANES - TOP_K, ts), F32)], axis=0)
    w_ref[...] = jnp.transpose(wrows)


def _mix(o_nsa, y_ml, gm, x, mod, pa, pb, wo, g2, rwh, rwl, rb, ts=512):
    B, S, _ = x.shape
    T = B * S
    nst = S // ts
    blk = lambda wd: pl.BlockSpec((1, ts, wd), lambda b_, s: (b_, s, 0))
    full = lambda a: pl.BlockSpec(a.shape, lambda b_, s: (0,) * a.ndim)
    return pl.pallas_call(
        functools.partial(_mix_kernel, ts=ts),
        grid=(B, nst),
        in_specs=[blk(QP_W), blk(ML_WIDTH), blk(2 * D_MODEL), blk(D_MODEL),
                  pl.BlockSpec((1, 6, D_MODEL), lambda b_, s: (b_, 0, 0)),
                  full(pa), full(pb), full(wo), full(g2), full(rwh), full(rwl), full(rb)],
        out_specs=[blk(D_MODEL), pl.BlockSpec((ts * ROW_TILE, LANES), lambda b_, s: (b_ * nst + s, 0)),
                   pl.BlockSpec((TOP_K, ts), lambda b_, s: (0, b_ * nst + s)),
                   pl.BlockSpec((ts, LANES), lambda b_, s: (b_ * nst + s, 0))],
        out_shape=[jax.ShapeDtypeStruct((B, S, D_MODEL), F32), jax.ShapeDtypeStruct((T * ROW_TILE, LANES), F32),
                   jax.ShapeDtypeStruct((TOP_K, T), I32), jax.ShapeDtypeStruct((T, LANES), F32)],
        compiler_params=_cparams(("arbitrary", "arbitrary")),
        name="mix_router",
    )(o_nsa, y_ml, gm, x, mod, pa, pb, wo, g2, rwh, rwl, rb)


MOE_TM = 256


def _moe_tiles(T):
    r = T * TOP_K
    return (r + N_EXPERTS * (MOE_TM - 1) + MOE_TM - 1) // MOE_TM


def _route_kernel(e_ref, dest_ref, te_ref, na_ref, *, ntp):
    nrow = e_ref.shape[0]
    eio = lax.broadcasted_iota(I32, (N_EXPERTS, LANES), 0)
    row = lax.broadcasted_iota(I32, (LANES, LANES), 0)
    col = lax.broadcasted_iota(I32, (LANES, LANES), 1)
    triu = jnp.where(row <= col, 1.0, 0.0).astype(BF16)
    ones = jnp.ones((LANES, LANES), BF16)

    ru = 8

    def rank_body(i, counts):
        r0 = pl.multiple_of(i * ru, ru)
        e_blk = e_ref[pl.ds(r0, ru), :]
        ranks = []
        for r in range(ru):
            oh = eio == e_blk[r:r + 1, :]
            ohb = jnp.where(oh, 1.0, 0.0).astype(BF16)
            cum = jnp.dot(ohb, triu, preferred_element_type=F32)
            ranks.append(jnp.sum(jnp.where(oh, cum - 1.0 + counts, 0.0), axis=0, keepdims=True))
            counts = counts + jnp.dot(ohb, ones, preferred_element_type=F32)
        dest_ref[pl.ds(r0, ru), :] = jnp.concatenate(ranks, axis=0).astype(I32)
        return counts

    counts = lax.fori_loop(0, nrow // ru, rank_body, jnp.zeros((N_EXPERTS, LANES), F32)).astype(I32)
    tiles = (counts + (MOE_TM - 1)) // MOE_TM
    incl = tiles
    for s in (1, 2, 4, 8, 16):
        incl = incl + jnp.where(eio >= s, pltpu.roll(incl, s, 0), 0)
    off = (incl - tiles) * MOE_TM

    def dest_body(i, carry):
        r0 = pl.multiple_of(i * ru, ru)
        e_blk = e_ref[pl.ds(r0, ru), :]
        base = [jnp.sum(jnp.where(eio == e_blk[r:r + 1, :], off, 0), axis=0, keepdims=True) for r in range(ru)]
        dest_ref[pl.ds(r0, ru), :] = (dest_ref[pl.ds(r0, ru), :] + jnp.concatenate(base, axis=0)) * ROW_TILE
        return carry

    lax.fori_loop(0, nrow // ru, dest_body, 0)
    ends = jnp.concatenate([incl] * (ntp // LANES), axis=1)
    tix = lax.broadcasted_iota(I32, (N_EXPERTS, ntp), 1)
    te = jnp.sum(jnp.where(ends <= tix, 1, 0), axis=0, keepdims=True)
    te_ref[...] = jnp.minimum(te, N_EXPERTS - 1)
    na_ref[...] = jnp.concatenate([incl, counts, off], axis=0)


def _route(e_t):
    T = e_t.shape[1]
    nrow = TOP_K * T // LANES
    ntp = -(-_moe_tiles(T) // LANES) * LANES
    dest, te, meta = pl.pallas_call(
        functools.partial(_route_kernel, ntp=ntp),
        out_shape=[jax.ShapeDtypeStruct((nrow, LANES), I32), jax.ShapeDtypeStruct((1, ntp), I32),
                   jax.ShapeDtypeStruct((3 * N_EXPERTS, LANES), I32)],
        compiler_params=pltpu.CompilerParams(vmem_limit_bytes=VMEM_LIMIT),
        name="route",
    )(e_t.reshape(nrow, LANES))
    return dest.reshape(TOP_K, T), te.reshape(ntp), meta[:, 0]


MOE_TT = 512


def _row_copy(src_ref, s8, dst_ref, d8, sem):
    src = src_ref.at[pl.ds(pl.multiple_of(s8, ROW_TILE), ROW_TILE), :]
    dst = dst_ref.at[pl.ds(pl.multiple_of(d8, ROW_TILE), ROW_TILE), :]
    return pltpu.make_async_copy(src, dst, sem)


MOE_UNROLL = 4


def _dispatch_kernel(meta_ref, dest_ref, h_ref, xs_ref, zero_sc, sem, zsem, *, n_tiles):
    @pl.when(pl.program_id(0) == 0)
    def _():
        zero_sc[...] = jnp.zeros(zero_sc.shape, F32)

        def pad_range(e):
            lo = meta_ref[2 * N_EXPERTS + e] + meta_ref[N_EXPERTS + e]
            hi = meta_ref[e] * MOE_TM
            return lo, hi

        def fill_expert(e, carry):
            def fill(p, c):
                _row_copy(zero_sc, 0, xs_ref, p * ROW_TILE, zsem).start()
                return c
            return lax.fori_loop(*pad_range(e), fill, carry)

        def drain_expert(e, carry):
            def drain(p, c):
                _row_copy(zero_sc, 0, xs_ref, p * ROW_TILE, zsem).wait()
                return c
            return lax.fori_loop(*pad_range(e), drain, carry)

        lax.fori_loop(0, N_EXPERTS, fill_expert, 0)
        lax.fori_loop(0, N_EXPERTS, drain_expert, 0)

        def tile_fill(t, carry):
            r0 = pl.multiple_of(t * (MOE_TM * ROW_TILE), MOE_TM * ROW_TILE)
            cp = pltpu.make_async_copy(zero_sc, xs_ref.at[pl.ds(r0, MOE_TM * ROW_TILE), :], zsem)
            cp.start()
            cp.wait()
            return carry

        lax.fori_loop(meta_ref[N_EXPERTS - 1], n_tiles, tile_fill, 0)

    def issue(j, carry):
        i0 = j * MOE_UNROLL
        r0 = i0 * ROW_TILE
        for u in range(MOE_UNROLL):
            for k in range(TOP_K):
                _row_copy(h_ref, r0 + u * ROW_TILE, xs_ref, dest_ref[k, i0 + u], sem).start(priority=k % 2)
        return carry

    lax.fori_loop(0, MOE_TT // MOE_UNROLL, issue, 0)
    for k in range(TOP_K):
        pltpu.make_async_copy(h_ref, xs_ref.at[pl.ds(0, MOE_TT * ROW_TILE), :], sem).wait()


def _dispatch(h2, dest, meta, n_tiles):
    T = h2.shape[0] // ROW_TILE
    grid_spec = pltpu.PrefetchScalarGridSpec(
        num_scalar_prefetch=1,
        grid=(T // MOE_TT,),
        in_specs=[pl.BlockSpec((TOP_K, MOE_TT), lambda i, m: (0, i), memory_space=pltpu.SMEM),
                  pl.BlockSpec((MOE_TT * ROW_TILE, LANES), lambda i, m: (i, 0))],
        out_specs=pl.BlockSpec(memory_space=pl.ANY),
        scratch_shapes=[pltpu.VMEM((MOE_TM * ROW_TILE, LANES), F32), pltpu.SemaphoreType.DMA,
                        pltpu.SemaphoreType.DMA],
    )
    return pl.pallas_call(
        functools.partial(_dispatch_kernel, n_tiles=n_tiles),
        grid_spec=grid_spec,
        out_shape=jax.ShapeDtypeStruct((n_tiles * MOE_TM * ROW_TILE, LANES), F32),
        compiler_params=_cparams(("arbitrary",)),
        name="dispatch",
    )(meta, dest, h2)


def _ffn_kernel(te_ref, na_ref, ends_ref, x_ref, wu_hbm, bu_ref, wd_hbm, bd_ref, y_ref,
                wu_f32, wd_f32, wu_sc, wd_sc, slot_ref, sems):
    n_act = na_ref[0]
    rows = MOE_TM * ROW_TILE

    def fetch(expert, slot):
        return (pltpu.make_async_copy(wu_hbm.at[expert], wu_f32.at[slot], sems.at[0, slot]),
                pltpu.make_async_copy(wd_hbm.at[expert], wd_f32.at[slot], sems.at[1, slot]))

    for part in range(FFN_TILES):
        t = pl.program_id(0) * FFN_TILES + part
        e = te_ref[t]
        fresh = jnp.logical_and(t < n_act, jnp.logical_or(t == 0, e != te_ref[jnp.maximum(t - 1, 0)]))
        x_part = x_ref.at[part * rows:(part + 1) * rows, :]
        y_part = y_ref.at[part * rows:(part + 1) * rows, :]

        @pl.when(t == 0)
        def _():
            slot_ref[0] = 0
            for cp in fetch(e, 0):
                cp.start()

        @pl.when(fresh)
        def _():
            slot = slot_ref[0]
            nxt = ends_ref[e]

            @pl.when(nxt < n_act)
            def _():
                for cp in fetch(te_ref[nxt], 1 - slot):
                    cp.start()

            for cp in fetch(e, slot):
                cp.wait()
            wu_sc[...] = wu_f32[slot].astype(BF16)
            wd_sc[...] = wd_f32[slot].astype(BF16)
            slot_ref[0] = 1 - slot

        @pl.when(t < n_act)
        def _():
            x = _load_token_tiles(x_part, MOE_TM).astype(BF16)
            gu = jnp.dot(x, wu_sc[...], preferred_element_type=F32) + bu_ref[e]
            gl = jnp.minimum(gu[:, :D_MODEL], SWIGLU_LIMIT)
            lin = jnp.clip(gu[:, D_MODEL:], -SWIGLU_LIMIT, SWIGLU_LIMIT)
            a = gl * jax.nn.sigmoid(SWIGLU_ALPHA * gl) * (lin + 1.0)
            _store_token_tiles(y_part, jnp.dot(a.astype(BF16), wd_sc[...], preferred_element_type=F32) + bd_ref[e])

        @pl.when(t >= n_act)
        def _():
            y_part[...] = jnp.zeros(y_part.shape, F32)


FFN_TILES = 4


def _ffn(xs, te, na, ends, w_up, b_up, w_down, b_down):
    n_rows = xs.shape[0] // ROW_TILE
    nt = n_rows // MOE_TM
    assert nt % FFN_TILES == 0
    de = w_up.shape[2]
    blk = FFN_TILES * MOE_TM * ROW_TILE
    row_spec = pl.BlockSpec((blk, LANES), lambda i, te_, na_, en_: (i, 0))
    x_spec = pl.BlockSpec((blk, LANES), lambda i, te_, na_, en_: (jnp.minimum(i, (na_[0] - 1) // FFN_TILES), 0))
    grid_spec = pltpu.PrefetchScalarGridSpec(
        num_scalar_prefetch=3,
        grid=(nt // FFN_TILES,),
        in_specs=[x_spec,
                  pl.BlockSpec(memory_space=pl.ANY),
                  pl.BlockSpec((N_EXPERTS, 1, de), lambda i, te_, na_, en_: (0, 0, 0)),
                  pl.BlockSpec(memory_space=pl.ANY),
                  pl.BlockSpec((N_EXPERTS, 1, D_MODEL), lambda i, te_, na_, en_: (0, 0, 0))],
        out_specs=row_spec,
        scratch_shapes=[pltpu.VMEM((2, D_MODEL, de), F32), pltpu.VMEM((2, de // 2, D_MODEL), F32),
                        pltpu.VMEM((D_MODEL, de), BF16), pltpu.VMEM((de // 2, D_MODEL), BF16),
                        pltpu.SMEM((1,), I32), pltpu.SemaphoreType.DMA((2, 2))],
    )
    return pl.pallas_call(
        _ffn_kernel,
        grid_spec=grid_spec,
        out_shape=jax.ShapeDtypeStruct(xs.shape, F32),
        compiler_params=_cparams(("arbitrary",)),
        name="expert_ffn",
    )(te, na, ends, xs, w_up, b_up.reshape(N_EXPERTS, 1, de), w_down, b_down.reshape(N_EXPERTS, 1, D_MODEL))


def _combine_kernel(dest_ref, dnext_ref, y_ref, w_ref, x1_ref, mod_ref, fg_ref, o_ref, buf, sems):
    i = pl.program_id(0)
    n = pl.num_programs(0)
    slot = i % 2
    nslot = 1 - slot
    grp = 32

    def gather(d_ref, tok, tok8, s):
        for k in range(TOP_K):
            _row_copy(y_ref, d_ref[k, tok], buf.at[s, k], tok8, sems.at[s]).start(priority=k % 2)

    def retire(s):
        for k in range(TOP_K):
            pltpu.make_async_copy(y_ref.at[pl.ds(0, MOE_TT * ROW_TILE), :], buf.at[s, k], sems.at[s]).wait()

    @pl.when(i == 0)
    def _():
        def first(tok, carry):
            gather(dest_ref, tok, tok * ROW_TILE, 0)
            return carry
        lax.fori_loop(0, MOE_TT, first, 0)

    retire(slot)
    g2 = mod_ref[0, 5:6, :]
    fg = fg_ref[...]

    def body(j, carry):
        base = pl.multiple_of(j * grp, grp)
        w8 = w_ref[pl.ds(base, grp), :]
        moe = None
        for k in range(TOP_K):
            rows = jnp.concatenate(
                [buf[slot, k, pl.ds(base * ROW_TILE + s, grp, stride=ROW_TILE), :] for s in range(ROW_TILE)], axis=1)
            term = w8[:, k:k + 1] * rows
            moe = term if moe is None else moe + term
        x2 = x1_ref[pl.ds(base, grp), :] + g2 * moe
        ms = jnp.mean(x2 * x2, axis=-1, keepdims=True)
        o_ref[pl.ds(base, grp), :] = x2 * lax.rsqrt(ms + RMS_EPS) * fg
        base8 = base * ROW_TILE
        for u in range(grp):
            gather(dnext_ref, base + u, base8 + u * ROW_TILE, nslot)
        return carry

    lax.fori_loop(0, MOE_TT // grp, body, 0)

    @pl.when(i == n - 1)
    def _():
        retire(nslot)


def _combine(y, dest, wcol, x1, mod, final_g, S):
    T = x1.shape[0]
    per_b = S // MOE_TT
    n = T // MOE_TT
    return pl.pallas_call(
        _combine_kernel,
        grid=(n,),
        in_specs=[pl.BlockSpec((TOP_K, MOE_TT), lambda i: (0, i), memory_space=pltpu.SMEM),
                  pl.BlockSpec((TOP_K, MOE_TT), lambda i: (0, jnp.minimum(i + 1, n - 1)), memory_space=pltpu.SMEM),
                  pl.BlockSpec(memory_space=pl.ANY),
                  pl.BlockSpec((MOE_TT, LANES), lambda i: (i, 0)),
                  pl.BlockSpec((MOE_TT, D_MODEL), lambda i: (i, 0)),
                  pl.BlockSpec((1, 6, D_MODEL), lambda i: (i // per_b, 0, 0)),
                  pl.BlockSpec((1, D_MODEL), lambda i: (0, 0))],
        out_specs=pl.BlockSpec((MOE_TT, D_MODEL), lambda i: (i, 0)),
        out_shape=jax.ShapeDtypeStruct((T, D_MODEL), F32),
        scratch_shapes=[pltpu.VMEM((2, TOP_K, MOE_TT * ROW_TILE, LANES), F32), pltpu.SemaphoreType.DMA((2,))],
        compiler_params=_cparams(("arbitrary",)),
        name="combine",
    )(dest, dest, y, wcol, x1, mod, final_g.reshape(1, D_MODEL))


def _layer(x, c, ada_w, ada_b, norm1_g, w_in, b_in, cmp_pe_k, cmp_w1_k, cmp_w2_k, cmp_pe_v, cmp_w1_v, cmp_w2_v,
           ml_conv_w, ml_conv_b, ml_wq, ml_wk, ml_wv, ml_f_bias, ml_norm_g, proj_a, proj_b, w_out, norm2_g,
           router_w, router_b, exp_w_up, exp_b_up, exp_w_down, exp_b_down, final_g):
    B, S, D = x.shape
    T = B * S
    mod = _adaln(c, ada_w, ada_b).reshape(B, 6, D)
    w, b = _prep_w_in(w_in, b_in)
    q, kc_in, vc_in, kv6, xm, op, gm, small, gates = _inproj(x, mod, norm1_g, w, b)
    prep = [_prep_compress(cmp_pe_k, cmp_w1_k, cmp_w2_k), _prep_compress(cmp_pe_v, cmp_w1_v, cmp_w2_v)]
    wk, pe, w2 = [jnp.stack([prep[0][i], prep[1][i]]) for i in range(3)]
    kc, vc = _compress(kc_in, vc_in, wk, pe, w2)
    o_nsa = _nsa(q, kc, vc, kv6, small, _nsa_consts(S))
    gates = gates.reshape(B, 2 * ML_HEADS, S // ML_CHUNK, ML_CHUNK)
    y_ml = _mlstm(xm, op, gates, ml_conv_w, ml_conv_b.reshape(1, ML_WIDTH), ml_wq.astype(BF16), ml_wk.astype(BF16),
                  jnp.transpose(ml_wk, (0, 2, 1)).astype(BF16), ml_wv.astype(BF16),
                  jnp.broadcast_to(ml_f_bias[:, None], (ML_HEADS, LANES)), ml_norm_g.reshape(1, ML_WIDTH))
    rwt = jnp.transpose(router_w)
    rwh = rwt.astype(BF16)
    rwl = (rwt - rwh.astype(F32)).astype(BF16)
    x1, h2, e_t, wcol = _mix(o_nsa, y_ml, gm, x, mod, _prep_proj_a(proj_a), proj_b.astype(BF16), w_out.astype(BF16),
                             norm2_g.reshape(1, D), rwh, rwl, router_b.reshape(N_EXPERTS, 1))
    dest, te, meta = _route(e_t)
    ends = meta[:N_EXPERTS]
    xs = _dispatch(h2, dest, meta, _moe_tiles(T))
    y = _ffn(xs, te, ends[N_EXPERTS - 1:], ends, exp_w_up, exp_b_up, exp_w_down, exp_b_down)
    out = _combine(y, dest, wcol, x1.reshape(T, D), mod, final_g, S)
    return out.reshape(B, S, D)


def kernel(x, c, ada_w, ada_b, norm1_g, w_in, b_in, cmp_pe_k, cmp_w1_k, cmp_w2_k, cmp_pe_v, cmp_w1_v, cmp_w2_v, ml_conv_w, ml_conv_b, ml_wq, ml_wk, ml_wv, ml_f_bias, ml_norm_g, proj_a, proj_b, w_out, norm2_g, router_w, router_b, exp_w_up, exp_b_up, exp_w_down, exp_b_down, final_g):
    assert ada_w.shape[0] == 1, "one layer: the final RMSNorm is fused into the layer's last kernel"
    layer = (ada_w, ada_b, norm1_g, w_in, b_in, cmp_pe_k, cmp_w1_k, cmp_w2_k, cmp_pe_v, cmp_w1_v, cmp_w2_v,
             ml_conv_w, ml_conv_b, ml_wq, ml_wk, ml_wv, ml_f_bias, ml_norm_g, proj_a, proj_b, w_out, norm2_g,
             router_w, router_b, exp_w_up, exp_b_up, exp_w_down, exp_b_down)
    return _layer(x, c, *[p[0] for p in layer], final_g)
```

```python
import functools
import math

import numpy as np
import jax
import jax.numpy as jnp
from jax import lax
from jax.experimental import pallas as pl
from jax.experimental.pallas import tpu as pltpu

F32 = jnp.float32
BF16 = jnp.bfloat16
I32 = jnp.int32

D_MODEL = 1024
N_HEADS = 8
N_GROUPS = 2
HPG = N_HEADS // N_GROUPS
HEAD_DIM = 64
CMP_BLOCK = 32
CMP_STRIDE = 16
CMP_HIDDEN = 128
SEL_BLOCK = 64
SEL_TOPN = 8
WINDOW = 256
ATTN_SCALE = HEAD_DIM ** -0.5
ML_HEADS = 4
ML_DH = 128
ML_WIDTH = ML_HEADS * ML_DH
ML_CHUNK = 128
CONV_W = 4
N_EXPERTS = 32
TOP_K = 4
SWIGLU_LIMIT = 7.0
SWIGLU_ALPHA = 1.702
RMS_EPS = 1e-5
LN_EPS = 1e-5

LANES = 128
NEG = -1e30
VMEM_LIMIT = 56 * 1024 * 1024

QP_W = N_HEADS * LANES
C_Q = 0
C_KC = C_Q + QP_W
C_VC = C_KC + LANES
C_KV6 = C_VC + LANES
C_XM = C_KV6 + 6 * LANES
C_OP = C_XM + ML_WIDTH
C_GM = C_OP + ML_WIDTH
C_SM = C_GM + 2 * D_MODEL
IN_W = C_SM + LANES


def _cparams(sem):
    return pltpu.CompilerParams(dimension_semantics=sem, vmem_limit_bytes=VMEM_LIMIT)


def _adaln_kernel(c_ref, w_ref, b_ref, o_ref):
    c = c_ref[...]
    sc = (c * jax.nn.sigmoid(c)).astype(BF16)
    o_ref[...] = jnp.dot(sc, w_ref[...].astype(BF16), preferred_element_type=F32) + b_ref[...]


def _adaln(c, ada_w, ada_b):
    B = c.shape[0]
    tn = 1024
    n = ada_w.shape[1]
    return pl.pallas_call(
        _adaln_kernel,
        grid=(n // tn,),
        in_specs=[pl.BlockSpec((B, D_MODEL), lambda j: (0, 0)),
                  pl.BlockSpec((D_MODEL, tn), lambda j: (0, j)),
                  pl.BlockSpec((1, tn), lambda j: (0, j))],
        out_specs=pl.BlockSpec((B, tn), lambda j: (0, j)),
        out_shape=jax.ShapeDtypeStruct((B, n), F32),
        compiler_params=_cparams(("arbitrary",)),
        name="adaln",
    )(c, ada_w, ada_b.reshape(1, n))


def _prep_w_in(w_in, b_in):
    def pad_q(a):
        lead = a.shape[:-1]
        a = a.reshape(lead + (N_HEADS, HEAD_DIM))
        z = jnp.zeros_like(a[..., :HPG, :])
        lo = jnp.concatenate([a[..., :HPG, :], z], axis=-1)
        hi = jnp.concatenate([z, a[..., HPG:, :]], axis=-1)
        return jnp.concatenate([lo, hi], axis=-2).reshape(lead + (QP_W,))

    def v_aug(a, fill):
        f = jnp.full(a.shape[:-1] + (HEAD_DIM,), fill, a.dtype)
        return jnp.concatenate([a[..., :HEAD_DIM], f, a[..., HEAD_DIM:], f], axis=-1)

    def cols(a):
        sl = lambda lo, hi: lax.slice_in_dim(a, lo, hi, axis=-1)
        return dict(q=sl(0, 512), kc=sl(512, 640), vc=sl(640, 768), ks=sl(768, 896), vs=sl(896, 1024),
                    kw=sl(1024, 1152), vw=sl(1152, 1280), gn=sl(1280, 1304), xm=sl(1304, 1816),
                    op=sl(1816, 2328), fi=sl(2328, 2336), gm=sl(2336, 4384))

    def layout(c, fill):
        zs = jnp.zeros(c["q"].shape[:-1] + (LANES - 32,), c["q"].dtype)
        return jnp.concatenate([pad_q(c["q"]), c["kc"], c["vc"], c["ks"], c["kw"], v_aug(c["vs"], fill),
                                v_aug(c["vw"], fill), c["xm"], c["op"], c["gm"], c["gn"], c["fi"], zs], axis=-1)

    w = layout(cols(w_in), 0.0)
    b = layout(cols(b_in), 1.0)
    return w.astype(BF16), b.reshape(1, IN_W)


def _inproj_kernel(x_ref, mod_ref, g_ref, w_ref, b_ref,
                   q_ref, kc_ref, vc_ref, kv_ref, xm_ref, op_ref, gm_ref, sm_ref, gt_ref):
    x = x_ref[0]
    ms = jnp.mean(x * x, axis=-1, keepdims=True)
    y = x * lax.rsqrt(ms + RMS_EPS) * g_ref[...]
    h = (y * (1.0 + mod_ref[0, 1:2, :]) + mod_ref[0, 0:1, :]).astype(BF16)

    def seg(lo, hi):
        return jnp.dot(h, w_ref[:, lo:hi], preferred_element_type=F32) + b_ref[:, lo:hi]

    q = seg(C_Q, C_KC).astype(BF16)
    for hd in range(N_HEADS):
        q_ref[0, hd] = q[:, hd * LANES:(hd + 1) * LANES]
    kv = seg(C_KC, C_XM)
    kc_ref[0] = kv[:, 0:LANES]
    vc_ref[0] = kv[:, LANES:2 * LANES]
    kv_ref[0] = kv[:, 2 * LANES:].astype(BF16)
    xm_ref[0] = seg(C_XM, C_OP)
    op_ref[0] = seg(C_OP, C_GM).astype(BF16)
    gs = seg(C_GM, IN_W)
    gm_ref[0] = gs[:, :C_SM - C_GM].astype(BF16)
    sm = gs[:, C_SM - C_GM:]
    sm_ref[0] = sm
    gt_ref[0] = jnp.transpose(sm)[3 * N_HEADS:3 * N_HEADS + 2 * ML_HEADS, :]


def _inproj(x, mod, norm_g, w, b, ts=512):
    B, S, _ = x.shape
    widths = [(LANES, F32), (LANES, F32), (6 * LANES, BF16), (ML_WIDTH, F32),
              (ML_WIDTH, BF16), (2 * D_MODEL, BF16), (LANES, F32)]
    q_spec = pl.BlockSpec((1, N_HEADS, ts, LANES), lambda b_, s: (b_, 0, s, 0))
    gt_spec = pl.BlockSpec((1, 2 * ML_HEADS, ts), lambda b_, s: (b_, 0, s))
    return pl.pallas_call(
        _inproj_kernel,
        grid=(B, S // ts),
        in_specs=[pl.BlockSpec((1, ts, D_MODEL), lambda b_, s: (b_, s, 0)),
                  pl.BlockSpec((1, 6, D_MODEL), lambda b_, s: (b_, 0, 0)),
                  pl.BlockSpec((1, D_MODEL), lambda b_, s: (0, 0)),
                  pl.BlockSpec((D_MODEL, IN_W), lambda b_, s: (0, 0)),
                  pl.BlockSpec((1, IN_W), lambda b_, s: (0, 0))],
        out_specs=([q_spec] + [pl.BlockSpec((1, ts, wd), lambda b_, s: (b_, s, 0)) for wd, _ in widths]
                   + [gt_spec]),
        out_shape=([jax.ShapeDtypeStruct((B, N_HEADS, S, LANES), BF16)]
                   + [jax.ShapeDtypeStruct((B, S, wd), dt) for wd, dt in widths]
                   + [jax.ShapeDtypeStruct((B, 2 * ML_HEADS, S), F32)]),
        compiler_params=_cparams(("arbitrary", "arbitrary")),
        name="inproj",
    )(x, mod, norm_g.reshape(1, D_MODEL), w, b)


def _prep_compress(pe, w1, w2):
    w1r = w1.reshape(2, CMP_STRIDE, HEAD_DIM, CMP_HIDDEN)
    z = jnp.zeros_like(w1r[0])
    cols = []
    for g in range(N_GROUPS):
        for a in range(2):
            parts = [w1r[a] if gg == g else z for gg in range(N_GROUPS)]
            cols.append(jnp.concatenate(parts, axis=1).reshape(CMP_STRIDE * LANES, CMP_HIDDEN))
    wk = jnp.concatenate(cols, axis=1)
    per = pe.reshape(2, CMP_STRIDE, HEAD_DIM)
    pe_rows = jnp.concatenate([per, per], axis=-1).reshape(2, CMP_STRIDE * LANES)
    pe_rows = jnp.concatenate([pe_rows, jnp.zeros((6, CMP_STRIDE * LANES), pe.dtype)], axis=0)
    z2 = jnp.zeros_like(w2)
    w2p = jnp.stack([jnp.concatenate([w2, z2], axis=1), jnp.concatenate([z2, w2], axis=1)])
    return wk.astype(BF16), pe_rows.astype(BF16), w2p.astype(BF16)


def _compress_kernel(ak_ref, av_ref, wk_ref, pe_ref, w2_ref, ok_ref, ov_ref):
    n = ok_ref.shape[1]

    def hidden(i, a_ref, g):
        a = jnp.concatenate([a_ref.at[0][pl.ds(r, n, stride=CMP_STRIDE), :] for r in range(CMP_STRIDE)], axis=1)
        y = jnp.dot(a.astype(BF16), wk_ref[i], preferred_element_type=F32)
        yp = jnp.dot(pe_ref[i], wk_ref[i], preferred_element_type=F32)
        ca, cb = 2 * g * LANES, (2 * g + 1) * LANES
        bias = yp[0:1, ca:ca + LANES] + yp[1:2, cb:cb + LANES]
        hid = y[:, ca:ca + LANES] + pltpu.roll(y[:, cb:cb + LANES], LANES - 1, 0) + bias
        return jax.nn.gelu(hid, approximate=True).astype(BF16)

    ok_ref[0] = (jnp.dot(hidden(0, ak_ref, 0), w2_ref[0, 0], preferred_element_type=F32)
                 + jnp.dot(hidden(0, ak_ref, 1), w2_ref[0, 1], preferred_element_type=F32)).astype(BF16)
    ones_hi = jnp.where(lax.broadcasted_iota(I32, (LANES, LANES), 1) >= HEAD_DIM, 1.0, 0.0)
    for g in range(N_GROUPS):
        v = jnp.dot(hidden(1, av_ref, g), w2_ref[1, 0], preferred_element_type=F32)
        ov_ref[0, g] = (v + ones_hi).astype(BF16)


def _compress(kc_in, vc_in, wk, pe, w2):
    B, S, _ = kc_in.shape
    n = S // CMP_STRIDE
    a_spec = pl.BlockSpec((1, S, LANES), lambda b_: (b_, 0, 0))
    full = lambda shp: pl.BlockSpec(shp, lambda b_: (0,) * len(shp))
    return pl.pallas_call(
        _compress_kernel,
        grid=(B,),
        in_specs=[a_spec, a_spec, full(wk.shape), full(pe.shape), full(w2.shape)],
        out_specs=[pl.BlockSpec((1, n, LANES), lambda b_: (b_, 0, 0)),
                   pl.BlockSpec((1, N_GROUPS, n, LANES), lambda b_: (b_, 0, 0, 0))],
        out_shape=[jax.ShapeDtypeStruct((B, n, LANES), BF16), jax.ShapeDtypeStruct((B, N_GROUPS, n, LANES), BF16)],
        compiler_params=_cparams(("arbitrary",)),
        name="compress",
    )(kc_in, vc_in, wk, pe, w2)


def _nsa_consts(S):
    n_cmp = (S - CMP_BLOCK) // CMP_STRIDE + 1
    n_slc = S // SEL_BLOCK
    cs = np.arange(n_cmp)[:, None] * CMP_STRIDE
    ss = np.arange(n_slc)[None, :] * SEL_BLOCK
    ov = np.clip(np.minimum(cs + CMP_BLOCK, ss + SEL_BLOCK) - np.maximum(cs, ss), 0, None) / CMP_BLOCK
    ovp = np.zeros((LANES, LANES), np.float32)
    ovp[:n_cmp, :n_slc] = ov
    npair = S // (2 * LANES)
    ex = np.zeros((npair, LANES, 2 * LANES), np.float32)
    for kp in range(npair):
        for l in range(2 * LANES):
            ex[kp, (kp * 2 * LANES + l) // SEL_BLOCK, l] = 1.0
    return jnp.asarray(ovp.T, BF16), jnp.asarray(ex, BF16)


M_FLOOR = 0.5 * NEG


def _nsa_kernel(q_ref, kc_ref, vc_ref, ks_ref, kw_ref, vs0_ref, vs1_ref, vw0_ref, vw1_ref, gate_ref, ovt_ref, ex_ref,
                 o_ref, s_sc, mx_sc, mb_sc, acc_sc, out_sc, *, tq):
    t0 = pl.program_id(1) * tq
    n_slc = ks_ref.shape[1] // SEL_BLOCK
    tpos = t0 + lax.broadcasted_iota(I32, (tq, LANES), 0)
    lane = lax.broadcasted_iota(I32, (tq, LANES), 1)
    lane_row = lax.broadcasted_iota(I32, (1, LANES), 1)
    slopes = [2.0 ** -(h + 1) for h in range(N_HEADS)]
    rows = [slice(h * tq, (h + 1) * tq) for h in range(N_HEADS)]
    grp = [h // HPG for h in range(N_HEADS)]
    vs_refs, vw_refs = (vs0_ref, vs1_ref), (vw0_ref, vw1_ref)
    nt = (((1,), (1,)), ((), ()))
    q8 = q_ref[0].reshape(N_HEADS * tq, LANES) * ATTN_SCALE
    gates = jax.nn.sigmoid(gate_ref[0])

    def inv_denom(pv):
        return 1.0 / jnp.maximum(jnp.where(lane < HEAD_DIM, pltpu.roll(pv, HEAD_DIM, 1), pv), 1e-30)

    def gated(br, h, pv):
        c = br * N_HEADS + h
        return jnp.broadcast_to(gates[:, c:c + 1], (tq, LANES)) * (pv * inv_denom(pv))

    s_all = lax.dot_general(q8, kc_ref[0], nt, preferred_element_type=F32)
    cmask = jnp.where(lane * CMP_STRIDE + (CMP_BLOCK - 1) <= tpos, 0.0, NEG)
    crel = (lane_row * CMP_STRIDE - t0).astype(F32)
    psum = [jnp.zeros((tq, LANES), F32) for _ in range(N_GROUPS)]
    for h in range(N_HEADS):
        s = s_all[rows[h]] + (cmask + slopes[h] * crel)
        m = jnp.maximum(jnp.max(s, axis=-1, keepdims=True), M_FLOOR)
        p = jnp.exp(s - m)
        pv = jnp.dot(p.astype(BF16), vc_ref[0, grp[h]], preferred_element_type=F32)
        psum[grp[h]] = psum[grp[h]] + p * inv_denom(pv)
        out_sc[rows[h], :] = gated(0, h, pv)

    blk = lax.broadcasted_iota(I32, (n_slc, tq), 0)
    cur = (t0 + lax.broadcasted_iota(I32, (n_slc, tq), 1)) // SEL_BLOCK
    blkf = blk.astype(F32)
    sel = []
    for g in range(N_GROUPS):
        ph, plo = _split_bf16(psum[g])
        imp = (lax.dot_general(ovt_ref[...], ph, nt, preferred_element_type=F32)
               + lax.dot_general(ovt_ref[...], plo, nt, preferred_element_type=F32))[0:n_slc]
        sc = jnp.where((blk == cur) | (blk == 0), -NEG, jnp.where(blk > cur, NEG, imp))
        selt = jnp.zeros((n_slc, tq), F32)
        for _ in range(SEL_TOPN):
            mx = jnp.max(sc, axis=0, keepdims=True)
            idx = jnp.min(jnp.where(sc == mx, blkf, 1e9), axis=0, keepdims=True)
            hit = blkf == idx
            selt = jnp.where(hit, 1.0, selt)
            sc = jnp.where(hit, -3e38, sc)
        selt = jnp.where(blk <= cur, selt, 0.0)
        sel.append(jnp.transpose(jnp.concatenate([selt, jnp.zeros((LANES - n_slc, tq), F32)], axis=0)).astype(BF16))

    KT = NSA_KCHUNK

    def scores(slots, k2, kpos, masks):
        s2 = lax.dot_general(q8, k2, nt, preferred_element_type=F32)
        rel2 = jnp.concatenate([(kp + lane_row - t0).astype(F32) for kp in kpos], axis=1)
        for h in range(N_HEADS):
            s = s2[rows[h]] + (masks[grp[h]] + slopes[h] * rel2)
            for j in range(2):
                s_sc[slots[j], rows[h], :] = s[:, j * LANES:(j + 1) * LANES]
            mx_sc[rows[h], :] = jnp.maximum(mx_sc[rows[h], :], jnp.maximum(s[:, :LANES], s[:, LANES:]))

    def finish_max():
        for h in range(N_HEADS):
            m = jnp.maximum(jnp.max(mx_sc[rows[h], :], axis=-1, keepdims=True), M_FLOOR)
            mb_sc[rows[h], :] = jnp.broadcast_to(m, (tq, LANES))

    def weighted(c, v_c):
        for h in range(N_HEADS):
            s4 = jnp.concatenate([s_sc[KT * c + j, rows[h], :] for j in range(KT)], axis=1)
            mb = mb_sc[rows[h], :]
            p = jnp.exp(s4 - jnp.concatenate([mb] * KT, axis=1)).astype(BF16)
            acc_sc[rows[h], :] += jnp.dot(p, v_c[grp[h]], preferred_element_type=F32)

    mx_sc[...] = jnp.full(mx_sc.shape, NEG, F32)
    acc_sc[...] = jnp.zeros(acc_sc.shape, F32)
    n_chunks = (t0 + tq + KT * LANES - 1) // (KT * LANES)
    tpos2 = t0 + lax.broadcasted_iota(I32, (tq, 2 * LANES), 0)
    lane2 = lax.broadcasted_iota(I32, (tq, 2 * LANES), 1)

    def slc_scores(kp, carry):
        r0 = pl.multiple_of(kp * 2 * LANES, 2 * LANES)
        causal2 = r0 + lane2 <= tpos2
        masks = [jnp.where((jnp.dot(sel[g], ex_ref[kp], preferred_element_type=F32) > 0.5) & causal2, 0.0, NEG)
                 for g in range(N_GROUPS)]
        scores((2 * kp, 2 * kp + 1), ks_ref[0, pl.ds(r0, 2 * LANES), :], (r0, r0 + LANES), masks)
        return carry

    lax.fori_loop(0, (KT // 2) * n_chunks, slc_scores, 0)
    finish_max()

    def slc_weighted(c, carry):
        r0 = pl.multiple_of(c * KT * LANES, KT * LANES)
        weighted(c, [r[0, pl.ds(r0, KT * LANES), :] for r in vs_refs])
        return carry

    lax.fori_loop(0, n_chunks, slc_weighted, 0)
    for h in range(N_HEADS):
        out_sc[rows[h], :] += gated(1, h, acc_sc[rows[h], :])

    mx_sc[...] = jnp.full(mx_sc.shape, NEG, F32)
    acc_sc[...] = jnp.zeros(acc_sc.shape, F32)
    v_tiles = [[] for _ in range(N_GROUPS)]
    for half in range(KT // 2):
        k_parts, kpos, masks = [], [], []
        for r in (2 * half, 2 * half + 1):
            kt = t0 // LANES - WINDOW // LANES + r
            r0 = pl.multiple_of(jnp.maximum(kt, 0) * LANES, LANES)
            spos = kt * LANES + lane
            dist = tpos - spos
            masks.append(jnp.where((dist >= 0) & (dist < WINDOW) & (spos >= 0), 0.0, NEG))
            k_parts.append(kw_ref[0, pl.ds(r0, LANES), :])
            kpos.append(kt * LANES)
            for g in range(N_GROUPS):
                v_tiles[g].append(vw_refs[g][0, pl.ds(r0, LANES), :])
        m2 = jnp.concatenate(masks, axis=1)
        scores((2 * half, 2 * half + 1), jnp.concatenate(k_parts, axis=0), kpos, [m2] * N_GROUPS)
    finish_max()
    weighted(0, [jnp.concatenate(v_tiles[g], axis=0) for g in range(N_GROUPS)])
    for h in range(N_HEADS):
        o = out_sc[rows[h], :] + gated(2, h, acc_sc[rows[h], :])
        o_ref[0, :, h * LANES:(h + 1) * LANES] = o.astype(BF16)


def _nsa(q, kc, vc, kv6, small, consts):
    ovt, ex = consts
    B, _, S, _ = q.shape
    tq = NSA_TQ
    seq = lambda j: pl.BlockSpec((1, S, LANES), lambda b_, i: (b_, 0, j))
    nrow = N_HEADS * tq
    return pl.pallas_call(
        functools.partial(_nsa_kernel, tq=tq),
        grid=(B, S // tq),
        in_specs=[pl.BlockSpec((1, N_HEADS, tq, LANES), lambda b_, i: (b_, 0, i, 0)),
                  pl.BlockSpec((1, LANES, LANES), lambda b_, i: (b_, 0, 0)),
                  pl.BlockSpec((1, N_GROUPS, LANES, LANES), lambda b_, i: (b_, 0, 0, 0)),
                  seq(0), seq(1), seq(2), seq(3), seq(4), seq(5),
                  pl.BlockSpec((1, tq, LANES), lambda b_, i: (b_, i, 0)),
                  pl.BlockSpec(ovt.shape, lambda b_, i: (0, 0)),
                  pl.BlockSpec(ex.shape, lambda b_, i: (0, 0, 0))],
        out_specs=pl.BlockSpec((1, tq, QP_W), lambda b_, i: (b_, i, 0)),
        out_shape=jax.ShapeDtypeStruct((B, S, QP_W), BF16),
        scratch_shapes=[pltpu.VMEM((S // LANES, nrow, LANES), F32)] + [pltpu.VMEM((nrow, LANES), F32)] * 4,
        compiler_params=_cparams(("arbitrary", "arbitrary")),
        name="nsa",
    )(q, kc, vc, kv6, kv6, kv6, kv6, kv6, kv6, small, ovt, ex)


NSA_KCHUNK = 4
NSA_TQ = (NSA_KCHUNK - WINDOW // LANES) * LANES


def _split_bf16(a):
    hi = a.astype(BF16)
    return hi, (a - hi.astype(F32)).astype(BF16)


def _mlstm_kernel(xm_ref, op_ref, gate_ref, cw_ref, cb_ref, wq_ref, wk_ref, wkt_ref, wv_ref, fb_ref, ng_ref,
                  y_ref, xpad_sc, xc_sc, c_sc, m_sc, q_sc, k_sc, v_sc, kt_sc, fr_sc, fc_sc, *, S):
    L = ML_CHUNK
    nt = (((1,), (1,)), ((), ()))
    xpad_sc[0:8, :] = jnp.zeros((8, ML_WIDTH), F32)
    xpad_sc[8:, :] = xm_ref[0]
    rb = 256
    for r in range(S // rb):
        acc = jnp.zeros((rb, ML_WIDTH), F32) + cb_ref[...]
        for j in range(CONV_W):
            o = 8 + r * rb - (CONV_W - 1) + j
            acc = acc + cw_ref[j:j + 1, :] * xpad_sc[o:o + rb, :]
        xc_sc[r * rb:(r + 1) * rb, :] = (acc * jax.nn.sigmoid(acc)).astype(BF16)

    c_sc[...] = jnp.zeros(c_sc.shape, F32)
    m_sc[...] = jnp.zeros(m_sc.shape, F32)
    row = lax.broadcasted_iota(I32, (L, L), 0)
    col = lax.broadcasted_iota(I32, (L, L), 1)
    causal = col <= row
    triu = jnp.where(row <= col, 1.0, 0.0).astype(BF16)
    ones = jnp.ones((L, L), BF16)
    rs = 1.0 / math.sqrt(ML_DH)
    nc = S // L

    for hh in range(ML_HEADS):
        hs = slice(hh * ML_DH, (hh + 1) * ML_DH)
        xc = xc_sc[:, hs]
        q_sc[:, hs] = jnp.dot(xc, wq_ref[hh], preferred_element_type=F32).astype(BF16)
        k_sc[:, hs] = (jnp.dot(xc, wk_ref[hh], preferred_element_type=F32) * rs).astype(BF16)
        v_sc[:, hs] = jnp.dot(xm_ref[0, :, hs].astype(BF16), wv_ref[hh], preferred_element_type=F32).astype(BF16)
        kt = lax.dot_general(wkt_ref[hh], xc, nt, preferred_element_type=F32) * rs
        for c in range(nc):
            kt_sc[c, hh] = kt[:, c * L:(c + 1) * L]

    def gate_sums(c, carry):
        for hh in range(ML_HEADS):
            z = gate_ref[0, ML_HEADS + hh, pl.ds(c, 1), :] + fb_ref[hh:hh + 1, :]
            lf_row = jnp.minimum(z, 0.0) - jnp.log(1.0 + jnp.exp(-jnp.abs(z)))
            rh, rl = _split_bf16(jnp.broadcast_to(lf_row, (8, L)))
            fr_sc[c, hh] = (jnp.dot(rh, triu, preferred_element_type=F32)
                            + jnp.dot(rl, triu, preferred_element_type=F32))
            lh, ll = _split_bf16(jnp.where(causal, jnp.broadcast_to(lf_row, (L, L)), 0.0))
            fc_sc[c, hh] = (jnp.dot(lh, ones, preferred_element_type=F32)
                            + jnp.dot(ll, ones, preferred_element_type=F32))
        return carry

    lax.fori_loop(0, nc, gate_sums, 0)

    def chunk(c, carry):
        r0 = pl.multiple_of(c * L, L)
        for hh in range(ML_HEADS):
            hs = slice(hh * ML_DH, (hh + 1) * ML_DH)
            q = q_sc[pl.ds(r0, L), hs]
            k = k_sc[pl.ds(r0, L), hs]
            kt = kt_sc[c, hh]
            vaug = jnp.concatenate([v_sc[pl.ds(r0, L), hs], ones], axis=1)
            i_row = gate_ref[0, hh, pl.ds(c, 1), :]
            f_r = fr_sc[c, hh, 0:1, :]
            f_c = fc_sc[c, hh]
            m_prev = m_sc[hh, 0:1, :]
            logd = jnp.where(causal, f_c - f_r + i_row, NEG)
            inter = f_c + m_prev
            m_t = jnp.maximum(inter, jnp.max(logd, axis=-1, keepdims=True))
            dm = jnp.where(causal, jnp.exp(logd - m_t), 0.0)
            wi = jnp.exp(inter - m_t)
            qk = (lax.dot_general(q, k, nt, preferred_element_type=F32) * dm).astype(BF16)
            caug = c_sc[hh]
            qc = jnp.dot(q, caug.astype(BF16), preferred_element_type=F32)
            qv = jnp.dot(qk, vaug, preferred_element_type=F32)
            num = wi * qc[:, :L] + qv[:, :L]
            den = wi * qc[:, L:] + qv[:, L:]
            h = num / jnp.maximum(jnp.abs(den), jnp.exp(-m_t))
            mu = jnp.mean(h, axis=-1, keepdims=True)
            hc = h - mu
            var = jnp.mean(hc * hc, axis=-1, keepdims=True)
            hn = hc * lax.rsqrt(var + LN_EPS) * ng_ref[:, hs]
            o = op_ref[0, pl.ds(r0, L), hs].astype(F32)
            y_ref[0, pl.ds(r0, L), hs] = (jax.nn.sigmoid(o) * hn).astype(BF16)

            f_l = f_c[L - 1:L, :]
            logw = f_l - f_r + i_row
            m_new = jnp.maximum(f_l + m_prev, jnp.max(logw, axis=-1, keepdims=True))
            decay = jnp.exp(f_l + m_prev - m_new)
            w_row = jnp.exp(logw - m_new)
            upd = jnp.dot((kt * w_row).astype(BF16), vaug, preferred_element_type=F32)
            c_sc[hh] = jnp.concatenate([decay, decay], axis=1) * caug + upd
            m_sc[hh] = jnp.broadcast_to(m_new, (8, L))
        return carry

    def chunk_pair(j, carry):
        return chunk(2 * j + 1, chunk(2 * j, carry))

    lax.fori_loop(0, nc // 2, chunk_pair, 0)


def _mlstm(xm, op, gates, conv_w, conv_b, wq, wk, wkt, wv, fb, ng):
    B, S, _ = xm.shape
    full = lambda a: pl.BlockSpec(a.shape, lambda b_: (0,) * a.ndim)
    seq = pl.BlockSpec((1, S, ML_WIDTH), lambda b_: (b_, 0, 0))
    return pl.pallas_call(
        functools.partial(_mlstm_kernel, S=S),
        grid=(B,),
        in_specs=[seq, seq, pl.BlockSpec((1,) + gates.shape[1:], lambda b_: (b_, 0, 0, 0)),
                  full(conv_w), full(conv_b), full(wq), full(wk), full(wkt), full(wv), full(fb), full(ng)],
        out_specs=seq,
        out_shape=jax.ShapeDtypeStruct((B, S, ML_WIDTH), BF16),
        scratch_shapes=[pltpu.VMEM((S + 8, ML_WIDTH), F32), pltpu.VMEM((S, ML_WIDTH), BF16),
                        pltpu.VMEM((ML_HEADS, ML_DH, 2 * ML_DH), F32), pltpu.VMEM((ML_HEADS, 8, ML_CHUNK), F32)]
                       + [pltpu.VMEM((S, ML_WIDTH), BF16)] * 3
                       + [pltpu.VMEM((S // ML_CHUNK, ML_HEADS, ML_DH, ML_CHUNK), F32),
                          pltpu.VMEM((S // ML_CHUNK, ML_HEADS, 8, ML_CHUNK), F32),
                          pltpu.VMEM((S // ML_CHUNK, ML_HEADS, ML_CHUNK, ML_CHUNK), F32)],
        compiler_params=_cparams(("arbitrary",)),
        name="mlstm",
    )(xm, op, gates, conv_w, conv_b, wq, wk, wkt, wv, fb, ng)


ROW_TILE = D_MODEL // LANES


def _store_token_tiles(ref, val):
    n = val.shape[0]
    for s in range(ROW_TILE):
        ref[pl.ds(s, n, stride=ROW_TILE), :] = val[:, s * LANES:(s + 1) * LANES]


def _load_token_tiles(ref, n):
    return jnp.concatenate([ref[pl.ds(s, n, stride=ROW_TILE), :] for s in range(ROW_TILE)], axis=1)


def _prep_proj_a(proj_a):
    pa = proj_a.reshape(N_HEADS, HEAD_DIM, D_MODEL)
    return jnp.concatenate([pa, jnp.zeros_like(pa)], axis=1).reshape(QP_W, D_MODEL).astype(BF16)


def _mix_kernel(o_ref, y_ref, gm_ref, x_ref, mod_ref, pa_ref, pb_ref, wo_ref, g2_ref, rwh_ref, rwl_ref, rb_ref,
                x1_ref, h2_ref, e_ref, w_ref, *, ts):
    nt = (((1,), (1,)), ((), ()))
    a = jnp.dot(o_ref[0], pa_ref[...], preferred_element_type=F32)
    bm = jnp.dot(y_ref[0], pb_ref[...], preferred_element_type=F32)
    ga = jax.nn.sigmoid(gm_ref[0, :, :D_MODEL].astype(F32))
    gb = jax.nn.sigmoid(gm_ref[0, :, D_MODEL:].astype(F32))
    mixed = jnp.dot((ga * a + gb * bm).astype(BF16), wo_ref[...], preferred_element_type=F32)
    x1 = x_ref[0] + mod_ref[0, 2:3, :] * mixed
    x1_ref[0] = x1
    ms = jnp.mean(x1 * x1, axis=-1, keepdims=True)
    h2 = x1 * lax.rsqrt(ms + RMS_EPS) * g2_ref[...] * (1.0 + mod_ref[0, 4:5, :]) + mod_ref[0, 3:4, :]
    _store_token_tiles(h2_ref, h2)
    hh, hl = _split_bf16(h2)
    logit = (lax.dot_general(rwh_ref[...], hh, nt, preferred_element_type=F32)
             + lax.dot_general(rwh_ref[...], hl, nt, preferred_element_type=F32)
             + lax.dot_general(rwl_ref[...], hh, nt, preferred_element_type=F32)) + rb_ref[...]
    eio = lax.broadcasted_iota(I32, (N_EXPERTS, ts), 0).astype(F32)
    vals, idxs = [], []
    for _ in range(TOP_K):
        mx = jnp.max(logit, axis=0, keepdims=True)
        ix = jnp.min(jnp.where(logit == mx, eio, 1e9), axis=0, keepdims=True)
        vals.append(mx)
        idxs.append(ix)
        logit = jnp.where(eio == ix, -3e38, logit)
    ex = [jnp.exp(v - vals[0]) for v in vals]
    tot = ex[0] + ex[1] + ex[2] + ex[3]
    e_ref[...] = jnp.concatenate(idxs, axis=0).astype(I32)
    wrows = jnp.concatenate([e_ / tot for e_ in ex] + [jnp.zeros((LANES - TOP_K, ts), F32)], axis=0)
    w_ref[...] = jnp.transpose(wrows)


def _mix(o_nsa, y_ml, gm, x, mod, pa, pb, wo, g2, rwh, rwl, rb, ts=512):
    B, S, _ = x.shape
    T = B * S
    nst = S // ts
    blk = lambda wd: pl.BlockSpec((1, ts, wd), lambda b_, s: (b_, s, 0))
    full = lambda a: pl.BlockSpec(a.shape, lambda b_, s: (0,) * a.ndim)
    return pl.pallas_call(
        functools.partial(_mix_kernel, ts=ts),
        grid=(B, nst),
        in_specs=[blk(QP_W), blk(ML_WIDTH), blk(2 * D_MODEL), blk(D_MODEL),
                  pl.BlockSpec((1, 6, D_MODEL), lambda b_, s: (b_, 0, 0)),
                  full(pa), full(pb), full(wo), full(g2), full(rwh), full(rwl), full(rb)],
        out_specs=[blk(D_MODEL), pl.BlockSpec((ts * ROW_TILE, LANES), lambda b_, s: (b_ * nst + s, 0)),
                   pl.BlockSpec((TOP_K, ts), lambda b_, s: (0, b_ * nst + s)),
                   pl.BlockSpec((ts, LANES), lambda b_, s: (b_ * nst + s, 0))],
        out_shape=[jax.ShapeDtypeStruct((B, S, D_MODEL), F32), jax.ShapeDtypeStruct((T * ROW_TILE, LANES), F32),
                   jax.ShapeDtypeStruct((TOP_K, T), I32), jax.ShapeDtypeStruct((T, LANES), F32)],
        compiler_params=_cparams(("arbitrary", "arbitrary")),
        name="mix_router",
    )(o_nsa, y_ml, gm, x, mod, pa, pb, wo, g2, rwh, rwl, rb)


MOE_TM = 256


def _moe_tiles(T):
    r = T * TOP_K
    return (r + N_EXPERTS * (MOE_TM - 1) + MOE_TM - 1) // MOE_TM


def _route_kernel(e_ref, dest_ref, te_ref, na_ref, *, ntp):
    nrow = e_ref.shape[0]
    eio = lax.broadcasted_iota(I32, (N_EXPERTS, LANES), 0)
    row = lax.broadcasted_iota(I32, (LANES, LANES), 0)
    col = lax.broadcasted_iota(I32, (LANES, LANES), 1)
    triu = jnp.where(row <= col, 1.0, 0.0).astype(BF16)
    ones = jnp.ones((LANES, LANES), BF16)

    ru = 8

    def rank_body(i, counts):
        r0 = pl.multiple_of(i * ru, ru)
        e_blk = e_ref[pl.ds(r0, ru), :]
        ranks = []
        for r in range(ru):
            oh = eio == e_blk[r:r + 1, :]
            ohb = jnp.where(oh, 1.0, 0.0).astype(BF16)
            cum = jnp.dot(ohb, triu, preferred_element_type=F32)
            ranks.append(jnp.sum(jnp.where(oh, cum - 1.0 + counts, 0.0), axis=0, keepdims=True))
            counts = counts + jnp.dot(ohb, ones, preferred_element_type=F32)
        dest_ref[pl.ds(r0, ru), :] = jnp.concatenate(ranks, axis=0).astype(I32)
        return counts

    counts = lax.fori_loop(0, nrow // ru, rank_body, jnp.zeros((N_EXPERTS, LANES), F32)).astype(I32)
    tiles = (counts + (MOE_TM - 1)) // MOE_TM
    incl = tiles
    for s in (1, 2, 4, 8, 16):
        incl = incl + jnp.where(eio >= s, pltpu.roll(incl, s, 0), 0)
    off = (incl - tiles) * MOE_TM

    def dest_body(i, carry):
        r0 = pl.multiple_of(i * ru, ru)
        e_blk = e_ref[pl.ds(r0, ru), :]
        base = [jnp.sum(jnp.where(eio == e_blk[r:r + 1, :], off, 0), axis=0, keepdims=True) for r in range(ru)]
        dest_ref[pl.ds(r0, ru), :] = (dest_ref[pl.ds(r0, ru), :] + jnp.concatenate(base, axis=0)) * ROW_TILE
        return carry

    lax.fori_loop(0, nrow // ru, dest_body, 0)
    ends = jnp.concatenate([incl] * (ntp // LANES), axis=1)
    tix = lax.broadcasted_iota(I32, (N_EXPERTS, ntp), 1)
    te = jnp.sum(jnp.where(ends <= tix, 1, 0), axis=0, keepdims=True)
    te_ref[...] = jnp.minimum(te, N_EXPERTS - 1)
    na_ref[...] = jnp.concatenate([incl, counts, off], axis=0)


def _route(e_t):
    T = e_t.shape[1]
    nrow = TOP_K * T // LANES
    ntp = -(-_moe_tiles(T) // LANES) * LANES
    dest, te, meta = pl.pallas_call(
        functools.partial(_route_kernel, ntp=ntp),
        out_shape=[jax.ShapeDtypeStruct((nrow, LANES), I32), jax.ShapeDtypeStruct((1, ntp), I32),
                   jax.ShapeDtypeStruct((3 * N_EXPERTS, LANES), I32)],
        compiler_params=pltpu.CompilerParams(vmem_limit_bytes=VMEM_LIMIT),
        name="route",
    )(e_t.reshape(nrow, LANES))
    return dest.reshape(TOP_K, T), te.reshape(ntp), meta[:, 0]


MOE_TT = 512


def _row_copy(src_ref, s8, dst_ref, d8, sem):
    src = src_ref.at[pl.ds(pl.multiple_of(s8, ROW_TILE), ROW_TILE), :]
    dst = dst_ref.at[pl.ds(pl.multiple_of(d8, ROW_TILE), ROW_TILE), :]
    return pltpu.make_async_copy(src, dst, sem)


MOE_UNROLL = 4


def _dispatch_kernel(meta_ref, dest_ref, h_ref, xs_ref, zero_sc, sem, zsem, *, n_tiles):
    @pl.when(pl.program_id(0) == 0)
    def _():
        zero_sc[...] = jnp.zeros(zero_sc.shape, F32)

        def pad_range(e):
            lo = meta_ref[2 * N_EXPERTS + e] + meta_ref[N_EXPERTS + e]
            hi = meta_ref[e] * MOE_TM
            return lo, hi

        def fill_expert(e, carry):
            def fill(p, c):
                _row_copy(zero_sc, 0, xs_ref, p * ROW_TILE, zsem).start()
                return c
            return lax.fori_loop(*pad_range(e), fill, carry)

        def drain_expert(e, carry):
            def drain(p, c):
                _row_copy(zero_sc, 0, xs_ref, p * ROW_TILE, zsem).wait()
                return c
            return lax.fori_loop(*pad_range(e), drain, carry)

        lax.fori_loop(0, N_EXPERTS, fill_expert, 0)
        lax.fori_loop(0, N_EXPERTS, drain_expert, 0)

        def tile_fill(t, carry):
            r0 = pl.multiple_of(t * (MOE_TM * ROW_TILE), MOE_TM * ROW_TILE)
            cp = pltpu.make_async_copy(zero_sc, xs_ref.at[pl.ds(r0, MOE_TM * ROW_TILE), :], zsem)
            cp.start()
            cp.wait()
            return carry

        lax.fori_loop(meta_ref[N_EXPERTS - 1], n_tiles, tile_fill, 0)

    def issue(j, carry):
        i0 = j * MOE_UNROLL
        r0 = i0 * ROW_TILE
        for u in range(MOE_UNROLL):
            for k in range(TOP_K):
                _row_copy(h_ref, r0 + u * ROW_TILE, xs_ref, dest_ref[k, i0 + u], sem).start(priority=k % 2)
        return carry

    lax.fori_loop(0, MOE_TT // MOE_UNROLL, issue, 0)
    for k in range(TOP_K):
        pltpu.make_async_copy(h_ref, xs_ref.at[pl.ds(0, MOE_TT * ROW_TILE), :], sem).wait()


def _dispatch(h2, dest, meta, n_tiles):
    T = h2.shape[0] // ROW_TILE
    grid_spec = pltpu.PrefetchScalarGridSpec(
        num_scalar_prefetch=1,
        grid=(T // MOE_TT,),
        in_specs=[pl.BlockSpec((TOP_K, MOE_TT), lambda i, m: (0, i), memory_space=pltpu.SMEM),
                  pl.BlockSpec((MOE_TT * ROW_TILE, LANES), lambda i, m: (i, 0))],
        out_specs=pl.BlockSpec(memory_space=pl.ANY),
        scratch_shapes=[pltpu.VMEM((MOE_TM * ROW_TILE, LANES), F32), pltpu.SemaphoreType.DMA,
                        pltpu.SemaphoreType.DMA],
    )
    return pl.pallas_call(
        functools.partial(_dispatch_kernel, n_tiles=n_tiles),
        grid_spec=grid_spec,
        out_shape=jax.ShapeDtypeStruct((n_tiles * MOE_TM * ROW_TILE, LANES), F32),
        compiler_params=_cparams(("arbitrary",)),
        name="dispatch",
    )(meta, dest, h2)


def _ffn_kernel(te_ref, na_ref, ends_ref, x_ref, wu_hbm, bu_ref, wd_hbm, bd_ref, y_ref,
                wu_f32, wd_f32, wu_sc, wd_sc, slot_ref, sems):
    n_act = na_ref[0]
    rows = MOE_TM * ROW_TILE

    def fetch(expert, slot):
        return (pltpu.make_async_copy(wu_hbm.at[expert], wu_f32.at[slot], sems.at[0, slot]),
                pltpu.make_async_copy(wd_hbm.at[expert], wd_f32.at[slot], sems.at[1, slot]))

    for part in range(FFN_TILES):
        t = pl.program_id(0) * FFN_TILES + part
        e = te_ref[t]
        fresh = jnp.logical_and(t < n_act, jnp.logical_or(t == 0, e != te_ref[jnp.maximum(t - 1, 0)]))
        x_part = x_ref.at[part * rows:(part + 1) * rows, :]
        y_part = y_ref.at[part * rows:(part + 1) * rows, :]

        @pl.when(t == 0)
        def _():
            slot_ref[0] = 0
            for cp in fetch(e, 0):
                cp.start()

        @pl.when(fresh)
        def _():
            slot = slot_ref[0]
            nxt = ends_ref[e]

            @pl.when(nxt < n_act)
            def _():
                for cp in fetch(te_ref[nxt], 1 - slot):
                    cp.start()

            for cp in fetch(e, slot):
                cp.wait()
            wu_sc[...] = wu_f32[slot].astype(BF16)
            wd_sc[...] = wd_f32[slot].astype(BF16)
            slot_ref[0] = 1 - slot

        @pl.when(t < n_act)
        def _():
            x = _load_token_tiles(x_part, MOE_TM).astype(BF16)
            gu = jnp.dot(x, wu_sc[...], preferred_element_type=F32) + bu_ref[e]
            gl = jnp.minimum(gu[:, :D_MODEL], SWIGLU_LIMIT)
            lin = jnp.clip(gu[:, D_MODEL:], -SWIGLU_LIMIT, SWIGLU_LIMIT)
            a = gl * jax.nn.sigmoid(SWIGLU_ALPHA * gl) * (lin + 1.0)
            _store_token_tiles(y_part, jnp.dot(a.astype(BF16), wd_sc[...], preferred_element_type=F32) + bd_ref[e])

        @pl.when(t >= n_act)
        def _():
            y_part[...] = jnp.zeros(y_part.shape, F32)


FFN_TILES = 4


def _ffn(xs, te, na, ends, w_up, b_up, w_down, b_down):
    n_rows = xs.shape[0] // ROW_TILE
    nt = n_rows // MOE_TM
    assert nt % FFN_TILES == 0
    de = w_up.shape[2]
    blk = FFN_TILES * MOE_TM * ROW_TILE
    row_spec = pl.BlockSpec((blk, LANES), lambda i, te_, na_, en_: (i, 0))
    x_spec = pl.BlockSpec((blk, LANES), lambda i, te_, na_, en_: (jnp.minimum(i, (na_[0] - 1) // FFN_TILES), 0))
    grid_spec = pltpu.PrefetchScalarGridSpec(
        num_scalar_prefetch=3,
        grid=(nt // FFN_TILES,),
        in_specs=[x_spec,
                  pl.BlockSpec(memory_space=pl.ANY),
                  pl.BlockSpec((N_EXPERTS, 1, de), lambda i, te_, na_, en_: (0, 0, 0)),
                  pl.BlockSpec(memory_space=pl.ANY),
                  pl.BlockSpec((N_EXPERTS, 1, D_MODEL), lambda i, te_, na_, en_: (0, 0, 0))],
        out_specs=row_spec,
        scratch_shapes=[pltpu.VMEM((2, D_MODEL, de), F32), pltpu.VMEM((2, de // 2, D_MODEL), F32),
                        pltpu.VMEM((D_MODEL, de), BF16), pltpu.VMEM((de // 2, D_MODEL), BF16),
                        pltpu.SMEM((1,), I32), pltpu.SemaphoreType.DMA((2, 2))],
    )
    return pl.pallas_call(
        _ffn_kernel,
        grid_spec=grid_spec,
        out_shape=jax.ShapeDtypeStruct(xs.shape, F32),
        compiler_params=_cparams(("arbitrary",)),
        name="expert_ffn",
    )(te, na, ends, xs, w_up, b_up.reshape(N_EXPERTS, 1, de), w_down, b_down.reshape(N_EXPERTS, 1, D_MODEL))


def _combine_kernel(dest_ref, dnext_ref, y_ref, w_ref, x1_ref, mod_ref, fg_ref, o_ref, buf, sems):
    i = pl.program_id(0)
    n = pl.num_programs(0)
    slot = i % 2
    nslot = 1 - slot
    grp = 32

    def gather(d_ref, tok, tok8, s):
        for k in range(TOP_K):
            _row_copy(y_ref, d_ref[k, tok], buf.at[s, k], tok8, sems.at[s]).start(priority=k % 2)

    def retire(s):
        for k in range(TOP_K):
            pltpu.make_async_copy(y_ref.at[pl.ds(0, MOE_TT * ROW_TILE), :], buf.at[s, k], sems.at[s]).wait()

    @pl.when(i == 0)
    def _():
        def first(tok, carry):
            gather(dest_ref, tok, tok * ROW_TILE, 0)
            return carry
        lax.fori_loop(0, MOE_TT, first, 0)

    retire(slot)
    g2 = mod_ref[0, 5:6, :]
    fg = fg_ref[...]

    def body(j, carry):
        base = pl.multiple_of(j * grp, grp)
        w8 = w_ref[pl.ds(base, grp), :]
        moe = None
        for k in range(TOP_K):
            rows = jnp.concatenate(
                [buf[slot, k, pl.ds(base * ROW_TILE + s, grp, stride=ROW_TILE), :] for s in range(ROW_TILE)], axis=1)
            term = w8[:, k:k + 1] * rows
            moe = term if moe is None else moe + term
        x2 = x1_ref[pl.ds(base, grp), :] + g2 * moe
        ms = jnp.mean(x2 * x2, axis=-1, keepdims=True)
        o_ref[pl.ds(base, grp), :] = x2 * lax.rsqrt(ms + RMS_EPS) * fg
        base8 = base * ROW_TILE
        for u in range(grp):
            gather(dnext_ref, base + u, base8 + u * ROW_TILE, nslot)
        return carry

    lax.fori_loop(0, MOE_TT // grp, body, 0)

    @pl.when(i == n - 1)
    def _():
        retire(nslot)


def _combine(y, dest, wcol, x1, mod, final_g, S):
    T = x1.shape[0]
    per_b = S // MOE_TT
    n = T // MOE_TT
    return pl.pallas_call(
        _combine_kernel,
        grid=(n,),
        in_specs=[pl.BlockSpec((TOP_K, MOE_TT), lambda i: (0, i), memory_space=pltpu.SMEM),
                  pl.BlockSpec((TOP_K, MOE_TT), lambda i: (0, jnp.minimum(i + 1, n - 1)), memory_space=pltpu.SMEM),
                  pl.BlockSpec(memory_space=pl.ANY),
                  pl.BlockSpec((MOE_TT, LANES), lambda i: (i, 0)),
                  pl.BlockSpec((MOE_TT, D_MODEL), lambda i: (i, 0)),
                  pl.BlockSpec((1, 6, D_MODEL), lambda i: (i // per_b, 0, 0)),
                  pl.BlockSpec((1, D_MODEL), lambda i: (0, 0))],
        out_specs=pl.BlockSpec((MOE_TT, D_MODEL), lambda i: (i, 0)),
        out_shape=jax.ShapeDtypeStruct((T, D_MODEL), F32),
        scratch_shapes=[pltpu.VMEM((2, TOP_K, MOE_TT * ROW_TILE, LANES), F32), pltpu.SemaphoreType.DMA((2,))],
        compiler_params=_cparams(("arbitrary",)),
        name="combine",
    )(dest, dest, y, wcol, x1, mod, final_g.reshape(1, D_MODEL))


def _layer(x, c, ada_w, ada_b, norm1_g, w_in, b_in, cmp_pe_k, cmp_w1_k, cmp_w2_k, cmp_pe_v, cmp_w1_v, cmp_w2_v,
           ml_conv_w, ml_conv_b, ml_wq, ml_wk, ml_wv, ml_f_bias, ml_norm_g, proj_a, proj_b, w_out, norm2_g,
           router_w, router_b, exp_w_up, exp_b_up, exp_w_down, exp_b_down, final_g):
    B, S, D = x.shape
    T = B * S
    mod = _adaln(c, ada_w, ada_b).reshape(B, 6, D)
    w, b = _prep_w_in(w_in, b_in)
    q, kc_in, vc_in, kv6, xm, op, gm, small, gates = _inproj(x, mod, norm1_g, w, b)
    prep = [_prep_compress(cmp_pe_k, cmp_w1_k, cmp_w2_k), _prep_compress(cmp_pe_v, cmp_w1_v, cmp_w2_v)]
    wk, pe, w2 = [jnp.stack([prep[0][i], prep[1][i]]) for i in range(3)]
    kc, vc = _compress(kc_in, vc_in, wk, pe, w2)
    o_nsa = _nsa(q, kc, vc, kv6, small, _nsa_consts(S))
    gates = gates.reshape(B, 2 * ML_HEADS, S // ML_CHUNK, ML_CHUNK)
    y_ml = _mlstm(xm, op, gates, ml_conv_w, ml_conv_b.reshape(1, ML_WIDTH), ml_wq.astype(BF16), ml_wk.astype(BF16),
                  jnp.transpose(ml_wk, (0, 2, 1)).astype(BF16), ml_wv.astype(BF16),
                  jnp.broadcast_to(ml_f_bias[:, None], (ML_HEADS, LANES)), ml_norm_g.reshape(1, ML_WIDTH))
    rwt = jnp.transpose(router_w)
    rwh = rwt.astype(BF16)
    rwl = (rwt - rwh.astype(F32)).astype(BF16)
    x1, h2, e_t, wcol = _mix(o_nsa, y_ml, gm, x, mod, _prep_proj_a(proj_a), proj_b.astype(BF16), w_out.astype(BF16),
                             norm2_g.reshape(1, D), rwh, rwl, router_b.reshape(N_EXPERTS, 1))
    dest, te, meta = _route(e_t)
    ends = meta[:N_EXPERTS]
    xs = _dispatch(h2, dest, meta, _moe_tiles(T))
    y = _ffn(xs, te, ends[N_EXPERTS - 1:], ends, exp_w_up, exp_b_up, exp_w_down, exp_b_down)
    out = _combine(y, dest, wcol, x1.reshape(T, D), mod, final_g, S)
    return out.reshape(B, S, D)


def kernel(x, c, ada_w, ada_b, norm1_g, w_in, b_in, cmp_pe_k, cmp_w1_k, cmp_w2_k, cmp_pe_v, cmp_w1_v, cmp_w2_v, ml_conv_w, ml_conv_b, ml_wq, ml_wk, ml_wv, ml_f_bias, ml_norm_g, proj_a, proj_b, w_out, norm2_g, router_w, router_b, exp_w_up, exp_b_up, exp_w_down, exp_b_down, final_g):
    assert ada_w.shape[0] == 1, "one layer: the final RMSNorm is fused into the layer's last kernel"
    layer = (ada_w, ada_b, norm1_g, w_in, b_in, cmp_pe_k, cmp_w1_k, cmp_w2_k, cmp_pe_v, cmp_w1_v, cmp_w2_v,
             ml_conv_w, ml_conv_b, ml_wq, ml_wk, ml_wv, ml_f_bias, ml_norm_g, proj_a, proj_b, w_out, norm2_g,
             router_w, router_b, exp_w_up, exp_b_up, exp_w_down, exp_b_down)
    return _layer(x, c, *[p[0] for p in layer], final_g)
```

```python
import functools
import math

import numpy as np
import jax
import jax.numpy as jnp
from jax import lax
from jax.experimental import pallas as pl
from jax.experimental.pallas import tpu as pltpu

F32 = jnp.float32
BF16 = jnp.bfloat16
I32 = jnp.int32

D_MODEL = 1024
N_HEADS = 8
N_GROUPS = 2
HPG = N_HEADS // N_GROUPS
HEAD_DIM = 64
CMP_BLOCK = 32
CMP_STRIDE = 16
CMP_HIDDEN = 128
SEL_BLOCK = 64
SEL_TOPN = 8
WINDOW = 256
ATTN_SCALE = HEAD_DIM ** -0.5
ML_HEADS = 4
ML_DH = 128
ML_WIDTH = ML_HEADS * ML_DH
ML_CHUNK = 128
CONV_W = 4
N_EXPERTS = 32
TOP_K = 4
SWIGLU_LIMIT = 7.0
SWIGLU_ALPHA = 1.702
RMS_EPS = 1e-5
LN_EPS = 1e-5

LANES = 128
NEG = -1e30
VMEM_LIMIT = 56 * 1024 * 1024

QP_W = N_HEADS * LANES
Q_W = HPG * LANES
C_Q = 0
C_KC = C_Q + Q_W
C_VC = C_KC + LANES
C_KV6 = C_VC + LANES
C_XM = C_KV6 + 6 * LANES
C_OP = C_XM + ML_WIDTH
C_GM = C_OP + ML_WIDTH
C_SM = C_GM + 2 * D_MODEL
IN_W = C_SM + LANES


def _cparams(sem):
    return pltpu.CompilerParams(dimension_semantics=sem, vmem_limit_bytes=VMEM_LIMIT)


def _adaln_kernel(c_ref, w_ref, b_ref, o_ref):
    c = c_ref[...]
    sc = (c * jax.nn.sigmoid(c)).astype(BF16)
    o_ref[...] = jnp.dot(sc, w_ref[...].astype(BF16), preferred_element_type=F32) + b_ref[...]


def _adaln(c, ada_w, ada_b):
    B = c.shape[0]
    tn = 1024
    n = ada_w.shape[1]
    return pl.pallas_call(
        _adaln_kernel,
        grid=(n // tn,),
        in_specs=[pl.BlockSpec((B, D_MODEL), lambda j: (0, 0)),
                  pl.BlockSpec((D_MODEL, tn), lambda j: (0, j)),
                  pl.BlockSpec((1, tn), lambda j: (0, j))],
        out_specs=pl.BlockSpec((B, tn), lambda j: (0, j)),
        out_shape=jax.ShapeDtypeStruct((B, n), F32),
        compiler_params=_cparams(("arbitrary",)),
        name="adaln",
    )(c, ada_w, ada_b.reshape(1, n))


def _prep_w_in(w_in, b_in):
    def pad_q(a):
        lead = a.shape[:-1]
        a = a.reshape(lead + (N_GROUPS, HPG, HEAD_DIM))
        return jnp.concatenate([a[..., 0, :, :], a[..., 1, :, :]], axis=-1).reshape(lead + (Q_W,))

    def v_aug(a, fill):
        f = jnp.full(a.shape[:-1] + (HEAD_DIM,), fill, a.dtype)
        return jnp.concatenate([a[..., :HEAD_DIM], f, a[..., HEAD_DIM:], f], axis=-1)

    def cols(a):
        sl = lambda lo, hi: lax.slice_in_dim(a, lo, hi, axis=-1)
        return dict(q=sl(0, 512), kc=sl(512, 640), vc=sl(640, 768), ks=sl(768, 896), vs=sl(896, 1024),
                    kw=sl(1024, 1152), vw=sl(1152, 1280), gn=sl(1280, 1304), xm=sl(1304, 1816),
                    op=sl(1816, 2328), fi=sl(2328, 2336), gm=sl(2336, 4384))

    def layout(c, fill):
        zs = jnp.zeros(c["q"].shape[:-1] + (LANES - 32,), c["q"].dtype)
        return jnp.concatenate([pad_q(c["q"]), c["kc"], c["vc"], c["ks"], c["kw"], v_aug(c["vs"], fill),
                                v_aug(c["vw"], fill), c["xm"], c["op"], c["gm"], c["gn"], c["fi"], zs], axis=-1)

    w = layout(cols(w_in), 0.0)
    b = layout(cols(b_in), 1.0)
    return w.astype(BF16), b.reshape(1, IN_W)


def _inproj_kernel(x_ref, mod_ref, g_ref, w_ref, b_ref,
                   q_ref, kc_ref, vc_ref, kv_ref, xm_ref, op_ref, gm_ref, sm_ref, gt_ref):
    x = x_ref[0]
    ms = jnp.mean(x * x, axis=-1, keepdims=True)
    y = x * lax.rsqrt(ms + RMS_EPS) * g_ref[...]
    h = (y * (1.0 + mod_ref[0, 1:2, :]) + mod_ref[0, 0:1, :]).astype(BF16)

    def seg(lo, hi):
        return jnp.dot(h, w_ref[:, lo:hi], preferred_element_type=F32) + b_ref[:, lo:hi]

    q = seg(C_Q, C_KC).astype(BF16)
    for j in range(HPG):
        q_ref[0, j] = q[:, j * LANES:(j + 1) * LANES]
    kv = seg(C_KC, C_XM)
    kc_ref[0] = kv[:, 0:LANES]
    vc_ref[0] = kv[:, LANES:2 * LANES]
    kv_ref[0] = kv[:, 2 * LANES:].astype(BF16)
    xm_ref[0] = seg(C_XM, C_OP)
    op_ref[0] = seg(C_OP, C_GM).astype(BF16)
    gs = seg(C_GM, IN_W)
    gm_ref[0] = gs[:, :C_SM - C_GM].astype(BF16)
    sm = gs[:, C_SM - C_GM:]
    sm_ref[0] = sm
    gt_ref[0] = jnp.transpose(sm)[3 * N_HEADS:3 * N_HEADS + 2 * ML_HEADS, :]


def _inproj(x, mod, norm_g, w, b, ts=512):
    B, S, _ = x.shape
    widths = [(LANES, F32), (LANES, F32), (6 * LANES, BF16), (ML_WIDTH, F32),
              (ML_WIDTH, BF16), (2 * D_MODEL, BF16), (LANES, F32)]
    q_spec = pl.BlockSpec((1, HPG, ts, LANES), lambda b_, s: (b_, 0, s, 0))
    gt_spec = pl.BlockSpec((1, 2 * ML_HEADS, ts), lambda b_, s: (b_, 0, s))
    return pl.pallas_call(
        _inproj_kernel,
        grid=(B, S // ts),
        in_specs=[pl.BlockSpec((1, ts, D_MODEL), lambda b_, s: (b_, s, 0)),
                  pl.BlockSpec((1, 6, D_MODEL), lambda b_, s: (b_, 0, 0)),
                  pl.BlockSpec((1, D_MODEL), lambda b_, s: (0, 0)),
                  pl.BlockSpec((D_MODEL, IN_W), lambda b_, s: (0, 0)),
                  pl.BlockSpec((1, IN_W), lambda b_, s: (0, 0))],
        out_specs=([q_spec] + [pl.BlockSpec((1, ts, wd), lambda b_, s: (b_, s, 0)) for wd, _ in widths]
                   + [gt_spec]),
        out_shape=([jax.ShapeDtypeStruct((B, HPG, S, LANES), BF16)]
                   + [jax.ShapeDtypeStruct((B, S, wd), dt) for wd, dt in widths]
                   + [jax.ShapeDtypeStruct((B, 2 * ML_HEADS, S), F32)]),
        compiler_params=_cparams(("arbitrary", "arbitrary")),
        name="inproj",
    )(x, mod, norm_g.reshape(1, D_MODEL), w, b)


def _prep_compress(pe, w1, w2):
    w1r = w1.reshape(2, CMP_STRIDE, HEAD_DIM, CMP_HIDDEN)
    z = jnp.zeros_like(w1r[0])
    cols = []
    for g in range(N_GROUPS):
        for a in range(2):
            parts = [w1r[a] if gg == g else z for gg in range(N_GROUPS)]
            cols.append(jnp.concatenate(parts, axis=1).reshape(CMP_STRIDE * LANES, CMP_HIDDEN))
    wk = jnp.concatenate(cols, axis=1)
    per = pe.reshape(2, CMP_STRIDE, HEAD_DIM)
    pe_rows = jnp.concatenate([per, per], axis=-1).reshape(2, CMP_STRIDE * LANES)
    pe_rows = jnp.concatenate([pe_rows, jnp.zeros((6, CMP_STRIDE * LANES), pe.dtype)], axis=0)
    z2 = jnp.zeros_like(w2)
    w2p = jnp.stack([jnp.concatenate([w2, z2], axis=1), jnp.concatenate([z2, w2], axis=1)])
    return wk.astype(BF16), pe_rows.astype(BF16), w2p.astype(BF16)


def _compress_kernel(ak_ref, av_ref, wk_ref, pe_ref, w2_ref, ok_ref, ov_ref):
    n = ok_ref.shape[1]

    def hidden(i, a_ref, g):
        a = jnp.concatenate([a_ref.at[0][pl.ds(r, n, stride=CMP_STRIDE), :] for r in range(CMP_STRIDE)], axis=1)
        y = jnp.dot(a.astype(BF16), wk_ref[i], preferred_element_type=F32)
        yp = jnp.dot(pe_ref[i], wk_ref[i], preferred_element_type=F32)
        ca, cb = 2 * g * LANES, (2 * g + 1) * LANES
        bias = yp[0:1, ca:ca + LANES] + yp[1:2, cb:cb + LANES]
        hid = y[:, ca:ca + LANES] + pltpu.roll(y[:, cb:cb + LANES], LANES - 1, 0) + bias
        return jax.nn.gelu(hid, approximate=True).astype(BF16)

    ok_ref[0] = (jnp.dot(hidden(0, ak_ref, 0), w2_ref[0, 0], preferred_element_type=F32)
                 + jnp.dot(hidden(0, ak_ref, 1), w2_ref[0, 1], preferred_element_type=F32)).astype(BF16)
    ones_hi = jnp.where(lax.broadcasted_iota(I32, (LANES, LANES), 1) >= HEAD_DIM, 1.0, 0.0)
    for g in range(N_GROUPS):
        v = jnp.dot(hidden(1, av_ref, g), w2_ref[1, 0], preferred_element_type=F32)
        ov_ref[0, g] = (v + ones_hi).astype(BF16)


def _compress(kc_in, vc_in, wk, pe, w2):
    B, S, _ = kc_in.shape
    n = S // CMP_STRIDE
    a_spec = pl.BlockSpec((1, S, LANES), lambda b_: (b_, 0, 0))
    full = lambda shp: pl.BlockSpec(shp, lambda b_: (0,) * len(shp))
    return pl.pallas_call(
        _compress_kernel,
        grid=(B,),
        in_specs=[a_spec, a_spec, full(wk.shape), full(pe.shape), full(w2.shape)],
        out_specs=[pl.BlockSpec((1, n, LANES), lambda b_: (b_, 0, 0)),
                   pl.BlockSpec((1, N_GROUPS, n, LANES), lambda b_: (b_, 0, 0, 0))],
        out_shape=[jax.ShapeDtypeStruct((B, n, LANES), BF16), jax.ShapeDtypeStruct((B, N_GROUPS, n, LANES), BF16)],
        compiler_params=_cparams(("arbitrary",)),
        name="compress",
    )(kc_in, vc_in, wk, pe, w2)


def _nsa_consts(S):
    n_cmp = (S - CMP_BLOCK) // CMP_STRIDE + 1
    n_slc = S // SEL_BLOCK
    cs = np.arange(n_cmp)[:, None] * CMP_STRIDE
    ss = np.arange(n_slc)[None, :] * SEL_BLOCK
    ov = np.clip(np.minimum(cs + CMP_BLOCK, ss + SEL_BLOCK) - np.maximum(cs, ss), 0, None) / CMP_BLOCK
    ovp = np.zeros((LANES, LANES), np.float32)
    ovp[:n_cmp, :n_slc] = ov
    npair = S // (2 * LANES)
    ex = np.zeros((npair, LANES, 2 * LANES), np.float32)
    for kp in range(npair):
        for l in range(2 * LANES):
            ex[kp, (kp * 2 * LANES + l) // SEL_BLOCK, l] = 1.0
    return jnp.asarray(ovp.T, BF16), jnp.asarray(ex, BF16)


M_FLOOR = 0.5 * NEG


def _nsa_kernel(q_ref, kc_ref, vc_ref, ks_ref, kw_ref, vs0_ref, vs1_ref, vw0_ref, vw1_ref, gate_ref, ovt_ref, ex_ref,
                 o_ref, s_sc, mx_sc, mb_sc, acc_sc, out_sc, *, tq):
    t0 = pl.program_id(1) * tq
    n_slc = ks_ref.shape[1] // SEL_BLOCK
    tpos = t0 + lax.broadcasted_iota(I32, (tq, LANES), 0)
    lane = lax.broadcasted_iota(I32, (tq, LANES), 1)
    lane_row = lax.broadcasted_iota(I32, (1, LANES), 1)
    slopes = [2.0 ** -(h + 1) for h in range(N_HEADS)]
    rows = [slice(h * tq, (h + 1) * tq) for h in range(N_HEADS)]
    grp = [h // HPG for h in range(N_HEADS)]
    vs_refs, vw_refs = (vs0_ref, vs1_ref), (vw0_ref, vw1_ref)
    nt = (((1,), (1,)), ((), ()))
    half = [jnp.where((lane < HEAD_DIM) == (g == 0), ATTN_SCALE, 0.0).astype(BF16) for g in range(N_GROUPS)]
    q8 = jnp.concatenate([q_ref[0, h % HPG] * half[grp[h]] for h in range(N_HEADS)], axis=0)
    gates = jax.nn.sigmoid(gate_ref[0])

    def inv_denom(pv):
        return 1.0 / jnp.maximum(jnp.where(lane < HEAD_DIM, pltpu.roll(pv, HEAD_DIM, 1), pv), 1e-30)

    def gated(br, h, pv):
        c = br * N_HEADS + h
        return jnp.broadcast_to(gates[:, c:c + 1], (tq, LANES)) * (pv * inv_denom(pv))

    s_all = lax.dot_general(q8, kc_ref[0], nt, preferred_element_type=F32)
    cmask = jnp.where(lane * CMP_STRIDE + (CMP_BLOCK - 1) <= tpos, 0.0, NEG)
    crel = (lane_row * CMP_STRIDE - t0).astype(F32)
    psum = [jnp.zeros((tq, LANES), F32) for _ in range(N_GROUPS)]
    for h in range(N_HEADS):
        s = s_all[rows[h]] + (cmask + slopes[h] * crel)
        m = jnp.maximum(jnp.max(s, axis=-1, keepdims=True), M_FLOOR)
        p = jnp.exp(s - m)
        pv = jnp.dot(p.astype(BF16), vc_ref[0, grp[h]], preferred_element_type=F32)
        psum[grp[h]] = psum[grp[h]] + p * inv_denom(pv)
        out_sc[rows[h], :] = gated(0, h, pv)

    blk = lax.broadcasted_iota(I32, (n_slc, tq), 0)
    cur = (t0 + lax.broadcasted_iota(I32, (n_slc, tq), 1)) // SEL_BLOCK
    blkf = blk.astype(F32)
    sel = []
    for g in range(N_GROUPS):
        ph, plo = _split_bf16(psum[g])
        imp = (lax.dot_general(ovt_ref[...], ph, nt, preferred_element_type=F32)
               + lax.dot_general(ovt_ref[...], plo, nt, preferred_element_type=F32))[0:n_slc]
        sc = jnp.where((blk == cur) | (blk == 0), -NEG, jnp.where(blk > cur, NEG, imp))
        selt = jnp.zeros((n_slc, tq), F32)
        for _ in range(SEL_TOPN):
            mx = jnp.max(sc, axis=0, keepdims=True)
            idx = jnp.min(jnp.where(sc == mx, blkf, 1e9), axis=0, keepdims=True)
            hit = blkf == idx
            selt = jnp.where(hit, 1.0, selt)
            sc = jnp.where(hit, -3e38, sc)
        selt = jnp.where(blk <= cur, selt, 0.0)
        sel.append(jnp.transpose(jnp.concatenate([selt, jnp.zeros((LANES - n_slc, tq), F32)], axis=0)).astype(BF16))

    KT = NSA_KCHUNK

    def scores(slots, k2, kpos, masks):
        s2 = lax.dot_general(q8, k2, nt, preferred_element_type=F32)
        rel2 = jnp.concatenate([(kp + lane_row - t0).astype(F32) for kp in kpos], axis=1)
        for h in range(N_HEADS):
            s = s2[rows[h]] + (masks[grp[h]] + slopes[h] * rel2)
            for j in range(2):
                s_sc[slots[j], rows[h], :] = s[:, j * LANES:(j + 1) * LANES]
            mx_sc[rows[h], :] = jnp.maximum(mx_sc[rows[h], :], jnp.maximum(s[:, :LANES], s[:, LANES:]))

    def finish_max():
        for h in range(N_HEADS):
            m = jnp.maximum(jnp.max(mx_sc[rows[h], :], axis=-1, keepdims=True), M_FLOOR)
            mb_sc[rows[h], :] = jnp.broadcast_to(m, (tq, LANES))

    def weighted(c, v_c):
        for h in range(N_HEADS):
            s4 = jnp.concatenate([s_sc[KT * c + j, rows[h], :] for j in range(KT)], axis=1)
            mb = mb_sc[rows[h], :]
            p = jnp.exp(s4 - jnp.concatenate([mb] * KT, axis=1)).astype(BF16)
            acc_sc[rows[h], :] += jnp.dot(p, v_c[grp[h]], preferred_element_type=F32)

    mx_sc[...] = jnp.full(mx_sc.shape, NEG, F32)
    acc_sc[...] = jnp.zeros(acc_sc.shape, F32)
    n_chunks = (t0 + tq + KT * LANES - 1) // (KT * LANES)
    tpos2 = t0 + lax.broadcasted_iota(I32, (tq, 2 * LANES), 0)
    lane2 = lax.broadcasted_iota(I32, (tq, 2 * LANES), 1)

    def slc_scores(kp, carry):
        r0 = pl.multiple_of(kp * 2 * LANES, 2 * LANES)
        causal2 = r0 + lane2 <= tpos2
        masks = [jnp.where((jnp.dot(sel[g], ex_ref[kp], preferred_element_type=F32) > 0.5) & causal2, 0.0, NEG)
                 for g in range(N_GROUPS)]
        scores((2 * kp, 2 * kp + 1), ks_ref[0, pl.ds(r0, 2 * LANES), :], (r0, r0 + LANES), masks)
        return carry

    lax.fori_loop(0, (KT // 2) * n_chunks, slc_scores, 0)
    finish_max()

    def slc_weighted(c, carry):
        r0 = pl.multiple_of(c * KT * LANES, KT * LANES)
        weighted(c, [r[0, pl.ds(r0, KT * LANES), :] for r in vs_refs])
        return carry

    lax.fori_loop(0, n_chunks, slc_weighted, 0)
    for h in range(N_HEADS):
        out_sc[rows[h], :] += gated(1, h, acc_sc[rows[h], :])

    mx_sc[...] = jnp.full(mx_sc.shape, NEG, F32)
    acc_sc[...] = jnp.zeros(acc_sc.shape, F32)
    v_tiles = [[] for _ in range(N_GROUPS)]
    for half in range(KT // 2):
        k_parts, kpos, masks = [], [], []
        for r in (2 * half, 2 * half + 1):
            kt = t0 // LANES - WINDOW // LANES + r
            r0 = pl.multiple_of(jnp.maximum(kt, 0) * LANES, LANES)
            spos = kt * LANES + lane
            dist = tpos - spos
            masks.append(jnp.where((dist >= 0) & (dist < WINDOW) & (spos >= 0), 0.0, NEG))
            k_parts.append(kw_ref[0, pl.ds(r0, LANES), :])
            kpos.append(kt * LANES)
            for g in range(N_GROUPS):
                v_tiles[g].append(vw_refs[g][0, pl.ds(r0, LANES), :])
        m2 = jnp.concatenate(masks, axis=1)
        scores((2 * half, 2 * half + 1), jnp.concatenate(k_parts, axis=0), kpos, [m2] * N_GROUPS)
    finish_max()
    weighted(0, [jnp.concatenate(v_tiles[g], axis=0) for g in range(N_GROUPS)])
    for h in range(N_HEADS):
        o = out_sc[rows[h], :] + gated(2, h, acc_sc[rows[h], :])
        o_ref[0, :, h * LANES:(h + 1) * LANES] = o.astype(BF16)


def _nsa(q, kc, vc, kv6, small, consts):
    ovt, ex = consts
    B, _, S, _ = q.shape
    tq = NSA_TQ
    seq = lambda j: pl.BlockSpec((1, S, LANES), lambda b_, i: (b_, 0, j))
    nrow = N_HEADS * tq
    return pl.pallas_call(
        functools.partial(_nsa_kernel, tq=tq),
        grid=(B, S // tq),
        in_specs=[pl.BlockSpec((1, HPG, tq, LANES), lambda b_, i: (b_, 0, i, 0)),
                  pl.BlockSpec((1, LANES, LANES), lambda b_, i: (b_, 0, 0)),
                  pl.BlockSpec((1, N_GROUPS, LANES, LANES), lambda b_, i: (b_, 0, 0, 0)),
                  seq(0), seq(1), seq(2), seq(3), seq(4), seq(5),
                  pl.BlockSpec((1, tq, LANES), lambda b_, i: (b_, i, 0)),
                  pl.BlockSpec(ovt.shape, lambda b_, i: (0, 0)),
                  pl.BlockSpec(ex.shape, lambda b_, i: (0, 0, 0))],
        out_specs=pl.BlockSpec((1, tq, QP_W), lambda b_, i: (b_, i, 0)),
        out_shape=jax.ShapeDtypeStruct((B, S, QP_W), BF16),
        scratch_shapes=[pltpu.VMEM((S // LANES, nrow, LANES), F32)] + [pltpu.VMEM((nrow, LANES), F32)] * 4,
        compiler_params=_cparams(("arbitrary", "arbitrary")),
        name="nsa",
    )(q, kc, vc, kv6, kv6, kv6, kv6, kv6, kv6, small, ovt, ex)


NSA_KCHUNK = 4
NSA_TQ = (NSA_KCHUNK - WINDOW // LANES) * LANES


def _split_bf16(a):
    hi = a.astype(BF16)
    return hi, (a - hi.astype(F32)).astype(BF16)


def _mlstm_kernel(xm_ref, op_ref, gate_ref, cw_ref, cb_ref, wq_ref, wk_ref, wkt_ref, wv_ref, fb_ref, ng_ref,
                  y_ref, xpad_sc, xc_sc, c_sc, m_sc, q_sc, k_sc, v_sc, kt_sc, fr_sc, fc_sc, *, S):
    L = ML_CHUNK
    nt = (((1,), (1,)), ((), ()))
    xpad_sc[0:8, :] = jnp.zeros((8, ML_WIDTH), F32)
    xpad_sc[8:, :] = xm_ref[0]
    rb = 256
    for r in range(S // rb):
        acc = jnp.zeros((rb, ML_WIDTH), F32) + cb_ref[...]
        for j in range(CONV_W):
            o = 8 + r * rb - (CONV_W - 1) + j
            acc = acc + cw_ref[j:j + 1, :] * xpad_sc[o:o + rb, :]
        xc_sc[r * rb:(r + 1) * rb, :] = (acc * jax.nn.sigmoid(acc)).astype(BF16)

    c_sc[...] = jnp.zeros(c_sc.shape, F32)
    m_sc[...] = jnp.zeros(m_sc.shape, F32)
    row = lax.broadcasted_iota(I32, (L, L), 0)
    col = lax.broadcasted_iota(I32, (L, L), 1)
    causal = col <= row
    triu = jnp.where(row <= col, 1.0, 0.0).astype(BF16)
    ones = jnp.ones((L, L), BF16)
    rs = 1.0 / math.sqrt(ML_DH)
    nc = S // L

    for hh in range(ML_HEADS):
        hs = slice(hh * ML_DH, (hh + 1) * ML_DH)
        xc = xc_sc[:, hs]
        q_sc[:, hs] = jnp.dot(xc, wq_ref[hh], preferred_element_type=F32).astype(BF16)
        k_sc[:, hs] = (jnp.dot(xc, wk_ref[hh], preferred_element_type=F32) * rs).astype(BF16)
        v_sc[:, hs] = jnp.dot(xm_ref[0, :, hs].astype(BF16), wv_ref[hh], preferred_element_type=F32).astype(BF16)
        kt = lax.dot_general(wkt_ref[hh], xc, nt, preferred_element_type=F32) * rs
        for c in range(nc):
            kt_sc[c, hh] = kt[:, c * L:(c + 1) * L]

    def gate_sums(c, carry):
        for hh in range(ML_HEADS):
            z = gate_ref[0, ML_HEADS + hh, pl.ds(c, 1), :] + fb_ref[hh:hh + 1, :]
            lf_row = jnp.minimum(z, 0.0) - jnp.log(1.0 + jnp.exp(-jnp.abs(z)))
            rh, rl = _split_bf16(jnp.broadcast_to(lf_row, (8, L)))
            fr_sc[c, hh] = (jnp.dot(rh, triu, preferred_element_type=F32)
                            + jnp.dot(rl, triu, preferred_element_type=F32))
            lh, ll = _split_bf16(jnp.where(causal, jnp.broadcast_to(lf_row, (L, L)), 0.0))
            fc_sc[c, hh] = (jnp.dot(lh, ones, preferred_element_type=F32)
                            + jnp.dot(ll, ones, preferred_element_type=F32))
        return carry

    lax.fori_loop(0, nc, gate_sums, 0)

    def chunk(c, carry):
        r0 = pl.multiple_of(c * L, L)
        for hh in range(ML_HEADS):
            hs = slice(hh * ML_DH, (hh + 1) * ML_DH)
            q = q_sc[pl.ds(r0, L), hs]
            k = k_sc[pl.ds(r0, L), hs]
            kt = kt_sc[c, hh]
            vaug = jnp.concatenate([v_sc[pl.ds(r0, L), hs], ones], axis=1)
            i_row = gate_ref[0, hh, pl.ds(c, 1), :]
            f_r = fr_sc[c, hh, 0:1, :]
            f_c = fc_sc[c, hh]
            m_prev = m_sc[hh, 0:1, :]
            logd = jnp.where(causal, f_c - f_r + i_row, NEG)
            inter = f_c + m_prev
            m_t = jnp.maximum(inter, jnp.max(logd, axis=-1, keepdims=True))
            dm = jnp.where(causal, jnp.exp(logd - m_t), 0.0)
            wi = jnp.exp(inter - m_t)
            qk = (lax.dot_general(q, k, nt, preferred_element_type=F32) * dm).astype(BF16)
            caug = c_sc[hh]
            qc = jnp.dot(q, caug.astype(BF16), preferred_element_type=F32)
            qv = jnp.dot(qk, vaug, preferred_element_type=F32)
            num = wi * qc[:, :L] + qv[:, :L]
            den = wi * qc[:, L:] + qv[:, L:]
            h = num / jnp.maximum(jnp.abs(den), jnp.exp(-m_t))
            mu = jnp.mean(h, axis=-1, keepdims=True)
            hc = h - mu
            var = jnp.mean(hc * hc, axis=-1, keepdims=True)
            hn = hc * lax.rsqrt(var + LN_EPS) * ng_ref[:, hs]
            o = op_ref[0, pl.ds(r0, L), hs].astype(F32)
            y_ref[0, pl.ds(r0, L), hs] = (jax.nn.sigmoid(o) * hn).astype(BF16)

            f_l = f_c[L - 1:L, :]
            logw = f_l - f_r + i_row
            m_new = jnp.maximum(f_l + m_prev, jnp.max(logw, axis=-1, keepdims=True))
            decay = jnp.exp(f_l + m_prev - m_new)
            w_row = jnp.exp(logw - m_new)
            upd = jnp.dot((kt * w_row).astype(BF16), vaug, preferred_element_type=F32)
            c_sc[hh] = jnp.concatenate([decay, decay], axis=1) * caug + upd
            m_sc[hh] = jnp.broadcast_to(m_new, (8, L))
        return carry

    def chunk_pair(j, carry):
        return chunk(2 * j + 1, chunk(2 * j, carry))

    lax.fori_loop(0, nc // 2, chunk_pair, 0)


def _mlstm(xm, op, gates, conv_w, conv_b, wq, wk, wkt, wv, fb, ng):
    B, S, _ = xm.shape
    full = lambda a: pl.BlockSpec(a.shape, lambda b_: (0,) * a.ndim)
    seq = pl.BlockSpec((1, S, ML_WIDTH), lambda b_: (b_, 0, 0))
    return pl.pallas_call(
        functools.partial(_mlstm_kernel, S=S),
        grid=(B,),
        in_specs=[seq, seq, pl.BlockSpec((1,) + gates.shape[1:], lambda b_: (b_, 0, 0, 0)),
                  full(conv_w), full(conv_b), full(wq), full(wk), full(wkt), full(wv), full(fb), full(ng)],
        out_specs=seq,
        out_shape=jax.ShapeDtypeStruct((B, S, ML_WIDTH), BF16),
        scratch_shapes=[pltpu.VMEM((S + 8, ML_WIDTH), F32), pltpu.VMEM((S, ML_WIDTH), BF16),
                        pltpu.VMEM((ML_HEADS, ML_DH, 2 * ML_DH), F32), pltpu.VMEM((ML_HEADS, 8, ML_CHUNK), F32)]
                       + [pltpu.VMEM((S, ML_WIDTH), BF16)] * 3
                       + [pltpu.VMEM((S // ML_CHUNK, ML_HEADS, ML_DH, ML_CHUNK), F32),
                          pltpu.VMEM((S // ML_CHUNK, ML_HEADS, 8, ML_CHUNK), F32),
                          pltpu.VMEM((S // ML_CHUNK, ML_HEADS, ML_CHUNK, ML_CHUNK), F32)],
        compiler_params=_cparams(("arbitrary",)),
        name="mlstm",
    )(xm, op, gates, conv_w, conv_b, wq, wk, wkt, wv, fb, ng)


ROW_TILE = D_MODEL // LANES


def _store_token_tiles(ref, val):
    n = val.shape[0]
    for s in range(ROW_TILE):
        ref[pl.ds(s, n, stride=ROW_TILE), :] = val[:, s * LANES:(s + 1) * LANES]


def _load_token_tiles(ref, n):
    return jnp.concatenate([ref[pl.ds(s, n, stride=ROW_TILE), :] for s in range(ROW_TILE)], axis=1)


def _prep_proj_a(proj_a):
    pa = proj_a.reshape(N_HEADS, HEAD_DIM, D_MODEL)
    return jnp.concatenate([pa, jnp.zeros_like(pa)], axis=1).reshape(QP_W, D_MODEL).astype(BF16)


def _mix_kernel(o_ref, y_ref, gm_ref, x_ref, mod_ref, pa_ref, pb_ref, wo_ref, g2_ref, rwh_ref, rwl_ref, rb_ref,
                x1_ref, h2_ref, e_ref, w_ref, *, ts):
    nt = (((1,), (1,)), ((), ()))
    a = jnp.dot(o_ref[0], pa_ref[...], preferred_element_type=F32)
    bm = jnp.dot(y_ref[0], pb_ref[...], preferred_element_type=F32)
    ga = jax.nn.sigmoid(gm_ref[0, :, :D_MODEL].astype(F32))
    gb = jax.nn.sigmoid(gm_ref[0, :, D_MODEL:].astype(F32))
    mixed = jnp.dot((ga * a + gb * bm).astype(BF16), wo_ref[...], preferred_element_type=F32)
    x1 = x_ref[0] + mod_ref[0, 2:3, :] * mixed
    x1_ref[0] = x1
    ms = jnp.mean(x1 * x1, axis=-1, keepdims=True)
    h2 = x1 * lax.rsqrt(ms + RMS_EPS) * g2_ref[...] * (1.0 + mod_ref[0, 4:5, :]) + mod_ref[0, 3:4, :]
    _store_token_tiles(h2_ref, h2)
    hh, hl = _split_bf16(h2)
    logit = (lax.dot_general(rwh_ref[...], hh, nt, preferred_element_type=F32)
             + lax.dot_general(rwh_ref[...], hl, nt, preferred_element_type=F32)
             + lax.dot_general(rwl_ref[...], hh, nt, preferred_element_type=F32)) + rb_ref[...]
    eio = lax.broadcasted_iota(I32, (N_EXPERTS, ts), 0).astype(F32)
    vals, idxs = [], []
    for _ in range(TOP_K):
        mx = jnp.max(logit, axis=0, keepdims=True)
        ix = jnp.min(jnp.where(logit == mx, eio, 1e9), axis=0, keepdims=True)
        vals.append(mx)
        idxs.append(ix)
        logit = jnp.where(eio == ix, -3e38, logit)
    ex = [jnp.exp(v - vals[0]) for v in vals]
    tot = ex[0] + ex[1] + ex[2] + ex[3]
    e_ref[...] = jnp.concatenate(idxs, axis=0).astype(I32)
    wrows = jnp.concatenate([e_ / tot for e_ in ex] + [jnp.zeros((LANES - TOP_K, ts), F32)], axis=0)
    w_ref[...] = jnp.transpose(wrows)


def _mix(o_nsa, y_ml, gm, x, mod, pa, pb, wo, g2, rwh, rwl, rb, ts=512):
    B, S, _ = x.shape
    T = B * S
    nst = S // ts
    blk = lambda wd: pl.BlockSpec((1, ts, wd), lambda b_, s: (b_, s, 0))
    full = lambda a: pl.BlockSpec(a.shape, lambda b_, s: (0,) * a.ndim)
    return pl.pallas_call(
        functools.partial(_mix_kernel, ts=ts),
        grid=(B, nst),
        in_specs=[blk(QP_W), blk(ML_WIDTH), blk(2 * D_MODEL), blk(D_MODEL),
                  pl.BlockSpec((1, 6, D_MODEL), lambda b_, s: (b_, 0, 0)),
                  full(pa), full(pb), full(wo), full(g2), full(rwh), full(rwl), full(rb)],
        out_specs=[blk(D_MODEL), pl.BlockSpec((ts * ROW_TILE, LANES), lambda b_, s: (b_ * nst + s, 0)),
                   pl.BlockSpec((TOP_K, ts), lambda b_, s: (0, b_ * nst + s)),
                   pl.BlockSpec((ts, LANES), lambda b_, s: (b_ * nst + s, 0))],
        out_shape=[jax.ShapeDtypeStruct((B, S, D_MODEL), F32), jax.ShapeDtypeStruct((T * ROW_TILE, LANES), F32),
                   jax.ShapeDtypeStruct((TOP_K, T), I32), jax.ShapeDtypeStruct((T, LANES), F32)],
        compiler_params=_cparams(("arbitrary", "arbitrary")),
        name="mix_router",
    )(o_nsa, y_ml, gm, x, mod, pa, pb, wo, g2, rwh, rwl, rb)


MOE_TM = 256


def _moe_tiles(T):
    r = T * TOP_K
    return (r + N_EXPERTS * (MOE_TM - 1) + MOE_TM - 1) // MOE_TM


def _route_kernel(e_ref, dest_ref, te_ref, na_ref, *, ntp):
    nrow = e_ref.shape[0]
    eio = lax.broadcasted_iota(I32, (N_EXPERTS, LANES), 0)
    row = lax.broadcasted_iota(I32, (LANES, LANES), 0)
    col = lax.broadcasted_iota(I32, (LANES, LANES), 1)
    triu = jnp.where(row <= col, 1.0, 0.0).astype(BF16)
    ones = jnp.ones((LANES, LANES), BF16)

    ru = 8

    def rank_body(i, counts):
        r0 = pl.multiple_of(i * ru, ru)
        e_blk = e_ref[pl.ds(r0, ru), :]
        ranks = []
        for r in range(ru):
            oh = eio == e_blk[r:r + 1, :]
            ohb = jnp.where(oh, 1.0, 0.0).astype(BF16)
            cum = jnp.dot(ohb, triu, preferred_element_type=F32)
            ranks.append(jnp.sum(jnp.where(oh, cum - 1.0 + counts, 0.0), axis=0, keepdims=True))
            counts = counts + jnp.dot(ohb, ones, preferred_element_type=F32)
        dest_ref[pl.ds(r0, ru), :] = jnp.concatenate(ranks, axis=0).astype(I32)
        return counts

    counts = lax.fori_loop(0, nrow // ru, rank_body, jnp.zeros((N_EXPERTS, LANES), F32)).astype(I32)
    tiles = (counts + (MOE_TM - 1)) // MOE_TM
    incl = tiles
    for s in (1, 2, 4, 8, 16):
        incl = incl + jnp.where(eio >= s, pltpu.roll(incl, s, 0), 0)
    off = (incl - tiles) * MOE_TM

    def dest_body(i, carry):
        r0 = pl.multiple_of(i * ru, ru)
        e_blk = e_ref[pl.ds(r0, ru), :]
        base = [jnp.sum(jnp.where(eio == e_blk[r:r + 1, :], off, 0), axis=0, keepdims=True) for r in range(ru)]
        dest_ref[pl.ds(r0, ru), :] = (dest_ref[pl.ds(r0, ru), :] + jnp.concatenate(base, axis=0)) * ROW_TILE
        return carry

    lax.fori_loop(0, nrow // ru, dest_body, 0)
    ends = jnp.concatenate([incl] * (ntp // LANES), axis=1)
    tix = lax.broadcasted_iota(I32, (N_EXPERTS, ntp), 1)
    te = jnp.sum(jnp.where(ends <= tix, 1, 0), axis=0, keepdims=True)
    te_ref[...] = jnp.minimum(te, N_EXPERTS - 1)
    na_ref[...] = jnp.concatenate([incl, counts, off], axis=0)


def _route(e_t):
    T = e_t.shape[1]
    nrow = TOP_K * T // LANES
    ntp = -(-_moe_tiles(T) // LANES) * LANES
    dest, te, meta = pl.pallas_call(
        functools.partial(_route_kernel, ntp=ntp),
        out_shape=[jax.ShapeDtypeStruct((nrow, LANES), I32), jax.ShapeDtypeStruct((1, ntp), I32),
                   jax.ShapeDtypeStruct((3 * N_EXPERTS, LANES), I32)],
        compiler_params=pltpu.CompilerParams(vmem_limit_bytes=VMEM_LIMIT),
        name="route",
    )(e_t.reshape(nrow, LANES))
    return dest.reshape(TOP_K, T), te.reshape(ntp), meta[:, 0]


MOE_TT = 512


def _row_copy(src_ref, s8, dst_ref, d8, sem):
    src = src_ref.at[pl.ds(pl.multiple_of(s8, ROW_TILE), ROW_TILE), :]
    dst = dst_ref.at[pl.ds(pl.multiple_of(d8, ROW_TILE), ROW_TILE), :]
    return pltpu.make_async_copy(src, dst, sem)


MOE_UNROLL = 4


def _dispatch_kernel(meta_ref, dest_ref, h_ref, xs_ref, zero_sc, sem, zsem, *, n_tiles):
    @pl.when(pl.program_id(0) == 0)
    def _():
        zero_sc[...] = jnp.zeros(zero_sc.shape, F32)

        def pad_range(e):
            lo = meta_ref[2 * N_EXPERTS + e] + meta_ref[N_EXPERTS + e]
            hi = meta_ref[e] * MOE_TM
            return lo, hi

        def fill_expert(e, carry):
            def fill(p, c):
                _row_copy(zero_sc, 0, xs_ref, p * ROW_TILE, zsem).start()
                return c
            return lax.fori_loop(*pad_range(e), fill, carry)

        def drain_expert(e, carry):
            def drain(p, c):
                _row_copy(zero_sc, 0, xs_ref, p * ROW_TILE, zsem).wait()
                return c
            return lax.fori_loop(*pad_range(e), drain, carry)

        lax.fori_loop(0, N_EXPERTS, fill_expert, 0)
        lax.fori_loop(0, N_EXPERTS, drain_expert, 0)

        def tile_fill(t, carry):
            r0 = pl.multiple_of(t * (MOE_TM * ROW_TILE), MOE_TM * ROW_TILE)
            cp = pltpu.make_async_copy(zero_sc, xs_ref.at[pl.ds(r0, MOE_TM * ROW_TILE), :], zsem)
            cp.start()
            cp.wait()
            return carry

        lax.fori_loop(meta_ref[N_EXPERTS - 1], n_tiles, tile_fill, 0)

    def issue(j, carry):
        i0 = j * MOE_UNROLL
        r0 = i0 * ROW_TILE
        for u in range(MOE_UNROLL):
            for k in range(TOP_K):
                _row_copy(h_ref, r0 + u * ROW_TILE, xs_ref, dest_ref[k, i0 + u], sem).start(priority=k % 2)
        return carry

    lax.fori_loop(0, MOE_TT // MOE_UNROLL, issue, 0)
    for k in range(TOP_K):
        pltpu.make_async_copy(h_ref, xs_ref.at[pl.ds(0, MOE_TT * ROW_TILE), :], sem).wait()


def _dispatch(h2, dest, meta, n_tiles):
    T = h2.shape[0] // ROW_TILE
    grid_spec = pltpu.PrefetchScalarGridSpec(
        num_scalar_prefetch=1,
        grid=(T // MOE_TT,),
        in_specs=[pl.BlockSpec((TOP_K, MOE_TT), lambda i, m: (0, i), memory_space=pltpu.SMEM),
                  pl.BlockSpec((MOE_TT * ROW_TILE, LANES), lambda i, m: (i, 0))],
        out_specs=pl.BlockSpec(memory_space=pl.ANY),
        scratch_shapes=[pltpu.VMEM((MOE_TM * ROW_TILE, LANES), F32), pltpu.SemaphoreType.DMA,
                        pltpu.SemaphoreType.DMA],
    )
    return pl.pallas_call(
        functools.partial(_dispatch_kernel, n_tiles=n_tiles),
        grid_spec=grid_spec,
        out_shape=jax.ShapeDtypeStruct((n_tiles * MOE_TM * ROW_TILE, LANES), F32),
        compiler_params=_cparams(("arbitrary",)),
        name="dispatch",
    )(meta, dest, h2)


def _ffn_kernel(te_ref, na_ref, ends_ref, x_ref, wu_hbm, bu_ref, wd_hbm, bd_ref, y_ref,
                wu_f32, wd_f32, wu_sc, wd_sc, slot_ref, sems):
    n_act = na_ref[0]
    rows = MOE_TM * ROW_TILE

    def fetch(expert, slot):
        return (pltpu.make_async_copy(wu_hbm.at[expert], wu_f32.at[slot], sems.at[0, slot]),
                pltpu.make_async_copy(wd_hbm.at[expert], wd_f32.at[slot], sems.at[1, slot]))

    for part in range(FFN_TILES):
        t = pl.program_id(0) * FFN_TILES + part
        e = te_ref[t]
        fresh = jnp.logical_and(t < n_act, jnp.logical_or(t == 0, e != te_ref[jnp.maximum(t - 1, 0)]))
        x_part = x_ref.at[part * rows:(part + 1) * rows, :]
        y_part = y_ref.at[part * rows:(part + 1) * rows, :]

        @pl.when(t == 0)
        def _():
            slot_ref[0] = 0
            for cp in fetch(e, 0):
                cp.start()

        @pl.when(fresh)
        def _():
            slot = slot_ref[0]
            nxt = ends_ref[e]

            @pl.when(nxt < n_act)
            def _():
                for cp in fetch(te_ref[nxt], 1 - slot):
                    cp.start()

            for cp in fetch(e, slot):
                cp.wait()
            wu_sc[...] = wu_f32[slot].astype(BF16)
            wd_sc[...] = wd_f32[slot].astype(BF16)
            slot_ref[0] = 1 - slot

        @pl.when(t < n_act)
        def _():
            x = _load_token_tiles(x_part, MOE_TM).astype(BF16)
            gu = jnp.dot(x, wu_sc[...], preferred_element_type=F32) + bu_ref[e]
            gl = jnp.minimum(gu[:, :D_MODEL], SWIGLU_LIMIT)
            lin = jnp.clip(gu[:, D_MODEL:], -SWIGLU_LIMIT, SWIGLU_LIMIT)
            a = gl * jax.nn.sigmoid(SWIGLU_ALPHA * gl) * (lin + 1.0)
            _store_token_tiles(y_part, jnp.dot(a.astype(BF16), wd_sc[...], preferred_element_type=F32) + bd_ref[e])

        @pl.when(t >= n_act)
        def _():
            y_part[...] = jnp.zeros(y_part.shape, F32)


FFN_TILES = 4


def _ffn(xs, te, na, ends, w_up, b_up, w_down, b_down):
    n_rows = xs.shape[0] // ROW_TILE
    nt = n_rows // MOE_TM
    assert nt % FFN_TILES == 0
    de = w_up.shape[2]
    blk = FFN_TILES * MOE_TM * ROW_TILE
    row_spec = pl.BlockSpec((blk, LANES), lambda i, te_, na_, en_: (i, 0))
    x_spec = pl.BlockSpec((blk, LANES), lambda i, te_, na_, en_: (jnp.minimum(i, (na_[0] - 1) // FFN_TILES), 0))
    grid_spec = pltpu.PrefetchScalarGridSpec(
        num_scalar_prefetch=3,
        grid=(nt // FFN_TILES,),
        in_specs=[x_spec,
                  pl.BlockSpec(memory_space=pl.ANY),
                  pl.BlockSpec((N_EXPERTS, 1, de), lambda i, te_, na_, en_: (0, 0, 0)),
                  pl.BlockSpec(memory_space=pl.ANY),
                  pl.BlockSpec((N_EXPERTS, 1, D_MODEL), lambda i, te_, na_, en_: (0, 0, 0))],
        out_specs=row_spec,
        scratch_shapes=[pltpu.VMEM((2, D_MODEL, de), F32), pltpu.VMEM((2, de // 2, D_MODEL), F32),
                        pltpu.VMEM((D_MODEL, de), BF16), pltpu.VMEM((de // 2, D_MODEL), BF16),
                        pltpu.SMEM((1,), I32), pltpu.SemaphoreType.DMA((2, 2))],
    )
    return pl.pallas_call(
        _ffn_kernel,
        grid_spec=grid_spec,
        out_shape=jax.ShapeDtypeStruct(xs.shape, F32),
        compiler_params=_cparams(("arbitrary",)),
        name="expert_ffn",
    )(te, na, ends, xs, w_up, b_up.reshape(N_EXPERTS, 1, de), w_down, b_down.reshape(N_EXPERTS, 1, D_MODEL))


def _combine_kernel(dest_ref, dnext_ref, y_ref, w_ref, x1_ref, mod_ref, fg_ref, o_ref, buf, sems):
    i = pl.program_id(0)
    n = pl.num_programs(0)
    slot = i % 2
    nslot = 1 - slot
    grp = 32

    def gather(d_ref, tok, tok8, s):
        for k in range(TOP_K):
            _row_copy(y_ref, d_ref[k, tok], buf.at[s, k], tok8, sems.at[s]).start(priority=k % 2)

    def retire(s):
        for k in range(TOP_K):
            pltpu.make_async_copy(y_ref.at[pl.ds(0, MOE_TT * ROW_TILE), :], buf.at[s, k], sems.at[s]).wait()

    @pl.when(i == 0)
    def _():
        def first(tok, carry):
            gather(dest_ref, tok, tok * ROW_TILE, 0)
            return carry
        lax.fori_loop(0, MOE_TT, first, 0)

    retire(slot)
    g2 = mod_ref[0, 5:6, :]
    fg = fg_ref[...]

    def body(j, carry):
        base = pl.multiple_of(j * grp, grp)
        w8 = w_ref[pl.ds(base, grp), :]
        moe = None
        for k in range(TOP_K):
            rows = jnp.concatenate(
                [buf[slot, k, pl.ds(base * ROW_TILE + s, grp, stride=ROW_TILE), :] for s in range(ROW_TILE)], axis=1)
            term = w8[:, k:k + 1] * rows
            moe = term if moe is None else moe + term
        x2 = x1_ref[pl.ds(base, grp), :] + g2 * moe
        ms = jnp.mean(x2 * x2, axis=-1, keepdims=True)
        o_ref[pl.ds(base, grp), :] = x2 * lax.rsqrt(ms + RMS_EPS) * fg
        base8 = base * ROW_TILE
        for u in range(grp):
            gather(dnext_ref, base + u, base8 + u * ROW_TILE, nslot)
        return carry

    lax.fori_loop(0, MOE_TT // grp, body, 0)

    @pl.when(i == n - 1)
    def _():
        retire(nslot)


def _combine(y, dest, wcol, x1, mod, final_g, S):
    T = x1.shape[0]
    per_b = S // MOE_TT
    n = T // MOE_TT
    return pl.pallas_call(
        _combine_kernel,
        grid=(n,),
        in_specs=[pl.BlockSpec((TOP_K, MOE_TT), lambda i: (0, i), memory_space=pltpu.SMEM),
                  pl.BlockSpec((TOP_K, MOE_TT), lambda i: (0, jnp.minimum(i + 1, n - 1)), memory_space=pltpu.SMEM),
                  pl.BlockSpec(memory_space=pl.ANY),
                  pl.BlockSpec((MOE_TT, LANES), lambda i: (i, 0)),
                  pl.BlockSpec((MOE_TT, D_MODEL), lambda i: (i, 0)),
                  pl.BlockSpec((1, 6, D_MODEL), lambda i: (i // per_b, 0, 0)),
                  pl.BlockSpec((1, D_MODEL), lambda i: (0, 0))],
        out_specs=pl.BlockSpec((MOE_TT, D_MODEL), lambda i: (i, 0)),
        out_shape=jax.ShapeDtypeStruct((T, D_MODEL), F32),
        scratch_shapes=[pltpu.VMEM((2, TOP_K, MOE_TT * ROW_TILE, LANES), F32), pltpu.SemaphoreType.DMA((2,))],
        compiler_params=_cparams(("arbitrary",)),
        name="combine",
    )(dest, dest, y, wcol, x1, mod, final_g.reshape(1, D_MODEL))


def _layer(x, c, ada_w, ada_b, norm1_g, w_in, b_in, cmp_pe_k, cmp_w1_k, cmp_w2_k, cmp_pe_v, cmp_w1_v, cmp_w2_v,
           ml_conv_w, ml_conv_b, ml_wq, ml_wk, ml_wv, ml_f_bias, ml_norm_g, proj_a, proj_b, w_out, norm2_g,
           router_w, router_b, exp_w_up, exp_b_up, exp_w_down, exp_b_down, final_g):
    B, S, D = x.shape
    T = B * S
    mod = _adaln(c, ada_w, ada_b).reshape(B, 6, D)
    w, b = _prep_w_in(w_in, b_in)
    q, kc_in, vc_in, kv6, xm, op, gm, small, gates = _inproj(x, mod, norm1_g, w, b)
    prep = [_prep_compress(cmp_pe_k, cmp_w1_k, cmp_w2_k), _prep_compress(cmp_pe_v, cmp_w1_v, cmp_w2_v)]
    wk, pe, w2 = [jnp.stack([prep[0][i], prep[1][i]]) for i in range(3)]
    kc, vc = _compress(kc_in, vc_in, wk, pe, w2)
    o_nsa = _nsa(q, kc, vc, kv6, small, _nsa_consts(S))
    gates = gates.reshape(B, 2 * ML_HEADS, S // ML_CHUNK, ML_CHUNK)
    y_ml = _mlstm(xm, op, gates, ml_conv_w, ml_conv_b.reshape(1, ML_WIDTH), ml_wq.astype(BF16), ml_wk.astype(BF16),
                  jnp.transpose(ml_wk, (0, 2, 1)).astype(BF16), ml_wv.astype(BF16),
                  jnp.broadcast_to(ml_f_bias[:, None], (ML_HEADS, LANES)), ml_norm_g.reshape(1, ML_WIDTH))
    rwt = jnp.transpose(router_w)
    rwh = rwt.astype(BF16)
    rwl = (rwt - rwh.astype(F32)).astype(BF16)
    x1, h2, e_t, wcol = _mix(o_nsa, y_ml, gm, x, mod, _prep_proj_a(proj_a), proj_b.astype(BF16), w_out.astype(BF16),
                             norm2_g.reshape(1, D), rwh, rwl, router_b.reshape(N_EXPERTS, 1))
    dest, te, meta = _route(e_t)
    ends = meta[:N_EXPERTS]
    xs = _dispatch(h2, dest, meta, _moe_tiles(T))
    y = _ffn(xs, te, ends[N_EXPERTS - 1:], ends, exp_w_up, exp_b_up, exp_w_down, exp_b_down)
    out = _combine(y, dest, wcol, x1.reshape(T, D), mod, final_g, S)
    return out.reshape(B, S, D)


def kernel(x, c, ada_w, ada_b, norm1_g, w_in, b_in, cmp_pe_k, cmp_w1_k, cmp_w2_k, cmp_pe_v, cmp_w1_v, cmp_w2_v, ml_conv_w, ml_conv_b, ml_wq, ml_wk, ml_wv, ml_f_bias, ml_norm_g, proj_a, proj_b, w_out, norm2_g, router_w, router_b, exp_w_up, exp_b_up, exp_w_down, exp_b_down, final_g):
    assert ada_w.shape[0] == 1, "one layer: the final RMSNorm is fused into the layer's last kernel"
    layer = (ada_w, ada_b, norm1_g, w_in, b_in, cmp_pe_k, cmp_w1_k, cmp_w2_k, cmp_pe_v, cmp_w1_v, cmp_w2_v,
             ml_conv_w, ml_conv_b, ml_wq, ml_wk, ml_wv, ml_f_bias, ml_norm_g, proj_a, proj_b, w_out, norm2_g,
             router_w, router_b, exp_w_up, exp_b_up, exp_w_down, exp_b_down)
    return _layer(x, c, *[p[0] for p in layer], final_g)
```

```python
import functools
import math

import numpy as np
import jax
import jax.numpy as jnp
from jax import lax
from jax.experimental import pallas as pl
from jax.experimental.pallas import tpu as pltpu

F32 = jnp.float32
BF16 = jnp.bfloat16
I32 = jnp.int32

D_MODEL = 1024
N_HEADS = 8
N_GROUPS = 2
HPG = N_HEADS // N_GROUPS
HEAD_DIM = 64
CMP_BLOCK = 32
CMP_STRIDE = 16
CMP_HIDDEN = 128
SEL_BLOCK = 64
SEL_TOPN = 8
WINDOW = 256
ATTN_SCALE = HEAD_DIM ** -0.5
ML_HEADS = 4
ML_DH = 128
ML_WIDTH = ML_HEADS * ML_DH
ML_CHUNK = 128
CONV_W = 4
N_EXPERTS = 32
TOP_K = 4
SWIGLU_LIMIT = 7.0
SWIGLU_ALPHA = 1.702
RMS_EPS = 1e-5
LN_EPS = 1e-5

LANES = 128
NEG = -1e30
VMEM_LIMIT = 56 * 1024 * 1024

QP_W = N_HEADS * LANES
Q_W = HPG * LANES
C_Q = 0
C_KC = C_Q + Q_W
C_VC = C_KC + LANES
C_KV6 = C_VC + LANES
C_XM = C_KV6 + 6 * LANES
C_OP = C_XM + ML_WIDTH
C_GM = C_OP + ML_WIDTH
C_SM = C_GM + 2 * D_MODEL
IN_W = C_SM + LANES


def _cparams(sem):
    return pltpu.CompilerParams(dimension_semantics=sem, vmem_limit_bytes=VMEM_LIMIT)


def _adaln_kernel(c_ref, w_ref, b_ref, o_ref):
    c = c_ref[...]
    sc = (c * jax.nn.sigmoid(c)).astype(BF16)
    o_ref[...] = jnp.dot(sc, w_ref[...].astype(BF16), preferred_element_type=F32) + b_ref[...]


def _adaln(c, ada_w, ada_b):
    B = c.shape[0]
    tn = 1024
    n = ada_w.shape[1]
    return pl.pallas_call(
        _adaln_kernel,
        grid=(n // tn,),
        in_specs=[pl.BlockSpec((B, D_MODEL), lambda j: (0, 0)),
                  pl.BlockSpec((D_MODEL, tn), lambda j: (0, j)),
                  pl.BlockSpec((1, tn), lambda j: (0, j))],
        out_specs=pl.BlockSpec((B, tn), lambda j: (0, j)),
        out_shape=jax.ShapeDtypeStruct((B, n), F32),
        compiler_params=_cparams(("arbitrary",)),
        name="adaln",
    )(c, ada_w, ada_b.reshape(1, n))


def _prep_w_in(w_in, b_in):
    def pad_q(a):
        lead = a.shape[:-1]
        a = a.reshape(lead + (N_GROUPS, HPG, HEAD_DIM))
        return jnp.concatenate([a[..., 0, :, :], a[..., 1, :, :]], axis=-1).reshape(lead + (Q_W,))

    def v_aug(a, fill):
        f = jnp.full(a.shape[:-1] + (HEAD_DIM,), fill, a.dtype)
        return jnp.concatenate([a[..., :HEAD_DIM], f, a[..., HEAD_DIM:], f], axis=-1)

    def cols(a):
        sl = lambda lo, hi: lax.slice_in_dim(a, lo, hi, axis=-1)
        return dict(q=sl(0, 512), kc=sl(512, 640), vc=sl(640, 768), ks=sl(768, 896), vs=sl(896, 1024),
                    kw=sl(1024, 1152), vw=sl(1152, 1280), gn=sl(1280, 1304), xm=sl(1304, 1816),
                    op=sl(1816, 2328), fi=sl(2328, 2336), gm=sl(2336, 4384))

    def layout(c, fill):
        zs = jnp.zeros(c["q"].shape[:-1] + (LANES - 32,), c["q"].dtype)
        return jnp.concatenate([pad_q(c["q"]), c["kc"], c["vc"], c["ks"], c["kw"], v_aug(c["vs"], fill),
                                v_aug(c["vw"], fill), c["xm"], c["op"], c["gm"], c["gn"], c["fi"], zs], axis=-1)

    w = layout(cols(w_in), 0.0)
    b = layout(cols(b_in), 1.0)
    return w.astype(BF16), b.reshape(1, IN_W)


def _inproj_kernel(x_ref, mod_ref, g_ref, w_ref, b_ref,
                   q_ref, kc_ref, vc_ref, kv_ref, xm_ref, op_ref, gm_ref, sm_ref, gt_ref):
    x = x_ref[0]
    ms = jnp.mean(x * x, axis=-1, keepdims=True)
    y = x * lax.rsqrt(ms + RMS_EPS) * g_ref[...]
    h = (y * (1.0 + mod_ref[0, 1:2, :]) + mod_ref[0, 0:1, :]).astype(BF16)

    def seg(lo, hi):
        return jnp.dot(h, w_ref[:, lo:hi], preferred_element_type=F32) + b_ref[:, lo:hi]

    q = seg(C_Q, C_KC).astype(BF16)
    for j in range(HPG):
        q_ref[0, j] = q[:, j * LANES:(j + 1) * LANES]
    kv = seg(C_KC, C_XM)
    kc_ref[0] = kv[:, 0:LANES]
    vc_ref[0] = kv[:, LANES:2 * LANES]
    kv_ref[0] = kv[:, 2 * LANES:].astype(BF16)
    xm_ref[0] = seg(C_XM, C_OP)
    op_ref[0] = seg(C_OP, C_GM).astype(BF16)
    gs = seg(C_GM, IN_W)
    gm_ref[0] = gs[:, :C_SM - C_GM].astype(BF16)
    sm = gs[:, C_SM - C_GM:]
    sm_ref[0] = sm
    gt_ref[0] = jnp.transpose(sm)[3 * N_HEADS:3 * N_HEADS + 2 * ML_HEADS, :]


def _inproj(x, mod, norm_g, w, b, ts=512):
    B, S, _ = x.shape
    widths = [(LANES, F32), (LANES, F32), (6 * LANES, BF16), (ML_WIDTH, F32),
              (ML_WIDTH, BF16), (2 * D_MODEL, BF16), (LANES, F32)]
    q_spec = pl.BlockSpec((1, HPG, ts, LANES), lambda b_, s: (b_, 0, s, 0))
    gt_spec = pl.BlockSpec((1, 2 * ML_HEADS, ts), lambda b_, s: (b_, 0, s))
    return pl.pallas_call(
        _inproj_kernel,
        grid=(B, S // ts),
        in_specs=[pl.BlockSpec((1, ts, D_MODEL), lambda b_, s: (b_, s, 0)),
                  pl.BlockSpec((1, 6, D_MODEL), lambda b_, s: (b_, 0, 0)),
                  pl.BlockSpec((1, D_MODEL), lambda b_, s: (0, 0)),
                  pl.BlockSpec((D_MODEL, IN_W), lambda b_, s: (0, 0)),
                  pl.BlockSpec((1, IN_W), lambda b_, s: (0, 0))],
        out_specs=([q_spec] + [pl.BlockSpec((1, ts, wd), lambda b_, s: (b_, s, 0)) for wd, _ in widths]
                   + [gt_spec]),
        out_shape=([jax.ShapeDtypeStruct((B, HPG, S, LANES), BF16)]
                   + [jax.ShapeDtypeStruct((B, S, wd), dt) for wd, dt in widths]
                   + [jax.ShapeDtypeStruct((B, 2 * ML_HEADS, S), F32)]),
        compiler_params=_cparams(("arbitrary", "arbitrary")),
        name="inproj",
    )(x, mod, norm_g.reshape(1, D_MODEL), w, b)


def _prep_compress(pe, w1, w2):
    w1r = w1.reshape(2, CMP_STRIDE, HEAD_DIM, CMP_HIDDEN)
    z = jnp.zeros_like(w1r[0])
    cols = []
    for g in range(N_GROUPS):
        for a in range(2):
            parts = [w1r[a] if gg == g else z for gg in range(N_GROUPS)]
            cols.append(jnp.concatenate(parts, axis=1).reshape(CMP_STRIDE * LANES, CMP_HIDDEN))
    wk = jnp.concatenate(cols, axis=1)
    per = pe.reshape(2, CMP_STRIDE, HEAD_DIM)
    pe_rows = jnp.concatenate([per, per], axis=-1).reshape(2, CMP_STRIDE * LANES)
    pe_rows = jnp.concatenate([pe_rows, jnp.zeros((6, CMP_STRIDE * LANES), pe.dtype)], axis=0)
    z2 = jnp.zeros_like(w2)
    w2p = jnp.stack([jnp.concatenate([w2, z2], axis=1), jnp.concatenate([z2, w2], axis=1)])
    return wk.astype(BF16), pe_rows.astype(BF16), w2p.astype(BF16)


def _compress_kernel(ak_ref, av_ref, wk_ref, pe_ref, w2_ref, ok_ref, ov_ref):
    n = ok_ref.shape[1]

    def hidden(i, a_ref, g):
        a = jnp.concatenate([a_ref.at[0][pl.ds(r, n, stride=CMP_STRIDE), :] for r in range(CMP_STRIDE)], axis=1)
        y = jnp.dot(a.astype(BF16), wk_ref[i], preferred_element_type=F32)
        yp = jnp.dot(pe_ref[i], wk_ref[i], preferred_element_type=F32)
        ca, cb = 2 * g * LANES, (2 * g + 1) * LANES
        bias = yp[0:1, ca:ca + LANES] + yp[1:2, cb:cb + LANES]
        hid = y[:, ca:ca + LANES] + pltpu.roll(y[:, cb:cb + LANES], LANES - 1, 0) + bias
        return jax.nn.gelu(hid, approximate=True).astype(BF16)

    ok_ref[0] = (jnp.dot(hidden(0, ak_ref, 0), w2_ref[0, 0], preferred_element_type=F32)
                 + jnp.dot(hidden(0, ak_ref, 1), w2_ref[0, 1], preferred_element_type=F32)).astype(BF16)
    ones_hi = jnp.where(lax.broadcasted_iota(I32, (LANES, LANES), 1) >= HEAD_DIM, 1.0, 0.0)
    for g in range(N_GROUPS):
        v = jnp.dot(hidden(1, av_ref, g), w2_ref[1, 0], preferred_element_type=F32)
        ov_ref[0, g] = (v + ones_hi).astype(BF16)


def _compress(kc_in, vc_in, wk, pe, w2):
    B, S, _ = kc_in.shape
    n = S // CMP_STRIDE
    a_spec = pl.BlockSpec((1, S, LANES), lambda b_: (b_, 0, 0))
    full = lambda shp: pl.BlockSpec(shp, lambda b_: (0,) * len(shp))
    return pl.pallas_call(
        _compress_kernel,
        grid=(B,),
        in_specs=[a_spec, a_spec, full(wk.shape), full(pe.shape), full(w2.shape)],
        out_specs=[pl.BlockSpec((1, n, LANES), lambda b_: (b_, 0, 0)),
                   pl.BlockSpec((1, N_GROUPS, n, LANES), lambda b_: (b_, 0, 0, 0))],
        out_shape=[jax.ShapeDtypeStruct((B, n, LANES), BF16), jax.ShapeDtypeStruct((B, N_GROUPS, n, LANES), BF16)],
        compiler_params=_cparams(("arbitrary",)),
        name="compress",
    )(kc_in, vc_in, wk, pe, w2)


def _nsa_consts(S):
    n_cmp = (S - CMP_BLOCK) // CMP_STRIDE + 1
    n_slc = S // SEL_BLOCK
    cs = np.arange(n_cmp)[:, None] * CMP_STRIDE
    ss = np.arange(n_slc)[None, :] * SEL_BLOCK
    ov = np.clip(np.minimum(cs + CMP_BLOCK, ss + SEL_BLOCK) - np.maximum(cs, ss), 0, None) / CMP_BLOCK
    ovp = np.zeros((LANES, LANES), np.float32)
    ovp[:n_cmp, :n_slc] = ov
    npair = S // (2 * LANES)
    ex = np.zeros((npair, LANES, 2 * LANES), np.float32)
    for kp in range(npair):
        for l in range(2 * LANES):
            ex[kp, (kp * 2 * LANES + l) // SEL_BLOCK, l] = 1.0
    return jnp.asarray(ovp.T, BF16), jnp.asarray(ex, BF16)


M_FLOOR = 0.5 * NEG


def _nsa_kernel(q_ref, kc_ref, vc_ref, ks_ref, kw_ref, vs0_ref, vs1_ref, vw0_ref, vw1_ref, gate_ref, ovt_ref, ex_ref,
                 o_ref, s_sc, mx_sc, mb_sc, acc_sc, out_sc, *, tq):
    t0 = pl.program_id(1) * tq
    n_slc = ks_ref.shape[1] // SEL_BLOCK
    tpos = t0 + lax.broadcasted_iota(I32, (tq, LANES), 0)
    lane = lax.broadcasted_iota(I32, (tq, LANES), 1)
    lane_row = lax.broadcasted_iota(I32, (1, LANES), 1)
    slopes = [2.0 ** -(h + 1) for h in range(N_HEADS)]
    rows = [slice(h * tq, (h + 1) * tq) for h in range(N_HEADS)]
    grp = [h // HPG for h in range(N_HEADS)]
    vs_refs, vw_refs = (vs0_ref, vs1_ref), (vw0_ref, vw1_ref)
    nt = (((1,), (1,)), ((), ()))
    half = [jnp.where((lane < HEAD_DIM) == (g == 0), ATTN_SCALE, 0.0).astype(BF16) for g in range(N_GROUPS)]
    q8 = jnp.concatenate([q_ref[0, h % HPG] * half[grp[h]] for h in range(N_HEADS)], axis=0)
    gates = jax.nn.sigmoid(gate_ref[0])

    def inv_denom(pv):
        return 1.0 / jnp.maximum(jnp.where(lane < HEAD_DIM, pltpu.roll(pv, HEAD_DIM, 1), pv), 1e-30)

    def gated(br, h, pv):
        c = br * N_HEADS + h
        return jnp.broadcast_to(gates[:, c:c + 1], (tq, LANES)) * (pv * inv_denom(pv))

    s_all = lax.dot_general(q8, kc_ref[0], nt, preferred_element_type=F32)
    cmask = jnp.where(lane * CMP_STRIDE + (CMP_BLOCK - 1) <= tpos, 0.0, NEG)
    crel = (lane_row * CMP_STRIDE - t0).astype(F32)
    psum = [jnp.zeros((tq, LANES), F32) for _ in range(N_GROUPS)]
    for h in range(N_HEADS):
        s = s_all[rows[h]] + (cmask + slopes[h] * crel)
        m = jnp.maximum(jnp.max(s, axis=-1, keepdims=True), M_FLOOR)
        p = jnp.exp(s - m)
        pv = jnp.dot(p.astype(BF16), vc_ref[0, grp[h]], preferred_element_type=F32)
        psum[grp[h]] = psum[grp[h]] + p * inv_denom(pv)
        out_sc[rows[h], :] = gated(0, h, pv)

    blk = lax.broadcasted_iota(I32, (n_slc, tq), 0)
    cur = (t0 + lax.broadcasted_iota(I32, (n_slc, tq), 1)) // SEL_BLOCK
    blkf = blk.astype(F32)
    sel = []
    for g in range(N_GROUPS):
        ph, plo = _split_bf16(psum[g])
        imp = (lax.dot_general(ovt_ref[...], ph, nt, preferred_element_type=F32)
               + lax.dot_general(ovt_ref[...], plo, nt, preferred_element_type=F32))[0:n_slc]
        sc = jnp.where((blk == cur) | (blk == 0), -NEG, jnp.where(blk > cur, NEG, imp))
        selt = jnp.zeros((n_slc, tq), F32)
        for _ in range(SEL_TOPN):
            mx = jnp.max(sc, axis=0, keepdims=True)
            idx = jnp.min(jnp.where(sc == mx, blkf, 1e9), axis=0, keepdims=True)
            hit = blkf == idx
            selt = jnp.where(hit, 1.0, selt)
            sc = jnp.where(hit, -3e38, sc)
        selt = jnp.where(blk <= cur, selt, 0.0)
        sel.append(jnp.transpose(jnp.concatenate([selt, jnp.zeros((LANES - n_slc, tq), F32)], axis=0)).astype(BF16))

    KT = NSA_KCHUNK

    def scores(slots, k2, kpos, masks):
        s2 = lax.dot_general(q8, k2, nt, preferred_element_type=F32)
        rel2 = jnp.concatenate([(kp + lane_row - t0).astype(F32) for kp in kpos], axis=1)
        for h in range(N_HEADS):
            s = s2[rows[h]] + (masks[grp[h]] + slopes[h] * rel2)
            for j in range(2):
                s_sc[slots[j], rows[h], :] = s[:, j * LANES:(j + 1) * LANES]
            mx_sc[rows[h], :] = jnp.maximum(mx_sc[rows[h], :], jnp.maximum(s[:, :LANES], s[:, LANES:]))

    def finish_max():
        for h in range(N_HEADS):
            m = jnp.maximum(jnp.max(mx_sc[rows[h], :], axis=-1, keepdims=True), M_FLOOR)
            mb_sc[rows[h], :] = jnp.broadcast_to(m, (tq, LANES))

    def weighted(c, v_c):
        for h in range(N_HEADS):
            s4 = jnp.concatenate([s_sc[KT * c + j, rows[h], :] for j in range(KT)], axis=1)
            mb = mb_sc[rows[h], :]
            p = jnp.exp(s4 - jnp.concatenate([mb] * KT, axis=1)).astype(BF16)
            acc_sc[rows[h], :] += jnp.dot(p, v_c[grp[h]], preferred_element_type=F32)

    mx_sc[...] = jnp.full(mx_sc.shape, NEG, F32)
    acc_sc[...] = jnp.zeros(acc_sc.shape, F32)
    n_chunks = (t0 + tq + KT * LANES - 1) // (KT * LANES)
    tpos2 = t0 + lax.broadcasted_iota(I32, (tq, 2 * LANES), 0)
    lane2 = lax.broadcasted_iota(I32, (tq, 2 * LANES), 1)

    def slc_scores(kp, carry):
        r0 = pl.multiple_of(kp * 2 * LANES, 2 * LANES)
        causal2 = r0 + lane2 <= tpos2
        masks = [jnp.where((jnp.dot(sel[g], ex_ref[kp], preferred_element_type=F32) > 0.5) & causal2, 0.0, NEG)
                 for g in range(N_GROUPS)]
        scores((2 * kp, 2 * kp + 1), ks_ref[0, pl.ds(r0, 2 * LANES), :], (r0, r0 + LANES), masks)
        return carry

    def slc_scores_chunk(c, carry):
        for j in range(KT // 2):
            carry = slc_scores(c * (KT // 2) + j, carry)
        return carry

    lax.fori_loop(0, n_chunks, slc_scores_chunk, 0)
    finish_max()

    def slc_weighted(c, carry):
        r0 = pl.multiple_of(c * KT * LANES, KT * LANES)
        weighted(c, [r[0, pl.ds(r0, KT * LANES), :] for r in vs_refs])
        return carry

    lax.fori_loop(0, n_chunks, slc_weighted, 0)
    for h in range(N_HEADS):
        out_sc[rows[h], :] += gated(1, h, acc_sc[rows[h], :])

    mx_sc[...] = jnp.full(mx_sc.shape, NEG, F32)
    acc_sc[...] = jnp.zeros(acc_sc.shape, F32)
    v_tiles = [[] for _ in range(N_GROUPS)]
    for half in range(KT // 2):
        k_parts, kpos, masks = [], [], []
        for r in (2 * half, 2 * half + 1):
            kt = t0 // LANES - WINDOW // LANES + r
            r0 = pl.multiple_of(jnp.maximum(kt, 0) * LANES, LANES)
            spos = kt * LANES + lane
            dist = tpos - spos
            masks.append(jnp.where((dist >= 0) & (dist < WINDOW) & (spos >= 0), 0.0, NEG))
            k_parts.append(kw_ref[0, pl.ds(r0, LANES), :])
            kpos.append(kt * LANES)
            for g in range(N_GROUPS):
                v_tiles[g].append(vw_refs[g][0, pl.ds(r0, LANES), :])
        m2 = jnp.concatenate(masks, axis=1)
        scores((2 * half, 2 * half + 1), jnp.concatenate(k_parts, axis=0), kpos, [m2] * N_GROUPS)
    finish_max()
    weighted(0, [jnp.concatenate(v_tiles[g], axis=0) for g in range(N_GROUPS)])
    for h in range(N_HEADS):
        o = out_sc[rows[h], :] + gated(2, h, acc_sc[rows[h], :])
        o_ref[0, :, h * LANES:(h + 1) * LANES] = o.astype(BF16)


def _nsa(q, kc, vc, kv6, small, consts):
    ovt, ex = consts
    B, _, S, _ = q.shape
    tq = NSA_TQ
    seq = lambda j: pl.BlockSpec((1, S, LANES), lambda b_, i: (b_, 0, j))
    nrow = N_HEADS * tq
    return pl.pallas_call(
        functools.partial(_nsa_kernel, tq=tq),
        grid=(B, S // tq),
        in_specs=[pl.BlockSpec((1, HPG, tq, LANES), lambda b_, i: (b_, 0, i, 0)),
                  pl.BlockSpec((1, LANES, LANES), lambda b_, i: (b_, 0, 0)),
                  pl.BlockSpec((1, N_GROUPS, LANES, LANES), lambda b_, i: (b_, 0, 0, 0)),
                  seq(0), seq(1), seq(2), seq(3), seq(4), seq(5),
                  pl.BlockSpec((1, tq, LANES), lambda b_, i: (b_, i, 0)),
                  pl.BlockSpec(ovt.shape, lambda b_, i: (0, 0)),
                  pl.BlockSpec(ex.shape, lambda b_, i: (0, 0, 0))],
        out_specs=pl.BlockSpec((1, tq, QP_W), lambda b_, i: (b_, i, 0)),
        out_shape=jax.ShapeDtypeStruct((B, S, QP_W), BF16),
        scratch_shapes=[pltpu.VMEM((S // LANES, nrow, LANES), F32)] + [pltpu.VMEM((nrow, LANES), F32)] * 4,
        compiler_params=_cparams(("arbitrary", "arbitrary")),
        name="nsa",
    )(q, kc, vc, kv6, kv6, kv6, kv6, kv6, kv6, small, ovt, ex)


NSA_KCHUNK = 4
NSA_TQ = (NSA_KCHUNK - WINDOW // LANES) * LANES


def _split_bf16(a):
    hi = a.astype(BF16)
    return hi, (a - hi.astype(F32)).astype(BF16)


def _mlstm_kernel(xm_ref, op_ref, gate_ref, cw_ref, cb_ref, wq_ref, wk_ref, wkt_ref, wv_ref, fb_ref, ng_ref,
                  y_ref, xpad_sc, xc_sc, c_sc, m_sc, q_sc, k_sc, v_sc, kt_sc, fr_sc, fc_sc, *, S):
    L = ML_CHUNK
    nt = (((1,), (1,)), ((), ()))
    xpad_sc[0:8, :] = jnp.zeros((8, ML_WIDTH), F32)
    xpad_sc[8:, :] = xm_ref[0]
    rb = 256
    for r in range(S // rb):
        acc = jnp.zeros((rb, ML_WIDTH), F32) + cb_ref[...]
        for j in range(CONV_W):
            o = 8 + r * rb - (CONV_W - 1) + j
            acc = acc + cw_ref[j:j + 1, :] * xpad_sc[o:o + rb, :]
        xc_sc[r * rb:(r + 1) * rb, :] = (acc * jax.nn.sigmoid(acc)).astype(BF16)

    c_sc[...] = jnp.zeros(c_sc.shape, F32)
    m_sc[...] = jnp.zeros(m_sc.shape, F32)
    row = lax.broadcasted_iota(I32, (L, L), 0)
    col = lax.broadcasted_iota(I32, (L, L), 1)
    causal = col <= row
    triu = jnp.where(row <= col, 1.0, 0.0).astype(BF16)
    ones = jnp.ones((L, L), BF16)
    rs = 1.0 / math.sqrt(ML_DH)
    nc = S // L

    for hh in range(ML_HEADS):
        hs = slice(hh * ML_DH, (hh + 1) * ML_DH)
        xc = xc_sc[:, hs]
        q_sc[:, hs] = jnp.dot(xc, wq_ref[hh], preferred_element_type=F32).astype(BF16)
        k_sc[:, hs] = (jnp.dot(xc, wk_ref[hh], preferred_element_type=F32) * rs).astype(BF16)
        v_sc[:, hs] = jnp.dot(xm_ref[0, :, hs].astype(BF16), wv_ref[hh], preferred_element_type=F32).astype(BF16)
        kt = lax.dot_general(wkt_ref[hh], xc, nt, preferred_element_type=F32) * rs
        for c in range(nc):
            kt_sc[c, hh] = kt[:, c * L:(c + 1) * L]

    def gate_sums(c, carry):
        for hh in range(ML_HEADS):
            z = gate_ref[0, ML_HEADS + hh, pl.ds(c, 1), :] + fb_ref[hh:hh + 1, :]
            lf_row = jnp.minimum(z, 0.0) - jnp.log(1.0 + jnp.exp(-jnp.abs(z)))
            rh, rl = _split_bf16(jnp.broadcast_to(lf_row, (8, L)))
            fr_sc[c, hh] = (jnp.dot(rh, triu, preferred_element_type=F32)
                            + jnp.dot(rl, triu, preferred_element_type=F32))
            lh, ll = _split_bf16(jnp.where(causal, jnp.broadcast_to(lf_row, (L, L)), 0.0))
            fc_sc[c, hh] = (jnp.dot(lh, ones, preferred_element_type=F32)
                            + jnp.dot(ll, ones, preferred_element_type=F32))
        return carry

    lax.fori_loop(0, nc, gate_sums, 0)

    def chunk(c, carry):
        r0 = pl.multiple_of(c * L, L)
        for hh in range(ML_HEADS):
            hs = slice(hh * ML_DH, (hh + 1) * ML_DH)
            q = q_sc[pl.ds(r0, L), hs]
            k = k_sc[pl.ds(r0, L), hs]
            kt = kt_sc[c, hh]
            vaug = jnp.concatenate([v_sc[pl.ds(r0, L), hs], ones], axis=1)
            i_row = gate_ref[0, hh, pl.ds(c, 1), :]
            f_r = fr_sc[c, hh, 0:1, :]
            f_c = fc_sc[c, hh]
            m_prev = m_sc[hh, 0:1, :]
            logd = jnp.where(causal, f_c - f_r + i_row, NEG)
            inter = f_c + m_prev
            m_t = jnp.maximum(inter, jnp.max(logd, axis=-1, keepdims=True))
            dm = jnp.where(causal, jnp.exp(logd - m_t), 0.0)
            wi = jnp.exp(inter - m_t)
            qk = (lax.dot_general(q, k, nt, preferred_element_type=F32) * dm).astype(BF16)
            caug = c_sc[hh]
            qc = jnp.dot(q, caug.astype(BF16), preferred_element_type=F32)
            qv = jnp.dot(qk, vaug, preferred_element_type=F32)
            num = wi * qc[:, :L] + qv[:, :L]
            den = wi * qc[:, L:] + qv[:, L:]
            h = num / jnp.maximum(jnp.abs(den), jnp.exp(-m_t))
            mu = jnp.mean(h, axis=-1, keepdims=True)
            hc = h - mu
            var = jnp.mean(hc * hc, axis=-1, keepdims=True)
            hn = hc * lax.rsqrt(var + LN_EPS) * ng_ref[:, hs]
            o = op_ref[0, pl.ds(r0, L), hs].astype(F32)
            y_ref[0, pl.ds(r0, L), hs] = (jax.nn.sigmoid(o) * hn).astype(BF16)

            f_l = f_c[L - 1:L, :]
            logw = f_l - f_r + i_row
            m_new = jnp.maximum(f_l + m_prev, jnp.max(logw, axis=-1, keepdims=True))
            decay = jnp.exp(f_l + m_prev - m_new)
            w_row = jnp.exp(logw - m_new)
            upd = jnp.dot((kt * w_row).astype(BF16), vaug, preferred_element_type=F32)
            c_sc[hh] = jnp.concatenate([decay, decay], axis=1) * caug + upd
            m_sc[hh] = jnp.broadcast_to(m_new, (8, L))
        return carry

    def chunk_pair(j, carry):
        return chunk(2 * j + 1, chunk(2 * j, carry))

    lax.fori_loop(0, nc // 2, chunk_pair, 0)


def _mlstm(xm, op, gates, conv_w, conv_b, wq, wk, wkt, wv, fb, ng):
    B, S, _ = xm.shape
    full = lambda a: pl.BlockSpec(a.shape, lambda b_: (0,) * a.ndim)
    seq = pl.BlockSpec((1, S, ML_WIDTH), lambda b_: (b_, 0, 0))
    return pl.pallas_call(
        functools.partial(_mlstm_kernel, S=S),
        grid=(B,),
        in_specs=[seq, seq, pl.BlockSpec((1,) + gates.shape[1:], lambda b_: (b_, 0, 0, 0)),
                  full(conv_w), full(conv_b), full(wq), full(wk), full(wkt), full(wv), full(fb), full(ng)],
        out_specs=seq,
        out_shape=jax.ShapeDtypeStruct((B, S, ML_WIDTH), BF16),
        scratch_shapes=[pltpu.VMEM((S + 8, ML_WIDTH), F32), pltpu.VMEM((S, ML_WIDTH), BF16),
                        pltpu.VMEM((ML_HEADS, ML_DH, 2 * ML_DH), F32), pltpu.VMEM((ML_HEADS, 8, ML_CHUNK), F32)]
                       + [pltpu.VMEM((S, ML_WIDTH), BF16)] * 3
                       + [pltpu.VMEM((S // ML_CHUNK, ML_HEADS, ML_DH, ML_CHUNK), F32),
                          pltpu.VMEM((S // ML_CHUNK, ML_HEADS, 8, ML_CHUNK), F32),
                          pltpu.VMEM((S // ML_CHUNK, ML_HEADS, ML_CHUNK, ML_CHUNK), F32)],
        compiler_params=_cparams(("arbitrary",)),
        name="mlstm",
    )(xm, op, gates, conv_w, conv_b, wq, wk, wkt, wv, fb, ng)


ROW_TILE = D_MODEL // LANES


def _store_token_tiles(ref, val):
    n = val.shape[0]
    for s in range(ROW_TILE):
        ref[pl.ds(s, n, stride=ROW_TILE), :] = val[:, s * LANES:(s + 1) * LANES]


def _load_token_tiles(ref, n):
    return jnp.concatenate([ref[pl.ds(s, n, stride=ROW_TILE), :] for s in range(ROW_TILE)], axis=1)


def _prep_proj_a(proj_a):
    pa = proj_a.reshape(N_HEADS, HEAD_DIM, D_MODEL)
    return jnp.concatenate([pa, jnp.zeros_like(pa)], axis=1).reshape(QP_W, D_MODEL).astype(BF16)


def _mix_kernel(o_ref, y_ref, gm_ref, x_ref, mod_ref, pa_ref, pb_ref, wo_ref, g2_ref, rwh_ref, rwl_ref, rb_ref,
                x1_ref, h2_ref, e_ref, w_ref, *, ts):
    nt = (((1,), (1,)), ((), ()))
    a = jnp.dot(o_ref[0], pa_ref[...], preferred_element_type=F32)
    bm = jnp.dot(y_ref[0], pb_ref[...], preferred_element_type=F32)
    ga = jax.nn.sigmoid(gm_ref[0, :, :D_MODEL].astype(F32))
    gb = jax.nn.sigmoid(gm_ref[0, :, D_MODEL:].astype(F32))
    mixed = jnp.dot((ga * a + gb * bm).astype(BF16), wo_ref[...], preferred_element_type=F32)
    x1 = x_ref[0] + mod_ref[0, 2:3, :] * mixed
    x1_ref[0] = x1
    ms = jnp.mean(x1 * x1, axis=-1, keepdims=True)
    h2 = x1 * lax.rsqrt(ms + RMS_EPS) * g2_ref[...] * (1.0 + mod_ref[0, 4:5, :]) + mod_ref[0, 3:4, :]
    _store_token_tiles(h2_ref, h2)
    hh, hl = _split_bf16(h2)
    logit = (lax.dot_general(rwh_ref[...], hh, nt, preferred_element_type=F32)
             + lax.dot_general(rwh_ref[...], hl, nt, preferred_element_type=F32)
             + lax.dot_general(rwl_ref[...], hh, nt, preferred_element_type=F32)) + rb_ref[...]
    eio = lax.broadcasted_iota(I32, (N_EXPERTS, ts), 0).astype(F32)
    vals, idxs = [], []
    for _ in range(TOP_K):
        mx = jnp.max(logit, axis=0, keepdims=True)
        ix = jnp.min(jnp.where(logit == mx, eio, 1e9), axis=0, keepdims=True)
        vals.append(mx)
        idxs.append(ix)
        logit = jnp.where(eio == ix, -3e38, logit)
    ex = [jnp.exp(v - vals[0]) for v in vals]
    tot = ex[0] + ex[1] + ex[2] + ex[3]
    e_ref[...] = jnp.concatenate(idxs, axis=0).astype(I32)
    wrows = jnp.concatenate([e_ / tot for e_ in ex] + [jnp.zeros((LANES - TOP_K, ts), F32)], axis=0)
    w_ref[...] = jnp.transpose(wrows)


def _mix(o_nsa, y_ml, gm, x, mod, pa, pb, wo, g2, rwh, rwl, rb, ts=512):
    B, S, _ = x.shape
    T = B * S
    nst = S // ts
    blk = lambda wd: pl.BlockSpec((1, ts, wd), lambda b_, s: (b_, s, 0))
    full = lambda a: pl.BlockSpec(a.shape, lambda b_, s: (0,) * a.ndim)
    return pl.pallas_call(
        functools.partial(_mix_kernel, ts=ts),
        grid=(B, nst),
        in_specs=[blk(QP_W), blk(ML_WIDTH), blk(2 * D_MODEL), blk(D_MODEL),
                  pl.BlockSpec((1, 6, D_MODEL), lambda b_, s: (b_, 0, 0)),
                  full(pa), full(pb), full(wo), full(g2), full(rwh), full(rwl), full(rb)],
        out_specs=[blk(D_MODEL), pl.BlockSpec((ts * ROW_TILE, LANES), lambda b_, s: (b_ * nst + s, 0)),
                   pl.BlockSpec((TOP_K, ts), lambda b_, s: (0, b_ * nst + s)),
                   pl.BlockSpec((ts, LANES), lambda b_, s: (b_ * nst + s, 0))],
        out_shape=[jax.ShapeDtypeStruct((B, S, D_MODEL), F32), jax.ShapeDtypeStruct((T * ROW_TILE, LANES), F32),
                   jax.ShapeDtypeStruct((TOP_K, T), I32), jax.ShapeDtypeStruct((T, LANES), F32)],
        compiler_params=_cparams(("arbitrary", "arbitrary")),
        name="mix_router",
    )(o_nsa, y_ml, gm, x, mod, pa, pb, wo, g2, rwh, rwl, rb)


MOE_TM = 256


def _moe_tiles(T):
    r = T * TOP_K
    return (r + N_EXPERTS * (MOE_TM - 1) + MOE_TM - 1) // MOE_TM


def _route_kernel(e_ref, dest_ref, te_ref, na_ref, *, ntp):
    nrow = e_ref.shape[0]
    eio = lax.broadcasted_iota(I32, (N_EXPERTS, LANES), 0)
    row = lax.broadcasted_iota(I32, (LANES, LANES), 0)
    col = lax.broadcasted_iota(I32, (LANES, LANES), 1)
    triu = jnp.where(row <= col, 1.0, 0.0).astype(BF16)
    ones = jnp.ones((LANES, LANES), BF16)

    ru = 8

    def rank_body(i, counts):
        r0 = pl.multiple_of(i * ru, ru)
        e_blk = e_ref[pl.ds(r0, ru), :]
        ranks = []
        for r in range(ru):
            oh = eio == e_blk[r:r + 1, :]
            ohb = jnp.where(oh, 1.0, 0.0).astype(BF16)
            cum = jnp.dot(ohb, triu, preferred_element_type=F32)
            ranks.append(jnp.sum(jnp.where(oh, cum - 1.0 + counts, 0.0), axis=0, keepdims=True))
            counts = counts + jnp.dot(ohb, ones, preferred_element_type=F32)
        dest_ref[pl.ds(r0, ru), :] = jnp.concatenate(ranks, axis=0).astype(I32)
        return counts

    counts = lax.fori_loop(0, nrow // ru, rank_body, jnp.zeros((N_EXPERTS, LANES), F32)).astype(I32)
    tiles = (counts + (MOE_TM - 1)) // MOE_TM
    incl = tiles
    for s in (1, 2, 4, 8, 16):
        incl = incl + jnp.where(eio >= s, pltpu.roll(incl, s, 0), 0)
    off = (incl - tiles) * MOE_TM

    def dest_body(i, carry):
        r0 = pl.multiple_of(i * ru, ru)
        e_blk = e_ref[pl.ds(r0, ru), :]
        base = [jnp.sum(jnp.where(eio == e_blk[r:r + 1, :], off, 0), axis=0, keepdims=True) for r in range(ru)]
        dest_ref[pl.ds(r0, ru), :] = (dest_ref[pl.ds(r0, ru), :] + jnp.concatenate(base, axis=0)) * ROW_TILE
        return carry

    lax.fori_loop(0, nrow // ru, dest_body, 0)
    ends = jnp.concatenate([incl] * (ntp // LANES), axis=1)
    tix = lax.broadcasted_iota(I32, (N_EXPERTS, ntp), 1)
    te = jnp.sum(jnp.where(ends <= tix, 1, 0), axis=0, keepdims=True)
    te_ref[...] = jnp.minimum(te, N_EXPERTS - 1)
    na_ref[...] = jnp.concatenate([incl, counts, off], axis=0)


def _route(e_t):
    T = e_t.shape[1]
    nrow = TOP_K * T // LANES
    ntp = -(-_moe_tiles(T) // LANES) * LANES
    dest, te, meta = pl.pallas_call(
        functools.partial(_route_kernel, ntp=ntp),
        out_shape=[jax.ShapeDtypeStruct((nrow, LANES), I32), jax.ShapeDtypeStruct((1, ntp), I32),
                   jax.ShapeDtypeStruct((3 * N_EXPERTS, LANES), I32)],
        compiler_params=pltpu.CompilerParams(vmem_limit_bytes=VMEM_LIMIT),
        name="route",
    )(e_t.reshape(nrow, LANES))
    return dest.reshape(TOP_K, T), te.reshape(ntp), meta[:, 0]


MOE_TT = 512


def _row_copy(src_ref, s8, dst_ref, d8, sem):
    src = src_ref.at[pl.ds(pl.multiple_of(s8, ROW_TILE), ROW_TILE), :]
    dst = dst_ref.at[pl.ds(pl.multiple_of(d8, ROW_TILE), ROW_TILE), :]
    return pltpu.make_async_copy(src, dst, sem)


MOE_UNROLL = 4


def _dispatch_kernel(meta_ref, dest_ref, h_ref, xs_ref, zero_sc, sem, zsem, *, n_tiles):
    @pl.when(pl.program_id(0) == 0)
    def _():
        zero_sc[...] = jnp.zeros(zero_sc.shape, F32)

        def pad_range(e):
            lo = meta_ref[2 * N_EXPERTS + e] + meta_ref[N_EXPERTS + e]
            hi = meta_ref[e] * MOE_TM
            return lo, hi

        def fill_expert(e, carry):
            def fill(p, c):
                _row_copy(zero_sc, 0, xs_ref, p * ROW_TILE, zsem).start()
                return c
            return lax.fori_loop(*pad_range(e), fill, carry)

        def drain_expert(e, carry):
            def drain(p, c):
                _row_copy(zero_sc, 0, xs_ref, p * ROW_TILE, zsem).wait()
                return c
            return lax.fori_loop(*pad_range(e), drain, carry)

        lax.fori_loop(0, N_EXPERTS, fill_expert, 0)
        lax.fori_loop(0, N_EXPERTS, drain_expert, 0)

        def tile_fill(t, carry):
            r0 = pl.multiple_of(t * (MOE_TM * ROW_TILE), MOE_TM * ROW_TILE)
            cp = pltpu.make_async_copy(zero_sc, xs_ref.at[pl.ds(r0, MOE_TM * ROW_TILE), :], zsem)
            cp.start()
            cp.wait()
            return carry

        lax.fori_loop(meta_ref[N_EXPERTS - 1], n_tiles, tile_fill, 0)

    def issue(j, carry):
        i0 = j * MOE_UNROLL
        r0 = i0 * ROW_TILE
        for u in range(MOE_UNROLL):
            for k in range(TOP_K):
                _row_copy(h_ref, r0 + u * ROW_TILE, xs_ref, dest_ref[k, i0 + u], sem).start(priority=k % 2)
        return carry

    lax.fori_loop(0, MOE_TT // MOE_UNROLL, issue, 0)
    for k in range(TOP_K):
        pltpu.make_async_copy(h_ref, xs_ref.at[pl.ds(0, MOE_TT * ROW_TILE), :], sem).wait()


def _dispatch(h2, dest, meta, n_tiles):
    T = h2.shape[0] // ROW_TILE
    grid_spec = pltpu.PrefetchScalarGridSpec(
        num_scalar_prefetch=1,
        grid=(T // MOE_TT,),
        in_specs=[pl.BlockSpec((TOP_K, MOE_TT), lambda i, m: (0, i), memory_space=pltpu.SMEM),
                  pl.BlockSpec((MOE_TT * ROW_TILE, LANES), lambda i, m: (i, 0))],
        out_specs=pl.BlockSpec(memory_space=pl.ANY),
        scratch_shapes=[pltpu.VMEM((MOE_TM * ROW_TILE, LANES), F32), pltpu.SemaphoreType.DMA,
                        pltpu.SemaphoreType.DMA],
    )
    return pl.pallas_call(
        functools.partial(_dispatch_kernel, n_tiles=n_tiles),
        grid_spec=grid_spec,
        out_shape=jax.ShapeDtypeStruct((n_tiles * MOE_TM * ROW_TILE, LANES), F32),
        compiler_params=_cparams(("arbitrary",)),
        name="dispatch",
    )(meta, dest, h2)


def _ffn_kernel(te_ref, na_ref, ends_ref, x_ref, wu_hbm, bu_ref, wd_hbm, bd_ref, y_ref,
                wu_f32, wd_f32, wu_sc, wd_sc, slot_ref, sems):
    n_act = na_ref[0]
    rows = MOE_TM * ROW_TILE

    def fetch(expert, slot):
        return (pltpu.make_async_copy(wu_hbm.at[expert], wu_f32.at[slot], sems.at[0, slot]),
                pltpu.make_async_copy(wd_hbm.at[expert], wd_f32.at[slot], sems.at[1, slot]))

    for part in range(FFN_TILES):
        t = pl.program_id(0) * FFN_TILES + part
        e = te_ref[t]
        fresh = jnp.logical_and(t < n_act, jnp.logical_or(t == 0, e != te_ref[jnp.maximum(t - 1, 0)]))
        x_part = x_ref.at[part * rows:(part + 1) * rows, :]
        y_part = y_ref.at[part * rows:(part + 1) * rows, :]

        @pl.when(t == 0)
        def _():
            slot_ref[0] = 0
            for cp in fetch(e, 0):
                cp.start()

        @pl.when(fresh)
        def _():
            slot = slot_ref[0]
            nxt = ends_ref[e]

            @pl.when(nxt < n_act)
            def _():
                for cp in fetch(te_ref[nxt], 1 - slot):
                    cp.start()

            for cp in fetch(e, slot):
                cp.wait()
            wu_sc[...] = wu_f32[slot].astype(BF16)
            wd_sc[...] = wd_f32[slot].astype(BF16)
            slot_ref[0] = 1 - slot

        @pl.when(t < n_act)
        def _():
            x = _load_token_tiles(x_part, MOE_TM).astype(BF16)
            gu = jnp.dot(x, wu_sc[...], preferred_element_type=F32) + bu_ref[e]
            gl = jnp.minimum(gu[:, :D_MODEL], SWIGLU_LIMIT)
            lin = jnp.clip(gu[:, D_MODEL:], -SWIGLU_LIMIT, SWIGLU_LIMIT)
            a = gl * jax.nn.sigmoid(SWIGLU_ALPHA * gl) * (lin + 1.0)
            _store_token_tiles(y_part, jnp.dot(a.astype(BF16), wd_sc[...], preferred_element_type=F32) + bd_ref[e])

        @pl.when(t >= n_act)
        def _():
            y_part[...] = jnp.zeros(y_part.shape, F32)


FFN_TILES = 4


def _ffn(xs, te, na, ends, w_up, b_up, w_down, b_down):
    n_rows = xs.shape[0] // ROW_TILE
    nt = n_rows // MOE_TM
    assert nt % FFN_TILES == 0
    de = w_up.shape[2]
    blk = FFN_TILES * MOE_TM * ROW_TILE
    row_spec = pl.BlockSpec((blk, LANES), lambda i, te_, na_, en_: (i, 0))
    x_spec = pl.BlockSpec((blk, LANES), lambda i, te_, na_, en_: (jnp.minimum(i, (na_[0] - 1) // FFN_TILES), 0))
    grid_spec = pltpu.PrefetchScalarGridSpec(
        num_scalar_prefetch=3,
        grid=(nt // FFN_TILES,),
        in_specs=[x_spec,
                  pl.BlockSpec(memory_space=pl.ANY),
                  pl.BlockSpec((N_EXPERTS, 1, de), lambda i, te_, na_, en_: (0, 0, 0)),
                  pl.BlockSpec(memory_space=pl.ANY),
                  pl.BlockSpec((N_EXPERTS, 1, D_MODEL), lambda i, te_, na_, en_: (0, 0, 0))],
        out_specs=row_spec,
        scratch_shapes=[pltpu.VMEM((2, D_MODEL, de), F32), pltpu.VMEM((2, de // 2, D_MODEL), F32),
                        pltpu.VMEM((D_MODEL, de), BF16), pltpu.VMEM((de // 2, D_MODEL), BF16),
                        pltpu.SMEM((1,), I32), pltpu.SemaphoreType.DMA((2, 2))],
    )
    return pl.pallas_call(
        _ffn_kernel,
        grid_spec=grid_spec,
        out_shape=jax.ShapeDtypeStruct(xs.shape, F32),
        compiler_params=_cparams(("arbitrary",)),
        name="expert_ffn",
    )(te, na, ends, xs, w_up, b_up.reshape(N_EXPERTS, 1, de), w_down, b_down.reshape(N_EXPERTS, 1, D_MODEL))


def _combine_kernel(dest_ref, dnext_ref, y_ref, w_ref, x1_ref, mod_ref, fg_ref, o_ref, buf, sems):
    i = pl.program_id(0)
    n = pl.num_programs(0)
    slot = i % 2
    nslot = 1 - slot
    grp = 32

    def gather(d_ref, tok, tok8, s):
        for k in range(TOP_K):
            _row_copy(y_ref, d_ref[k, tok], buf.at[s, k], tok8, sems.at[s]).start(priority=k % 2)

    def retire(s):
        for k in range(TOP_K):
            pltpu.make_async_copy(y_ref.at[pl.ds(0, MOE_TT * ROW_TILE), :], buf.at[s, k], sems.at[s]).wait()

    @pl.when(i == 0)
    def _():
        def first(tok, carry):
            gather(dest_ref, tok, tok * ROW_TILE, 0)
            return carry
        lax.fori_loop(0, MOE_TT, first, 0)

    retire(slot)
    g2 = mod_ref[0, 5:6, :]
    fg = fg_ref[...]

    def body(j, carry):
        base = pl.multiple_of(j * grp, grp)
        w8 = w_ref[pl.ds(base, grp), :]
        moe = None
        for k in range(TOP_K):
            rows = jnp.concatenate(
                [buf[slot, k, pl.ds(base * ROW_TILE + s, grp, stride=ROW_TILE), :] for s in range(ROW_TILE)], axis=1)
            term = w8[:, k:k + 1] * rows
            moe = term if moe is None else moe + term
        x2 = x1_ref[pl.ds(base, grp), :] + g2 * moe
        ms = jnp.mean(x2 * x2, axis=-1, keepdims=True)
        o_ref[pl.ds(base, grp), :] = x2 * lax.rsqrt(ms + RMS_EPS) * fg
        base8 = base * ROW_TILE
        for u in range(grp):
            gather(dnext_ref, base + u, base8 + u * ROW_TILE, nslot)
        return carry

    lax.fori_loop(0, MOE_TT // grp, body, 0)

    @pl.when(i == n - 1)
    def _():
        retire(nslot)


def _combine(y, dest, wcol, x1, mod, final_g, S):
    T = x1.shape[0]
    per_b = S // MOE_TT
    n = T // MOE_TT
    return pl.pallas_call(
        _combine_kernel,
        grid=(n,),
        in_specs=[pl.BlockSpec((TOP_K, MOE_TT), lambda i: (0, i), memory_space=pltpu.SMEM),
                  pl.BlockSpec((TOP_K, MOE_TT), lambda i: (0, jnp.minimum(i + 1, n - 1)), memory_space=pltpu.SMEM),
                  pl.BlockSpec(memory_space=pl.ANY),
                  pl.BlockSpec((MOE_TT, LANES), lambda i: (i, 0)),
                  pl.BlockSpec((MOE_TT, D_MODEL), lambda i: (i, 0)),
                  pl.BlockSpec((1, 6, D_MODEL), lambda i: (i // per_b, 0, 0)),
                  pl.BlockSpec((1, D_MODEL), lambda i: (0, 0))],
        out_specs=pl.BlockSpec((MOE_TT, D_MODEL), lambda i: (i, 0)),
        out_shape=jax.ShapeDtypeStruct((T, D_MODEL), F32),
        scratch_shapes=[pltpu.VMEM((2, TOP_K, MOE_TT * ROW_TILE, LANES), F32), pltpu.SemaphoreType.DMA((2,))],
        compiler_params=_cparams(("arbitrary",)),
        name="combine",
    )(dest, dest, y, wcol, x1, mod, final_g.reshape(1, D_MODEL))


def _layer(x, c, ada_w, ada_b, norm1_g, w_in, b_in, cmp_pe_k, cmp_w1_k, cmp_w2_k, cmp_pe_v, cmp_w1_v, cmp_w2_v,
           ml_conv_w, ml_conv_b, ml_wq, ml_wk, ml_wv, ml_f_bias, ml_norm_g, proj_a, proj_b, w_out, norm2_g,
           router_w, router_b, exp_w_up, exp_b_up, exp_w_down, exp_b_down, final_g):
    B, S, D = x.shape
    T = B * S
    mod = _adaln(c, ada_w, ada_b).reshape(B, 6, D)
    w, b = _prep_w_in(w_in, b_in)
    q, kc_in, vc_in, kv6, xm, op, gm, small, gates = _inproj(x, mod, norm1_g, w, b)
    prep = [_prep_compress(cmp_pe_k, cmp_w1_k, cmp_w2_k), _prep_compress(cmp_pe_v, cmp_w1_v, cmp_w2_v)]
    wk, pe, w2 = [jnp.stack([prep[0][i], prep[1][i]]) for i in range(3)]
    kc, vc = _compress(kc_in, vc_in, wk, pe, w2)
    o_nsa = _nsa(q, kc, vc, kv6, small, _nsa_consts(S))
    gates = gates.reshape(B, 2 * ML_HEADS, S // ML_CHUNK, ML_CHUNK)
    y_ml = _mlstm(xm, op, gates, ml_conv_w, ml_conv_b.reshape(1, ML_WIDTH), ml_wq.astype(BF16), ml_wk.astype(BF16),
                  jnp.transpose(ml_wk, (0, 2, 1)).astype(BF16), ml_wv.astype(BF16),
                  jnp.broadcast_to(ml_f_bias[:, None], (ML_HEADS, LANES)), ml_norm_g.reshape(1, ML_WIDTH))
    rwt = jnp.transpose(router_w)
    rwh = rwt.astype(BF16)
    rwl = (rwt - rwh.astype(F32)).astype(BF16)
    x1, h2, e_t, wcol = _mix(o_nsa, y_ml, gm, x, mod, _prep_proj_a(proj_a), proj_b.astype(BF16), w_out.astype(BF16),
                             norm2_g.reshape(1, D), rwh, rwl, router_b.reshape(N_EXPERTS, 1))
    dest, te, meta = _route(e_t)
    ends = meta[:N_EXPERTS]
    xs = _dispatch(h2, dest, meta, _moe_tiles(T))
    y = _ffn(xs, te, ends[N_EXPERTS - 1:], ends, exp_w_up, exp_b_up, exp_w_down, exp_b_down)
    out = _combine(y, dest, wcol, x1.reshape(T, D), mod, final_g, S)
    return out.reshape(B, S, D)


def kernel(x, c, ada_w, ada_b, norm1_g, w_in, b_in, cmp_pe_k, cmp_w1_k, cmp_w2_k, cmp_pe_v, cmp_w1_v, cmp_w2_v, ml_conv_w, ml_conv_b, ml_wq, ml_wk, ml_wv, ml_f_bias, ml_norm_g, proj_a, proj_b, w_out, norm2_g, router_w, router_b, exp_w_up, exp_b_up, exp_w_down, exp_b_down, final_g):
    assert ada_w.shape[0] == 1, "one layer: the final RMSNorm is fused into the layer's last kernel"
    layer = (ada_w, ada_b, norm1_g, w_in, b_in, cmp_pe_k, cmp_w1_k, cmp_w2_k, cmp_pe_v, cmp_w1_v, cmp_w2_v,
             ml_conv_w, ml_conv_b, ml_wq, ml_wk, ml_wv, ml_f_bias, ml_norm_g, proj_a, proj_b, w_out, norm2_g,
             router_w, router_b, exp_w_up, exp_b_up, exp_w_down, exp_b_down)
    return _layer(x, c, *[p[0] for p in layer], final_g)
```
